```python
import jax, jax.numpy as jnp
from jax import lax
import numpy as np

D_MODEL = 2048
BATCH = 2
SEQ = 4096
DEPTH = 1
DEC_BATCH = 8
DEC_SEQ = 4
PAST_LEN = 16384
PAGE_SIZE = 128

ATTN_WIDTH = D_MODEL // 2
HEAD_DIM = 64
N_HEADS = ATTN_WIDTH // HEAD_DIM
DILATED_PATTERNS = ((128, 1), (512, 4), (2048, 16))
MAX_WINDOW = max(w for w, _ in DILATED_PATTERNS)
POOL_WIDTH = D_MODEL - ATTN_WIDTH
POOL_WINDOWS = (2, 4, 8, 16)
N_POOL_GROUPS = len(POOL_WINDOWS)
POOL_GROUP_WIDTH = POOL_WIDTH // N_POOL_GROUPS
POOL_HIST = max(POOL_WINDOWS) - 1
MIX_WIDTH = ATTN_WIDTH + POOL_WIDTH
IN_WIDTH = 4 * ATTN_WIDTH + 2 * POOL_WIDTH
BLOCK = 128
EPS = 1e-6
ATTN_SCALE = HEAD_DIM ** -0.5

kernel_name = "hymba_dilated_window_pool_hybrid_step"


def rms_norm(x, g):
    xf = x.astype(jnp.float32)
    y = xf * lax.rsqrt(jnp.mean(xf * xf, axis=-1, keepdims=True) + EPS)
    return (y * g.astype(jnp.float32)).astype(x.dtype)


def project(x, g, w_in):
    b, s, _ = x.shape
    z = jnp.einsum('bsd,de->bse', rms_norm(x, g), w_in)
    cuts = [ATTN_WIDTH, 2 * ATTN_WIDTH, 3 * ATTN_WIDTH, 4 * ATTN_WIDTH, 4 * ATTN_WIDTH + POOL_WIDTH]
    q, k, v, ga, u, gp = jnp.split(z, cuts, axis=-1)
    hs = (b, s, N_HEADS, HEAD_DIM)
    return q.reshape(hs), k.reshape(hs), v.reshape(hs), ga, u, gp


def band_attention(q, k, v, window_steps):
    assert window_steps <= BLOCK
    n, m, h, dh = q.shape
    nb = m // BLOCK
    qb = q.reshape(n, nb, BLOCK, h, dh)
    kb = k.reshape(n, nb, BLOCK, h, dh)
    vb = v.reshape(n, nb, BLOCK, h, dh)
    pad = ((0, 0), (1, 0), (0, 0), (0, 0), (0, 0))
    kcat = jnp.concatenate([jnp.pad(kb, pad)[:, :-1], kb], axis=2)
    vcat = jnp.concatenate([jnp.pad(vb, pad)[:, :-1], vb], axis=2)
    s = jnp.einsum('nbqhd,nbkhd->nbhqk', qb, kcat,
                   preferred_element_type=jnp.float32) * ATTN_SCALE
    qi = jnp.arange(BLOCK)[:, None]
    ki = jnp.arange(2 * BLOCK)[None, :] - BLOCK
    dist = qi - ki
    band = (dist >= 0) & (dist <= window_steps)
    blk_start = jnp.arange(nb) * BLOCK
    valid = band[None] & ((blk_start[:, None, None] + ki[None]) >= 0)
    s = jnp.where(valid[None, :, None], s, -jnp.inf)
    lse = jax.nn.logsumexp(s, axis=-1)
    p = jnp.exp(s - lse[..., None])
    o = jnp.einsum('nbhqk,nbkhd->nbqhd', p, vcat.astype(jnp.float32))
    o = o.reshape(n, m, h, dh)
    lse = jnp.transpose(lse, (0, 1, 3, 2)).reshape(n, m, h)
    return o, lse


def combine_patterns(outs, lses):
    w = jax.nn.softmax(jnp.stack(lses, axis=0), axis=0)
    return jnp.einsum('pbsh,pbshd->bshd', w, jnp.stack(outs, axis=0))


def dilated_prompt(q, k, v):
    b, s, h, dh = q.shape
    outs, lses = [], []
    for window, d in DILATED_PATTERNS:
        span = d * BLOCK
        sp = -(-s // span) * span
        mlen = sp // d
        padw = ((0, 0), (0, sp - s), (0, 0), (0, 0))

        def to_residue(t):
            t = jnp.pad(t, padw).reshape(b, mlen, d, h, dh)
            return jnp.swapaxes(t, 1, 2).reshape(b * d, mlen, h, dh)

        o, l = band_attention(to_residue(q), to_residue(k), to_residue(v), window // d)
        o = jnp.swapaxes(o.reshape(b, d, mlen, h, dh), 1, 2).reshape(b, sp, h, dh)[:, :s]
        l = jnp.swapaxes(l.reshape(b, d, mlen, h), 1, 2).reshape(b, sp, h)[:, :s]
        outs.append(o)
        lses.append(l)
    return combine_patterns(outs, lses).astype(q.dtype)


def dilated_sample(q, k_all, v_all, buf_len):
    t = q.shape[1]
    i = jnp.arange(t)
    outs, lses = [], []
    for window, d in DILATED_PATTERNS:
        j = jnp.arange(window // d + 1)
        idx = buf_len + i[:, None] - j[None, :] * d
        valid = idx >= 0
        idx_c = jnp.clip(idx, 0)
        kg = jnp.take(k_all, idx_c, axis=1)
        vg = jnp.take(v_all, idx_c, axis=1)
        s = jnp.einsum('bthd,btjhd->bhtj', q, kg,
                       preferred_element_type=jnp.float32) * ATTN_SCALE
        s = jnp.where(valid[None, None], s, -jnp.inf)
        l = jax.nn.logsumexp(s, axis=-1)
        p = jnp.exp(s - l[..., None])
        o = jnp.einsum('bhtj,btjhd->bthd', p, vg.astype(jnp.float32))
        outs.append(o)
        lses.append(jnp.swapaxes(l, 1, 2))
    return combine_patterns(outs, lses).astype(q.dtype)


def pool_mix(u_ext, pos, w_pool, pool_scale):
    b, le, c = u_ext.shape
    L = le - POOL_HIST
    uf = u_ext.astype(jnp.float32)
    csum = jnp.concatenate([jnp.zeros((b, 1, c), jnp.float32), jnp.cumsum(uf, axis=1)], axis=1)
    cur = uf[:, POOL_HIST:]
    groups = []
    for g, w in enumerate(POOL_WINDOWS):
        sl = slice(g * POOL_GROUP_WIDTH, (g + 1) * POOL_GROUP_WIDTH)
        win_sum = (csum[:, POOL_HIST + 1:POOL_HIST + 1 + L, sl]
                   - csum[:, POOL_HIST + 1 - w:POOL_HIST + 1 - w + L, sl])
        cnt = jnp.minimum(pos + 1, w).astype(jnp.float32)[None, :, None]
        groups.append(win_sum / cnt - cur[..., sl])
    pooled = jnp.stack(groups, axis=2)
    y = jnp.einsum('blgc,gcd->blgd', pooled, w_pool.astype(jnp.float32)).reshape(b, L, c)
    return (y * pool_scale.astype(jnp.float32)).astype(u_ext.dtype)


def merge(x, attn_o, ga, pool_o, gp, w_out):
    b, s, _ = x.shape
    mix = jnp.concatenate([attn_o.reshape(b, s, ATTN_WIDTH) * jax.nn.silu(ga),
                           pool_o * jax.nn.silu(gp)], axis=-1)
    return x + jnp.einsum('bse,ed->bsd', mix, w_out)


def setup_inputs(seed: int = 0) -> dict:
    key = jax.random.key(seed)
    ks = jax.random.split(key, 12)
    buf = min(MAX_WINDOW, PAST_LEN)
    f32 = jnp.float32
    x_prompt = jax.random.normal(ks[0], (BATCH, SEQ, D_MODEL), f32)
    x_sample = jax.random.normal(ks[1], (DEC_BATCH, DEC_SEQ, D_MODEL), f32)
    cache_k = jax.random.normal(ks[2], (DEPTH, DEC_BATCH, buf, N_HEADS, HEAD_DIM), f32)
    cache_v = jax.random.normal(ks[3], (DEPTH, DEC_BATCH, buf, N_HEADS, HEAD_DIM), f32)
    state_pool = jax.random.normal(ks[4], (DEPTH, DEC_BATCH, POOL_HIST, POOL_WIDTH), f32)
    norm_g = 1.0 + 0.02 * jax.random.normal(ks[5], (DEPTH, D_MODEL), f32)
    w_in = jax.random.normal(ks[6], (DEPTH, D_MODEL, IN_WIDTH), f32) * D_MODEL ** -0.5
    w_pool = jax.random.normal(ks[7], (DEPTH, N_POOL_GROUPS, POOL_GROUP_WIDTH, POOL_GROUP_WIDTH), f32) * POOL_GROUP_WIDTH ** -0.5
    pool_scale = 1.0 + 0.02 * jax.random.normal(ks[8], (DEPTH, POOL_WIDTH), f32)
    w_out = jax.random.normal(ks[9], (DEPTH, MIX_WIDTH, D_MODEL), f32) * MIX_WIDTH ** -0.5
    final_norm_g = 1.0 + 0.02 * jax.random.normal(ks[10], (D_MODEL,), f32)
    return {"x_prompt": x_prompt, "x_sample": x_sample, "cache_k": cache_k, "cache_v": cache_v,
            "state_pool": state_pool, "norm_g": norm_g, "w_in": w_in, "w_pool": w_pool,
            "pool_scale": pool_scale, "w_out": w_out, "final_norm_g": final_norm_g}


def reference(x_prompt, x_sample, cache_k, cache_v, state_pool, norm_g, w_in, w_pool,
              pool_scale, w_out, final_norm_g):
    hp, hs = x_prompt, x_sample
    s = hp.shape[1]
    t = hs.shape[1]
    pos_p = jnp.arange(s)
    pos_s = PAST_LEN + jnp.arange(t)
    buf = cache_k.shape[2]
    keep = min(MAX_WINDOW, s)
    kp, vp, pp, ksn, vsn, psn = [], [], [], [], [], []
    for l in range(DEPTH):
        q, k, v, ga, u, gp = project(hp, norm_g[l], w_in[l])
        a = dilated_prompt(q, k, v)
        u_ext = jnp.pad(u, ((0, 0), (POOL_HIST, 0), (0, 0)))
        po = pool_mix(u_ext, pos_p, w_pool[l], pool_scale[l])
        hp = merge(hp, a, ga, po, gp, w_out[l])
        kp.append(k[:, s - keep:])
        vp.append(v[:, s - keep:])
        pp.append(u_ext[:, -POOL_HIST:])
        q, k, v, ga, u, gp = project(hs, norm_g[l], w_in[l])
        k_all = jnp.concatenate([cache_k[l].astype(k.dtype), k], axis=1)
        v_all = jnp.concatenate([cache_v[l].astype(v.dtype), v], axis=1)
        a = dilated_sample(q, k_all, v_all, buf)
        u_ext = jnp.concatenate([state_pool[l].astype(u.dtype), u], axis=1)
        po = pool_mix(u_ext, pos_s, w_pool[l], pool_scale[l])
        hs = merge(hs, a, ga, po, gp, w_out[l])
        ksn.append(k_all[:, -buf:])
        vsn.append(v_all[:, -buf:])
        psn.append(u_ext[:, -POOL_HIST:])
    y_prompt = rms_norm(hp, final_norm_g)
    y_sample = rms_norm(hs, final_norm_g)
    return (y_prompt, y_sample, jnp.stack(kp), jnp.stack(vp), jnp.stack(pp),
            jnp.stack(ksn), jnp.stack(vsn), jnp.stack(psn))
```

```python
import functools

import jax
import jax.numpy as jnp
from jax import lax
from jax.experimental import pallas as pl
from jax.experimental.pallas import tpu as pltpu

F32 = jnp.float32
BF16 = jnp.bfloat16

D_MODEL = 2048
ATTN_WIDTH = 1024
HEAD_DIM = 64
N_HEADS = 16
POOL_WIDTH = 1024
POOL_WINDOWS = (2, 4, 8, 16)
POOL_GROUP_WIDTH = 256
POOL_HIST = 15
IN_WIDTH = 6144
DILATED_PATTERNS = ((128, 1), (512, 4), (2048, 16))
MAX_WINDOW = 2048
BLOCK = 128
PAST_LEN = 16384
EPS = 1e-6
ATTN_SCALE = HEAD_DIM ** -0.5
NEG = -1e30

LANES = 128
HEAD_PAIR = LANES // HEAD_DIM

PROJ_ROWS = 256
MERGE_ROWS = 256
SUPER = 2048
SAMPLE_HEADS = 4
VMEM_LIMIT = 56 * 1024 * 1024


def _silu(x):
    return x / (1.0 + jnp.exp(-x))


def _rms(x, g):
    ms = jnp.mean(x * x, axis=-1, keepdims=True)
    return x * lax.rsqrt(ms + EPS) * g


def _proj_kernel(x_ref, g_ref, w_ref, q_ref, k_ref, v_ref, ga_ref, u_ref, gp_ref, kt_ref, vt_ref,
                 *, first_kept_tile):
    i = pl.program_id(1)
    xn = _rms(x_ref[0], g_ref[...]).astype(BF16)

    def part(c):
        return jnp.dot(xn, w_ref[:, c * ATTN_WIDTH:(c + 1) * ATTN_WIDTH], preferred_element_type=F32)

    q_ref[0] = (part(0) * ATTN_SCALE).astype(BF16)
    k = part(1)
    k_ref[0] = k.astype(BF16)
    v = part(2)
    v_ref[0] = v.astype(BF16)
    ga_ref[0] = part(3).astype(BF16)
    u_ref[0] = part(4)
    gp_ref[0] = part(5).astype(BF16)

    @pl.when(i >= first_kept_tile)
    def _():
        kt_ref[0] = k.T
        vt_ref[0] = v.T


def _project_prompt(x, g, w_bf):
    b, s, _ = x.shape
    tiles = s // PROJ_ROWS
    first_kept = (s - MAX_WINDOW) // PROJ_ROWS
    tok = lambda width: pl.BlockSpec((1, PROJ_ROWS, width), lambda bi, i: (bi, i, 0))
    feat = pl.BlockSpec((1, ATTN_WIDTH, PROJ_ROWS), lambda bi, i: (bi, 0, jnp.maximum(i - first_kept, 0)))
    return pl.pallas_call(
        functools.partial(_proj_kernel, first_kept_tile=first_kept),
        grid=(b, tiles),
        in_specs=[tok(D_MODEL),
                  pl.BlockSpec((1, D_MODEL), lambda bi, i: (0, 0)),
                  pl.BlockSpec((D_MODEL, IN_WIDTH), lambda bi, i: (0, 0), pipeline_mode=pl.Buffered(1))],
        out_specs=[tok(ATTN_WIDTH)] * 6 + [feat, feat],
        out_shape=[jax.ShapeDtypeStruct((b, s, ATTN_WIDTH), BF16)] * 4
        + [jax.ShapeDtypeStruct((b, s, POOL_WIDTH), F32), jax.ShapeDtypeStruct((b, s, POOL_WIDTH), BF16)]
        + [jax.ShapeDtypeStruct((b, ATTN_WIDTH, MAX_WINDOW), F32)] * 2,
        compiler_params=pltpu.CompilerParams(dimension_semantics=("arbitrary", "arbitrary"),
                                             vmem_limit_bytes=VMEM_LIMIT),
        name="proj_prompt",
    )(x, g, w_bf)


def _proj_sample_kernel(x_ref, g_ref, w_ref, z_ref):
    xn = _rms(x_ref[...], g_ref[...]).astype(BF16)
    z_ref[...] = jnp.dot(xn, w_ref[...], preferred_element_type=F32)


def _project_sample(x_tb, g, w_bf):
    rows = x_tb.shape[0]
    return pl.pallas_call(
        _proj_sample_kernel,
        out_shape=jax.ShapeDtypeStruct((rows, IN_WIDTH), F32),
        compiler_params=pltpu.CompilerParams(vmem_limit_bytes=VMEM_LIMIT),
        name="proj_sample",
    )(x_tb, g, w_bf)


def _attn_kernel(q_ref, kp_ref, kc_ref, vp_ref, vc_ref, o_ref,
                 qf, kf, vf, acc, mx, sm, bias, bias_first):
    sb = pl.program_id(1)
    qf[...] = q_ref[0].astype(F32)
    kf[0:SUPER] = kp_ref[0].astype(F32)
    kf[SUPER:2 * SUPER] = kc_ref[0].astype(F32)
    vf[0:SUPER] = vp_ref[0].astype(F32)
    vf[SUPER:2 * SUPER] = vc_ref[0].astype(F32)

    qq = lax.broadcasted_iota(jnp.int32, (HEAD_PAIR * BLOCK, 2 * BLOCK), 0) % BLOCK
    kk = lax.broadcasted_iota(jnp.int32, (HEAD_PAIR * BLOCK, 2 * BLOCK), 1)
    band = (kk - qq >= 0) & (kk - qq <= BLOCK)
    bias[...] = jnp.where(band, 0.0, NEG)
    bias_first[...] = jnp.where(band & (kk >= BLOCK), 0.0, NEG)

    lane = lax.broadcasted_iota(jnp.int32, (BLOCK, LANES), 1)
    low = lane < HEAD_DIM

    for pat, (_, d) in enumerate(DILATED_PATTERNS):
        blocks_per_residue = SUPER // (d * BLOCK)

        def body(t, carry, pat=pat, d=d, blocks_per_residue=blocks_per_residue):
            r = t % d
            j = t // d
            qs = r + d * BLOCK * j
            ks = SUPER + qs - d * BLOCK
            qb = qf[pl.ds(qs, BLOCK, stride=d), :]
            q2 = jnp.concatenate([jnp.where(low, qb, 0.0), jnp.where(low, 0.0, qb)], axis=0).astype(BF16)
            kb = kf[pl.ds(ks, 2 * BLOCK, stride=d), :].astype(BF16)
            vb = vf[pl.ds(ks, 2 * BLOCK, stride=d), :].astype(BF16)
            s = lax.dot_general(q2, kb, (((1,), (1,)), ((), ())), preferred_element_type=F32)
            first = jnp.logical_and(sb == 0, j == 0)
            s = s + jnp.where(first, bias_first[...], bias[...])
            m = jnp.max(s, axis=1, keepdims=True)
            p = jnp.exp(s - m)
            l = jnp.sum(p, axis=1, keepdims=True)
            o2 = jnp.dot(p.astype(BF16), vb, preferred_element_type=F32)
            rows = pl.ds(qs, BLOCK, stride=d)
            acc[pat, rows, :] = jnp.where(low, o2[:BLOCK], o2[BLOCK:])
            mx[pat, rows, :] = jnp.where(low, m[:BLOCK], m[BLOCK:])
            sm[pat, rows, :] = jnp.where(low, l[:BLOCK], l[BLOCK:])
            return carry

        lax.fori_loop(0, d * blocks_per_residue, body, 0)

    m_all = jnp.maximum(jnp.maximum(mx[0], mx[1]), mx[2])
    num = jnp.zeros((SUPER, LANES), F32)
    den = jnp.zeros((SUPER, LANES), F32)
    for pat in range(len(DILATED_PATTERNS)):
        w = jnp.exp(mx[pat] - m_all)
        num = num + w * acc[pat]
        den = den + w * sm[pat]
    o_ref[0] = (num / den).astype(BF16)


def _attend_prompt(q, k, v):
    b, s, _ = q.shape
    n_super = s // SUPER
    cur = pl.BlockSpec((1, SUPER, LANES), lambda bi, sb, hp: (bi, sb, hp))
    prev = pl.BlockSpec((1, SUPER, LANES), lambda bi, sb, hp: (bi, jnp.maximum(sb - 1, 0), hp))
    n_pat = len(DILATED_PATTERNS)
    return pl.pallas_call(
        _attn_kernel,
        grid=(b, n_super, N_HEADS // HEAD_PAIR),
        in_specs=[cur, prev, cur, prev, cur],
        out_specs=cur,
        out_shape=jax.ShapeDtypeStruct((b, s, ATTN_WIDTH), BF16),
        scratch_shapes=[pltpu.VMEM((SUPER, LANES), F32),
                        pltpu.VMEM((2 * SUPER, LANES), F32),
                        pltpu.VMEM((2 * SUPER, LANES), F32),
                        pltpu.VMEM((n_pat, SUPER, LANES), F32),
                        pltpu.VMEM((n_pat, SUPER, LANES), F32),
                        pltpu.VMEM((n_pat, SUPER, LANES), F32),
                        pltpu.VMEM((HEAD_PAIR * BLOCK, 2 * BLOCK), F32),
                        pltpu.VMEM((HEAD_PAIR * BLOCK, 2 * BLOCK), F32)],
        compiler_params=pltpu.CompilerParams(dimension_semantics=("arbitrary",) * 3,
                                             vmem_limit_bytes=VMEM_LIMIT),
        name="attn_prompt",
    )(q, k, k, v, v)


def _pool_groups(window_sum, cur, counts, wp_ref, scale):
    outs = []
    for g, w in enumerate(POOL_WINDOWS):
        tot = window_sum(g, 0)
        for j in range(1, w):
            tot = tot + window_sum(g, j)
        pooled = tot / counts[g] - cur(g)
        outs.append(jnp.dot(pooled.astype(BF16), wp_ref[g], preferred_element_type=F32))
    return jnp.concatenate(outs, axis=-1) * scale


def _merge_kernel(x_ref, o_ref, ga_ref, gp_ref, u_ref, uh_ref, wp_ref, ps_ref, wo_ref, fg_ref, y_ref, ue):
    i = pl.program_id(1)
    hist_rows = uh_ref.shape[1]
    rows = u_ref.shape[1]
    ue[0:hist_rows] = jnp.where(i == 0, 0.0, uh_ref[0])
    ue[hist_rows:hist_rows + rows] = u_ref[0]

    pos = i * rows + lax.broadcasted_iota(jnp.int32, (rows, 1), 0)
    counts = [jnp.minimum(pos + 1, w).astype(F32) for w in POOL_WINDOWS]

    def window_sum(g, j):
        return ue[pl.ds(hist_rows - j, rows), g * POOL_GROUP_WIDTH:(g + 1) * POOL_GROUP_WIDTH]

    pool = _pool_groups(window_sum, lambda g: window_sum(g, 0), counts, wp_ref, ps_ref[...])
    mix = jnp.concatenate([o_ref[0].astype(F32) * _silu(ga_ref[0].astype(F32)),
                           pool * _silu(gp_ref[0].astype(F32))], axis=-1).astype(BF16)
    h = x_ref[0] + jnp.dot(mix, wo_ref[...], preferred_element_type=F32)
    y_ref[0] = _rms(h, fg_ref[...])


def _merge_prompt(x, o, ga, gp, u, wp_bf, pool_scale, wo_bf, final_g):
    b, s, _ = x.shape
    hist_rows = 16
    per_tile = MERGE_ROWS // hist_rows
    tok = lambda width: pl.BlockSpec((1, MERGE_ROWS, width), lambda bi, i: (bi, i, 0))
    const = lambda shape: pl.BlockSpec(shape, lambda bi, i: (0,) * len(shape))
    return pl.pallas_call(
        _merge_kernel,
        grid=(b, s // MERGE_ROWS),
        in_specs=[tok(D_MODEL), tok(ATTN_WIDTH), tok(ATTN_WIDTH), tok(POOL_WIDTH), tok(POOL_WIDTH),
                  pl.BlockSpec((1, hist_rows, POOL_WIDTH),
                               lambda bi, i: (bi, jnp.maximum(i * per_tile - 1, 0), 0)),
                  const((len(POOL_WINDOWS), POOL_GROUP_WIDTH, POOL_GROUP_WIDTH)),
                  const((1, POOL_WIDTH)),
                  pl.BlockSpec((D_MODEL, D_MODEL), lambda bi, i: (0, 0), pipeline_mode=pl.Buffered(1)),
                  const((1, D_MODEL))],
        out_specs=tok(D_MODEL),
        out_shape=jax.ShapeDtypeStruct((b, s, D_MODEL), F32),
        scratch_shapes=[pltpu.VMEM((hist_rows + MERGE_ROWS, POOL_WIDTH), F32)],
        compiler_params=pltpu.CompilerParams(dimension_semantics=("arbitrary", "arbitrary"),
                                             vmem_limit_bytes=VMEM_LIMIT),
        name="merge_prompt",
    )(x, o, ga, gp, u, u, wp_bf, pool_scale, wo_bf, final_g)


def _sample_attn_kernel(q_ref, kn_ref, vn_ref, ck_ref, cv_ref, o_ref, ok_ref, ov_ref, *, n_new):
    buf = ck_ref.shape[-1]
    lane = lax.broadcasted_iota(jnp.int32, (1, buf), 1)
    tail_lane = lax.broadcasted_iota(jnp.int32, (HEAD_DIM, LANES), 1)
    new_lane = lax.broadcasted_iota(jnp.int32, (1, LANES), 1)

    def multiplicity(delta):
        c = jnp.zeros(delta.shape, F32)
        for window, d in DILATED_PATTERNS:
            assert d & (d - 1) == 0, "power-of-two dilation: residue test is a bit mask"
            c = c + jnp.where(((delta & (d - 1)) == 0) & (delta <= window), 1.0, 0.0)
        return c

    for hh in range(ck_ref.shape[1]):
        kc = ck_ref[0, hh]
        vc = cv_ref[0, hh]
        qt = q_ref[0, hh] * ATTN_SCALE
        knt = kn_ref[0, hh]
        vnt = vn_ref[0, hh]
        out = jnp.zeros((HEAD_DIM, LANES), F32)
        for i in range(n_new):
            qc = qt[:, i:i + 1]
            sc = jnp.sum(kc * qc, axis=0, keepdims=True)
            sn = jnp.sum(knt * qc, axis=0, keepdims=True)
            cc = multiplicity(buf + i - lane)
            cn = jnp.where(new_lane <= i, multiplicity(i - jnp.minimum(new_lane, i)), 0.0)
            sc = jnp.where(cc > 0, sc, NEG)
            sn = jnp.where(cn > 0, sn, NEG)
            m = jnp.maximum(jnp.max(sc, axis=1, keepdims=True), jnp.max(sn, axis=1, keepdims=True))
            pc = cc * jnp.exp(sc - m)
            pn = cn * jnp.exp(sn - m)
            den = jnp.sum(pc, axis=1, keepdims=True) + jnp.sum(pn, axis=1, keepdims=True)
            oc = jnp.sum(vc * pc, axis=1, keepdims=True) + jnp.sum(vnt * pn, axis=1, keepdims=True)
            out = jnp.where(tail_lane == i, oc / den, out)
        o_ref[0, hh] = out

        for src, new, dst in ((kc, knt, ok_ref), (vc, vnt, ov_ref)):
            rolled = pltpu.roll(src, buf - n_new, 1)
            tail = rolled[:, buf - LANES:]
            placed = pltpu.roll(new, LANES - n_new, 1)
            dst[0, hh, :, 0:buf - LANES] = rolled[:, 0:buf - LANES]
            dst[0, hh, :, buf - LANES:buf] = jnp.where(tail_lane >= LANES - n_new, placed, tail)


def _attend_sample(qt, knt, vnt, ck, cv, n_new):
    b, h, dh, buf = ck.shape
    small = pl.BlockSpec((1, SAMPLE_HEADS, dh, LANES), lambda bi, hg: (bi, hg, 0, 0))
    big = pl.BlockSpec((1, SAMPLE_HEADS, dh, buf), lambda bi, hg: (bi, hg, 0, 0))
    return pl.pallas_call(
        functools.partial(_sample_attn_kernel, n_new=n_new),
        grid=(b, h // SAMPLE_HEADS),
        in_specs=[small, small, small, big, big],
        out_specs=[small, big, big],
        out_shape=[jax.ShapeDtypeStruct((b, h, dh, LANES), F32),
                   jax.ShapeDtypeStruct((b, h, dh, buf), F32),
                   jax.ShapeDtypeStruct((b, h, dh, buf), F32)],
        compiler_params=pltpu.CompilerParams(dimension_semantics=("arbitrary", "arbitrary"),
                                             vmem_limit_bytes=VMEM_LIMIT),
        name="attn_sample",
    )(qt, knt, vnt, ck, cv)


def _merge_sample_kernel(x_ref, o_ref, z_ref, hist_ref, wp_ref, ps_ref, wo_ref, fg_ref, y_ref, pool_ref, ue,
                         *, n_new, n_batch):
    u_col = 4 * ATTN_WIDTH
    ue[0:POOL_HIST] = hist_ref[...]
    ue[POOL_HIST:POOL_HIST + n_new] = z_ref[:, u_col:u_col + POOL_WIDTH].reshape(n_new, n_batch, POOL_WIDTH)
    pool_ref[...] = ue[n_new:n_new + POOL_HIST]

    rows = n_new * n_batch

    def window_sum(g, j):
        blk = ue[pl.ds(POOL_HIST - j, n_new), :, g * POOL_GROUP_WIDTH:(g + 1) * POOL_GROUP_WIDTH]
        return blk.reshape(rows, POOL_GROUP_WIDTH)

    pos = PAST_LEN + lax.broadcasted_iota(jnp.int32, (rows, 1), 0) // n_batch
    counts = [jnp.minimum(pos + 1, w).astype(F32) for w in POOL_WINDOWS]
    pool = _pool_groups(window_sum, lambda g: window_sum(g, 0), counts, wp_ref, ps_ref[...])
    ga = z_ref[:, 3 * ATTN_WIDTH:4 * ATTN_WIDTH]
    gp = z_ref[:, u_col + POOL_WIDTH:u_col + 2 * POOL_WIDTH]
    mix = jnp.concatenate([o_ref[...] * _silu(ga), pool * _silu(gp)], axis=-1).astype(BF16)
    h = x_ref[...] + jnp.dot(mix, wo_ref[...], preferred_element_type=F32)
    y_ref[...] = _rms(h, fg_ref[...])


def _merge_sample(x_tb, o_tb, z_tb, hist, wp_bf, pool_scale, wo_bf, final_g, n_new, n_batch):
    return pl.pallas_call(
        functools.partial(_merge_sample_kernel, n_new=n_new, n_batch=n_batch),
        out_shape=[jax.ShapeDtypeStruct(x_tb.shape, F32),
                   jax.ShapeDtypeStruct((POOL_HIST, n_batch, POOL_WIDTH), F32)],
        scratch_shapes=[pltpu.VMEM((POOL_HIST + n_new, n_batch, POOL_WIDTH), F32)],
        compiler_params=pltpu.CompilerParams(vmem_limit_bytes=VMEM_LIMIT),
        name="merge_sample",
    )(x_tb, o_tb, z_tb, hist, wp_bf, pool_scale, wo_bf, final_g)


def _to_feature_major(z_cols, n_new, n_batch):
    t = z_cols.reshape(n_new, n_batch, N_HEADS, HEAD_DIM).transpose(1, 2, 3, 0)
    return jnp.pad(t, ((0, 0), (0, 0), (0, 0), (0, LANES - n_new)))


def kernel(x_prompt, x_sample, cache_k, cache_v, state_pool, norm_g, w_in, w_pool, pool_scale, w_out, final_norm_g):
    depth = norm_g.shape[0]
    assert depth == 1, "single-layer step"
    b, s, _ = x_prompt.shape
    n_batch, n_new, _ = x_sample.shape
    buf = cache_k.shape[2]
    assert s % SUPER == 0 and s >= MAX_WINDOW and buf == MAX_WINDOW and n_new <= POOL_HIST

    g = norm_g[0][None, :]
    fg = final_norm_g[None, :]
    w_bf = w_in[0].astype(BF16)
    wo_bf = w_out[0].astype(BF16)
    wp_bf = w_pool[0].astype(BF16)
    ps = pool_scale[0][None, :]

    q, k, v, ga, u, gp, kt, vt = _project_prompt(x_prompt, g, w_bf)
    o = _attend_prompt(q, k, v)
    y_prompt = _merge_prompt(x_prompt, o, ga, gp, u, wp_bf, ps, wo_bf, fg)
    to_window = lambda t: t.reshape(b, N_HEADS, HEAD_DIM, MAX_WINDOW).transpose(0, 3, 1, 2)[None]
    k_prompt, v_prompt = to_window(kt), to_window(vt)
    pool_prompt = u[:, s - POOL_HIST:][None]

    x_tb = x_sample.transpose(1, 0, 2).reshape(n_new * n_batch, D_MODEL)
    z_tb = _project_sample(x_tb, g, w_bf)
    qt = _to_feature_major(z_tb[:, 0:ATTN_WIDTH], n_new, n_batch)
    knt = _to_feature_major(z_tb[:, ATTN_WIDTH:2 * ATTN_WIDTH], n_new, n_batch)
    vnt = _to_feature_major(z_tb[:, 2 * ATTN_WIDTH:3 * ATTN_WIDTH], n_new, n_batch)
    ck = cache_k[0].transpose(0, 2, 3, 1)
    cv = cache_v[0].transpose(0, 2, 3, 1)
    ot, ok, ov = _attend_sample(qt, knt, vnt, ck, cv, n_new)
    o_tb = ot[..., :n_new].transpose(3, 0, 1, 2).reshape(n_new * n_batch, ATTN_WIDTH)
    hist = state_pool[0].transpose(1, 0, 2)
    y_tb, pool_tb = _merge_sample(x_tb, o_tb, z_tb, hist, wp_bf, ps, wo_bf, fg, n_new, n_batch)
    y_sample = y_tb.reshape(n_new, n_batch, D_MODEL).transpose(1, 0, 2)
    k_sample = ok.transpose(0, 3, 1, 2)[None]
    v_sample = ov.transpose(0, 3, 1, 2)[None]
    pool_sample = pool_tb.transpose(1, 0, 2)[None]

    return (y_prompt, y_sample, k_prompt, v_prompt, pool_prompt, k_sample, v_sample, pool_sample)
```

```python
import functools

import jax
import jax.numpy as jnp
from jax import lax
from jax.experimental import pallas as pl
from jax.experimental.pallas import tpu as pltpu

F32 = jnp.float32
BF16 = jnp.bfloat16

D_MODEL = 2048
ATTN_WIDTH = 1024
HEAD_DIM = 64
N_HEADS = 16
POOL_WIDTH = 1024
POOL_WINDOWS = (2, 4, 8, 16)
POOL_GROUP_WIDTH = 256
POOL_HIST = 15
IN_WIDTH = 6144
DILATED_PATTERNS = ((128, 1), (512, 4), (2048, 16))
MAX_WINDOW = 2048
BLOCK = 128
PAST_LEN = 16384
EPS = 1e-6
ATTN_SCALE = HEAD_DIM ** -0.5
NEG = -1e30
LOG2E = 1.4426950408889634

LANES = 128
HEAD_PAIR = LANES // HEAD_DIM

PROJ_ROWS = 256
MERGE_ROWS = 256
SUPER = 2048
ATTN_UNROLL = 16
SAMPLE_HEADS = 4
VMEM_LIMIT = 56 * 1024 * 1024


def _silu(x):
    return x / (1.0 + jnp.exp(-x))


def _rms(x, g):
    ms = jnp.mean(x * x, axis=-1, keepdims=True)
    return x * lax.rsqrt(ms + EPS) * g


def _proj_kernel(x_ref, g_ref, w_ref, q_ref, k_ref, v_ref, ga_ref, u_ref, gp_ref, kt_ref, vt_ref,
                 *, first_kept_tile):
    i = pl.program_id(1)
    xn = _rms(x_ref[0], g_ref[...]).astype(BF16)

    def part(c):
        return jnp.dot(xn, w_ref[:, c * ATTN_WIDTH:(c + 1) * ATTN_WIDTH], preferred_element_type=F32)

    q_ref[0] = (part(0) * (ATTN_SCALE * LOG2E)).astype(BF16)
    k = part(1)
    k_ref[0] = k.astype(BF16)
    v = part(2)
    v_ref[0] = v.astype(BF16)
    ga_ref[0] = part(3).astype(BF16)
    u_ref[0] = part(4)
    gp_ref[0] = part(5).astype(BF16)

    @pl.when(i >= first_kept_tile)
    def _():
        kt_ref[0] = k.T
        vt_ref[0] = v.T


def _project_prompt(x, g, w_bf):
    b, s, _ = x.shape
    tiles = s // PROJ_ROWS
    first_kept = (s - MAX_WINDOW) // PROJ_ROWS
    tok = lambda width: pl.BlockSpec((1, PROJ_ROWS, width), lambda bi, i: (bi, i, 0))
    feat = pl.BlockSpec((1, ATTN_WIDTH, PROJ_ROWS), lambda bi, i: (bi, 0, jnp.maximum(i - first_kept, 0)))
    return pl.pallas_call(
        functools.partial(_proj_kernel, first_kept_tile=first_kept),
        grid=(b, tiles),
        in_specs=[tok(D_MODEL),
                  pl.BlockSpec((1, D_MODEL), lambda bi, i: (0, 0)),
                  pl.BlockSpec((D_MODEL, IN_WIDTH), lambda bi, i: (0, 0), pipeline_mode=pl.Buffered(1))],
        out_specs=[tok(ATTN_WIDTH)] * 6 + [feat, feat],
        out_shape=[jax.ShapeDtypeStruct((b, s, ATTN_WIDTH), BF16)] * 4
        + [jax.ShapeDtypeStruct((b, s, POOL_WIDTH), F32), jax.ShapeDtypeStruct((b, s, POOL_WIDTH), BF16)]
        + [jax.ShapeDtypeStruct((b, ATTN_WIDTH, MAX_WINDOW), F32)] * 2,
        compiler_params=pltpu.CompilerParams(dimension_semantics=("arbitrary", "arbitrary"),
                                             vmem_limit_bytes=VMEM_LIMIT),
        name="proj_prompt",
    )(x, g, w_bf)


def _proj_sample_kernel(x_ref, g_ref, w_ref, z_ref):
    xn = _rms(x_ref[...], g_ref[...]).astype(BF16)
    z_ref[...] = jnp.dot(xn, w_ref[...], preferred_element_type=F32)


def _project_sample(x_tb, g, w_bf):
    rows = x_tb.shape[0]
    return pl.pallas_call(
        _proj_sample_kernel,
        out_shape=jax.ShapeDtypeStruct((rows, IN_WIDTH), F32),
        compiler_params=pltpu.CompilerParams(vmem_limit_bytes=VMEM_LIMIT),
        name="proj_sample",
    )(x_tb, g, w_bf)


def _attn_kernel(q_ref, kp_ref, kc_ref, vp_ref, vc_ref, o_ref,
                 qf, kf, vf, acc, mx, sm, bias):
    sb = pl.program_id(1)
    qf[...] = q_ref[0].astype(F32)
    kf[0:SUPER] = kp_ref[0].astype(F32)
    kf[SUPER:2 * SUPER] = kc_ref[0].astype(F32)
    vf[0:SUPER] = vp_ref[0].astype(F32)
    vf[SUPER:2 * SUPER] = vc_ref[0].astype(F32)

    qq = lax.broadcasted_iota(jnp.int32, (HEAD_PAIR * BLOCK, 2 * BLOCK), 0) % BLOCK
    kk = lax.broadcasted_iota(jnp.int32, (HEAD_PAIR * BLOCK, 2 * BLOCK), 1)
    band = (kk - qq >= 0) & (kk - qq <= BLOCK)
    bias[0] = jnp.where(band, 0.0, NEG)
    bias[1] = jnp.where(band & (kk >= BLOCK), 0.0, NEG)

    lane = lax.broadcasted_iota(jnp.int32, (BLOCK, LANES), 1)
    low = lane < HEAD_DIM

    for pat, (_, d) in enumerate(DILATED_PATTERNS):
        blocks_per_residue = SUPER // (d * BLOCK)

        def body(t, carry, pat=pat, d=d, blocks_per_residue=blocks_per_residue):
            r = t % d
            j = t // d
            qs = r + d * BLOCK * j
            ks = SUPER + qs - d * BLOCK
            qb = qf[pl.ds(qs, BLOCK, stride=d), :]
            q2 = jnp.concatenate([jnp.where(low, qb, 0.0), jnp.where(low, 0.0, qb)], axis=0).astype(BF16)
            kb = kf[pl.ds(ks, 2 * BLOCK, stride=d), :].astype(BF16)
            vb = vf[pl.ds(ks, 2 * BLOCK, stride=d), :].astype(BF16)
            s = lax.dot_general(q2, kb, (((1,), (1,)), ((), ())), preferred_element_type=F32)
            first = jnp.logical_and(sb == 0, j == 0).astype(jnp.int32)
            s = s + bias[first]
            m = jnp.max(s, axis=1, keepdims=True)
            p = jnp.exp2(s - m)
            l = jnp.sum(p, axis=1, keepdims=True)
            o2 = jnp.dot(p.astype(BF16), vb, preferred_element_type=F32)
            rows = pl.ds(qs, BLOCK, stride=d)
            acc[pat, rows, :] = jnp.where(low, o2[:BLOCK], o2[BLOCK:])
            mx[pat, rows, :] = jnp.where(low, m[:BLOCK], m[BLOCK:])
            sm[pat, rows, :] = jnp.where(low, l[:BLOCK], l[BLOCK:])
            return carry

        lax.fori_loop(0, d * blocks_per_residue, body, 0, unroll=ATTN_UNROLL)

    m_all = jnp.maximum(jnp.maximum(mx[0], mx[1]), mx[2])
    num = jnp.zeros((SUPER, LANES), F32)
    den = jnp.zeros((SUPER, LANES), F32)
    for pat in range(len(DILATED_PATTERNS)):
        w = jnp.exp2(mx[pat] - m_all)
        num = num + w * acc[pat]
        den = den + w * sm[pat]
    o_ref[0] = (num / den).astype(BF16)


def _attend_prompt(q, k, v):
    b, s, _ = q.shape
    n_super = s // SUPER
    cur = pl.BlockSpec((1, SUPER, LANES), lambda bi, sb, hp: (bi, sb, hp))
    prev = pl.BlockSpec((1, SUPER, LANES), lambda bi, sb, hp: (bi, jnp.maximum(sb - 1, 0), hp))
    n_pat = len(DILATED_PATTERNS)
    return pl.pallas_call(
        _attn_kernel,
        grid=(b, n_super, N_HEADS // HEAD_PAIR),
        in_specs=[cur, prev, cur, prev, cur],
        out_specs=cur,
        out_shape=jax.ShapeDtypeStruct((b, s, ATTN_WIDTH), BF16),
        scratch_shapes=[pltpu.VMEM((SUPER, LANES), F32),
                        pltpu.VMEM((2 * SUPER, LANES), F32),
                        pltpu.VMEM((2 * SUPER, LANES), F32),
                        pltpu.VMEM((n_pat, SUPER, LANES), F32),
                        pltpu.VMEM((n_pat, SUPER, LANES), F32),
                        pltpu.VMEM((n_pat, SUPER, LANES), F32),
                        pltpu.VMEM((2, HEAD_PAIR * BLOCK, 2 * BLOCK), F32)],
        compiler_params=pltpu.CompilerParams(dimension_semantics=("arbitrary",) * 3,
                                             vmem_limit_bytes=VMEM_LIMIT),
        name="attn_prompt",
    )(q, k, k, v, v)


def _pool_groups(window_sum, cur, counts, wp_ref, scale):
    outs = []
    for g, w in enumerate(POOL_WINDOWS):
        tot = window_sum(g, 0)
        for j in range(1, w):
            tot = tot + window_sum(g, j)
        pooled = tot / counts[g] - cur(g)
        outs.append(jnp.dot(pooled.astype(BF16), wp_ref[g], preferred_element_type=F32))
    return jnp.concatenate(outs, axis=-1) * scale


def _merge_kernel(x_ref, o_ref, ga_ref, gp_ref, u_ref, uh_ref, wp_ref, ps_ref, wo_ref, fg_ref, y_ref, ue):
    i = pl.program_id(1)
    hist_rows = uh_ref.shape[1]
    rows = u_ref.shape[1]
    ue[0:hist_rows] = jnp.where(i == 0, 0.0, uh_ref[0])
    ue[hist_rows:hist_rows + rows] = u_ref[0]

    pos = i * rows + lax.broadcasted_iota(jnp.int32, (rows, 1), 0)
    counts = [jnp.minimum(pos + 1, w).astype(F32) for w in POOL_WINDOWS]

    def window_sum(g, j):
        return ue[pl.ds(hist_rows - j, rows), g * POOL_GROUP_WIDTH:(g + 1) * POOL_GROUP_WIDTH]

    pool = _pool_groups(window_sum, lambda g: window_sum(g, 0), counts, wp_ref, ps_ref[...])
    mix = jnp.concatenate([o_ref[0].astype(F32) * _silu(ga_ref[0].astype(F32)),
                           pool * _silu(gp_ref[0].astype(F32))], axis=-1).astype(BF16)
    h = x_ref[0] + jnp.dot(mix, wo_ref[...], preferred_element_type=F32)
    y_ref[0] = _rms(h, fg_ref[...])


def _merge_prompt(x, o, ga, gp, u, wp_bf, pool_scale, wo_bf, final_g):
    b, s, _ = x.shape
    hist_rows = 16
    per_tile = MERGE_ROWS // hist_rows
    tok = lambda width: pl.BlockSpec((1, MERGE_ROWS, width), lambda bi, i: (bi, i, 0))
    const = lambda shape: pl.BlockSpec(shape, lambda bi, i: (0,) * len(shape))
    return pl.pallas_call(
        _merge_kernel,
        grid=(b, s // MERGE_ROWS),
        in_specs=[tok(D_MODEL), tok(ATTN_WIDTH), tok(ATTN_WIDTH), tok(POOL_WIDTH), tok(POOL_WIDTH),
                  pl.BlockSpec((1, hist_rows, POOL_WIDTH),
                               lambda bi, i: (bi, jnp.maximum(i * per_tile - 1, 0), 0)),
                  const((len(POOL_WINDOWS), POOL_GROUP_WIDTH, POOL_GROUP_WIDTH)),
                  const((1, POOL_WIDTH)),
                  pl.BlockSpec((D_MODEL, D_MODEL), lambda bi, i: (0, 0), pipeline_mode=pl.Buffered(1)),
                  const((1, D_MODEL))],
        out_specs=tok(D_MODEL),
        out_shape=jax.ShapeDtypeStruct((b, s, D_MODEL), F32),
        scratch_shapes=[pltpu.VMEM((hist_rows + MERGE_ROWS, POOL_WIDTH), F32)],
        compiler_params=pltpu.CompilerParams(dimension_semantics=("arbitrary", "arbitrary"),
                                             vmem_limit_bytes=VMEM_LIMIT),
        name="merge_prompt",
    )(x, o, ga, gp, u, u, wp_bf, pool_scale, wo_bf, final_g)


def _sample_attn_kernel(q_ref, kn_ref, vn_ref, ck_ref, cv_ref, o_ref, ok_ref, ov_ref, *, n_new):
    buf = ck_ref.shape[-1]
    lane = lax.broadcasted_iota(jnp.int32, (1, buf), 1)
    tail_lane = lax.broadcasted_iota(jnp.int32, (HEAD_DIM, LANES), 1)
    new_lane = lax.broadcasted_iota(jnp.int32, (1, LANES), 1)

    def multiplicity(delta):
        c = jnp.zeros(delta.shape, F32)
        for window, d in DILATED_PATTERNS:
            assert d & (d - 1) == 0, "power-of-two dilation: residue test is a bit mask"
            c = c + jnp.where(((delta & (d - 1)) == 0) & (delta <= window), 1.0, 0.0)
        return c

    for hh in range(ck_ref.shape[1]):
        kc = ck_ref[0, hh]
        vc = cv_ref[0, hh]
        qt = q_ref[0, hh] * ATTN_SCALE
        knt = kn_ref[0, hh]
        vnt = vn_ref[0, hh]
        out = jnp.zeros((HEAD_DIM, LANES), F32)
        for i in range(n_new):
            qc = qt[:, i:i + 1]
            sc = jnp.sum(kc * qc, axis=0, keepdims=True)
            sn = jnp.sum(knt * qc, axis=0, keepdims=True)
            cc = multiplicity(buf + i - lane)
            cn = jnp.where(new_lane <= i, multiplicity(i - jnp.minimum(new_lane, i)), 0.0)
            sc = jnp.where(cc > 0, sc, NEG)
            sn = jnp.where(cn > 0, sn, NEG)
            m = jnp.maximum(jnp.max(sc, axis=1, keepdims=True), jnp.max(sn, axis=1, keepdims=True))
            pc = cc * jnp.exp(sc - m)
            pn = cn * jnp.exp(sn - m)
            den = jnp.sum(pc, axis=1, keepdims=True) + jnp.sum(pn, axis=1, keepdims=True)
            oc = jnp.sum(vc * pc, axis=1, keepdims=True) + jnp.sum(vnt * pn, axis=1, keepdims=True)
            out = jnp.where(tail_lane == i, oc / den, out)
        o_ref[0, hh] = out

        for src, new, dst in ((kc, knt, ok_ref), (vc, vnt, ov_ref)):
            rolled = pltpu.roll(src, buf - n_new, 1)
            tail = rolled[:, buf - LANES:]
            placed = pltpu.roll(new, LANES - n_new, 1)
            dst[0, hh, :, 0:buf - LANES] = rolled[:, 0:buf - LANES]
            dst[0, hh, :, buf - LANES:buf] = jnp.where(tail_lane >= LANES - n_new, placed, tail)


def _attend_sample(qt, knt, vnt, ck, cv, n_new):
    b, h, dh, buf = ck.shape
    small = pl.BlockSpec((1, SAMPLE_HEADS, dh, LANES), lambda bi, hg: (bi, hg, 0, 0))
    big = pl.BlockSpec((1, SAMPLE_HEADS, dh, buf), lambda bi, hg: (bi, hg, 0, 0))
    return pl.pallas_call(
        functools.partial(_sample_attn_kernel, n_new=n_new),
        grid=(b, h // SAMPLE_HEADS),
        in_specs=[small, small, small, big, big],
        out_specs=[small, big, big],
        out_shape=[jax.ShapeDtypeStruct((b, h, dh, LANES), F32),
                   jax.ShapeDtypeStruct((b, h, dh, buf), F32),
                   jax.ShapeDtypeStruct((b, h, dh, buf), F32)],
        compiler_params=pltpu.CompilerParams(dimension_semantics=("arbitrary", "arbitrary"),
                                             vmem_limit_bytes=VMEM_LIMIT),
        name="attn_sample",
    )(qt, knt, vnt, ck, cv)


def _merge_sample_kernel(x_ref, o_ref, z_ref, hist_ref, wp_ref, ps_ref, wo_ref, fg_ref, y_ref, pool_ref, ue,
                         *, n_new, n_batch):
    u_col = 4 * ATTN_WIDTH
    ue[0:POOL_HIST] = hist_ref[...]
    ue[POOL_HIST:POOL_HIST + n_new] = z_ref[:, u_col:u_col + POOL_WIDTH].reshape(n_new, n_batch, POOL_WIDTH)
    pool_ref[...] = ue[n_new:n_new + POOL_HIST]

    rows = n_new * n_batch

    def window_sum(g, j):
        blk = ue[pl.ds(POOL_HIST - j, n_new), :, g * POOL_GROUP_WIDTH:(g + 1) * POOL_GROUP_WIDTH]
        return blk.reshape(rows, POOL_GROUP_WIDTH)

    pos = PAST_LEN + lax.broadcasted_iota(jnp.int32, (rows, 1), 0) // n_batch
    counts = [jnp.minimum(pos + 1, w).astype(F32) for w in POOL_WINDOWS]
    pool = _pool_groups(window_sum, lambda g: window_sum(g, 0), counts, wp_ref, ps_ref[...])
    ga = z_ref[:, 3 * ATTN_WIDTH:4 * ATTN_WIDTH]
    gp = z_ref[:, u_col + POOL_WIDTH:u_col + 2 * POOL_WIDTH]
    mix = jnp.concatenate([o_ref[...] * _silu(ga), pool * _silu(gp)], axis=-1).astype(BF16)
    h = x_ref[...] + jnp.dot(mix, wo_ref[...], preferred_element_type=F32)
    y_ref[...] = _rms(h, fg_ref[...])


def _merge_sample(x_tb, o_tb, z_tb, hist, wp_bf, pool_scale, wo_bf, final_g, n_new, n_batch):
    return pl.pallas_call(
        functools.partial(_merge_sample_kernel, n_new=n_new, n_batch=n_batch),
        out_shape=[jax.ShapeDtypeStruct(x_tb.shape, F32),
                   jax.ShapeDtypeStruct((POOL_HIST, n_batch, POOL_WIDTH), F32)],
        scratch_shapes=[pltpu.VMEM((POOL_HIST + n_new, n_batch, POOL_WIDTH), F32)],
        compiler_params=pltpu.CompilerParams(vmem_limit_bytes=VMEM_LIMIT),
        name="merge_sample",
    )(x_tb, o_tb, z_tb, hist, wp_bf, pool_scale, wo_bf, final_g)


def _to_feature_major(z_cols, n_new, n_batch):
    t = z_cols.reshape(n_new, n_batch, N_HEADS, HEAD_DIM).transpose(1, 2, 3, 0)
    return jnp.pad(t, ((0, 0), (0, 0), (0, 0), (0, LANES - n_new)))


def kernel(x_prompt, x_sample, cache_k, cache_v, state_pool, norm_g, w_in, w_pool, pool_scale, w_out, final_norm_g):
    depth = norm_g.shape[0]
    assert depth == 1, "single-layer step"
    b, s, _ = x_prompt.shape
    n_batch, n_new, _ = x_sample.shape
    buf = cache_k.shape[2]
    assert s % SUPER == 0 and s >= MAX_WINDOW and buf == MAX_WINDOW and n_new <= POOL_HIST

    g = norm_g[0][None, :]
    fg = final_norm_g[None, :]
    w_bf = w_in[0].astype(BF16)
    wo_bf = w_out[0].astype(BF16)
    wp_bf = w_pool[0].astype(BF16)
    ps = pool_scale[0][None, :]

    q, k, v, ga, u, gp, kt, vt = _project_prompt(x_prompt, g, w_bf)
    o = _attend_prompt(q, k, v)
    y_prompt = _merge_prompt(x_prompt, o, ga, gp, u, wp_bf, ps, wo_bf, fg)
    to_window = lambda t: t.reshape(b, N_HEADS, HEAD_DIM, MAX_WINDOW).transpose(0, 3, 1, 2)[None]
    k_prompt, v_prompt = to_window(kt), to_window(vt)
    pool_prompt = u[:, s - POOL_HIST:][None]

    x_tb = x_sample.transpose(1, 0, 2).reshape(n_new * n_batch, D_MODEL)
    z_tb = _project_sample(x_tb, g, w_bf)
    qt = _to_feature_major(z_tb[:, 0:ATTN_WIDTH], n_new, n_batch)
    knt = _to_feature_major(z_tb[:, ATTN_WIDTH:2 * ATTN_WIDTH], n_new, n_batch)
    vnt = _to_feature_major(z_tb[:, 2 * ATTN_WIDTH:3 * ATTN_WIDTH], n_new, n_batch)
    ck = cache_k[0].transpose(0, 2, 3, 1)
    cv = cache_v[0].transpose(0, 2, 3, 1)
    ot, ok, ov = _attend_sample(qt, knt, vnt, ck, cv, n_new)
    o_tb = ot[..., :n_new].transpose(3, 0, 1, 2).reshape(n_new * n_batch, ATTN_WIDTH)
    hist = state_pool[0].transpose(1, 0, 2)
    y_tb, pool_tb = _merge_sample(x_tb, o_tb, z_tb, hist, wp_bf, ps, wo_bf, fg, n_new, n_batch)
    y_sample = y_tb.reshape(n_new, n_batch, D_MODEL).transpose(1, 0, 2)
    k_sample = ok.transpose(0, 3, 1, 2)[None]
    v_sample = ov.transpose(0, 3, 1, 2)[None]
    pool_sample = pool_tb.transpose(1, 0, 2)[None]

    return (y_prompt, y_sample, k_prompt, v_prompt, pool_prompt, k_sample, v_sample, pool_sample)
```

```python
import functools

import jax
import jax.numpy as jnp
from jax import lax
from jax.experimental import pallas as pl
from jax.experimental.pallas import tpu as pltpu

F32 = jnp.float32
BF16 = jnp.bfloat16

D_MODEL = 2048
ATTN_WIDTH = 1024
HEAD_DIM = 64
N_HEADS = 16
POOL_WIDTH = 1024
POOL_WINDOWS = (2, 4, 8, 16)
POOL_GROUP_WIDTH = 256
POOL_HIST = 15
IN_WIDTH = 6144
DILATED_PATTERNS = ((128, 1), (512, 4), (2048, 16))
MAX_WINDOW = 2048
BLOCK = 128
PAST_LEN = 16384
EPS = 1e-6
ATTN_SCALE = HEAD_DIM ** -0.5
NEG = -1e30
LOG2E = 1.4426950408889634

LANES = 128
SUBLANES = 8
HEAD_PAIR = LANES // HEAD_DIM

PROJ_ROWS = 256
MERGE_ROWS = 512
SUPER = 2048
ATTN_UNROLL = 16
SAMPLE_HEADS = 4
VMEM_LIMIT = 56 * 1024 * 1024


def _silu(x):
    return x / (1.0 + jnp.exp(-x))


def _rms(x, g):
    ms = jnp.mean(x * x, axis=-1, keepdims=True)
    return x * lax.rsqrt(ms + EPS) * g


def _proj_kernel(x_ref, g_ref, w_ref, q_ref, k_ref, v_ref, ga_ref, u_ref, gp_ref, kt_ref, vt_ref,
                 *, first_kept_tile):
    i = pl.program_id(1)
    xn = _rms(x_ref[0], g_ref[...]).astype(BF16)

    def part(c):
        return jnp.dot(xn, w_ref[:, c * ATTN_WIDTH:(c + 1) * ATTN_WIDTH], preferred_element_type=F32)

    q_ref[0] = (part(0) * (ATTN_SCALE * LOG2E)).astype(BF16)
    k = part(1)
    k_ref[0] = k.astype(BF16)
    v = part(2)
    v_ref[0] = v.astype(BF16)
    ga_ref[0] = part(3).astype(BF16)
    u_ref[0] = part(4)
    gp_ref[0] = part(5).astype(BF16)

    @pl.when(i >= first_kept_tile)
    def _():
        kt_ref[0] = k.T
        vt_ref[0] = v.T


def _project_prompt(x, g, w_bf):
    b, s, _ = x.shape
    tiles = s // PROJ_ROWS
    first_kept = (s - MAX_WINDOW) // PROJ_ROWS
    tok = lambda width: pl.BlockSpec((1, PROJ_ROWS, width), lambda bi, i: (bi, i, 0))
    feat = pl.BlockSpec((1, ATTN_WIDTH, PROJ_ROWS), lambda bi, i: (bi, 0, jnp.maximum(i - first_kept, 0)))
    return pl.pallas_call(
        functools.partial(_proj_kernel, first_kept_tile=first_kept),
        grid=(b, tiles),
        in_specs=[tok(D_MODEL),
                  pl.BlockSpec((1, D_MODEL), lambda bi, i: (0, 0)),
                  pl.BlockSpec((D_MODEL, IN_WIDTH), lambda bi, i: (0, 0), pipeline_mode=pl.Buffered(1))],
        out_specs=[tok(ATTN_WIDTH)] * 6 + [feat, feat],
        out_shape=[jax.ShapeDtypeStruct((b, s, ATTN_WIDTH), BF16)] * 4
        + [jax.ShapeDtypeStruct((b, s, POOL_WIDTH), F32), jax.ShapeDtypeStruct((b, s, POOL_WIDTH), BF16)]
        + [jax.ShapeDtypeStruct((b, ATTN_WIDTH, MAX_WINDOW), F32)] * 2,
        compiler_params=pltpu.CompilerParams(dimension_semantics=("arbitrary", "arbitrary"),
                                             vmem_limit_bytes=VMEM_LIMIT),
        name="proj_prompt",
    )(x, g, w_bf)


def _proj_sample_kernel(x_ref, g_ref, w_ref, z_ref):
    xn = _rms(x_ref[...], g_ref[...]).astype(BF16)
    z_ref[...] = jnp.dot(xn, w_ref[...], preferred_element_type=F32)


def _project_sample(x_tb, g, w_bf):
    rows = x_tb.shape[0]
    return pl.pallas_call(
        _proj_sample_kernel,
        out_shape=jax.ShapeDtypeStruct((rows, IN_WIDTH), F32),
        compiler_params=pltpu.CompilerParams(vmem_limit_bytes=VMEM_LIMIT),
        name="proj_sample",
    )(x_tb, g, w_bf)


def _attn_kernel(q_ref, kp_ref, kc_ref, vp_ref, vc_ref, o_ref,
                 qf, kf, vf, acc, mx, sm, bias):
    sb = pl.program_id(1)
    qf[...] = q_ref[0].astype(F32)
    kf[0:SUPER] = kp_ref[0].astype(F32)
    kf[SUPER:2 * SUPER] = kc_ref[0].astype(F32)
    vf[0:SUPER] = vp_ref[0].astype(F32)
    vf[SUPER:2 * SUPER] = vc_ref[0].astype(F32)

    qq = lax.broadcasted_iota(jnp.int32, (HEAD_PAIR * BLOCK, 2 * BLOCK), 0) % BLOCK
    kk = lax.broadcasted_iota(jnp.int32, (HEAD_PAIR * BLOCK, 2 * BLOCK), 1)
    band = (kk - qq >= 0) & (kk - qq <= BLOCK)
    bias[0] = jnp.where(band, 0.0, NEG)
    bias[1] = jnp.where(band & (kk >= BLOCK), 0.0, NEG)

    lane = lax.broadcasted_iota(jnp.int32, (BLOCK, LANES), 1)
    low = lane < HEAD_DIM

    for pat, (_, d) in enumerate(DILATED_PATTERNS):
        blocks_per_residue = SUPER // (d * BLOCK)

        def body(t, carry, pat=pat, d=d, blocks_per_residue=blocks_per_residue):
            r = t % d
            j = t // d
            qs = r + d * BLOCK * j
            ks = SUPER + qs - d * BLOCK
            qb = qf[pl.ds(qs, BLOCK, stride=d), :]
            q2 = jnp.concatenate([jnp.where(low, qb, 0.0), jnp.where(low, 0.0, qb)], axis=0).astype(BF16)
            kb = kf[pl.ds(ks, 2 * BLOCK, stride=d), :].astype(BF16)
            vb = vf[pl.ds(ks, 2 * BLOCK, stride=d), :].astype(BF16)
            s = lax.dot_general(q2, kb, (((1,), (1,)), ((), ())), preferred_element_type=F32)
            first = jnp.logical_and(sb == 0, j == 0).astype(jnp.int32)
            s = s + bias[first]
            m = jnp.max(s, axis=1, keepdims=True)
            p = jnp.exp2(s - m)
            l = jnp.sum(p, axis=1, keepdims=True)
            o2 = jnp.dot(p.astype(BF16), vb, preferred_element_type=F32)
            rows = pl.ds(qs, BLOCK, stride=d)
            acc[pat, rows, :] = jnp.where(low, o2[:BLOCK], o2[BLOCK:])
            mx[pat, rows, :] = jnp.where(low, m[:BLOCK], m[BLOCK:])
            sm[pat, rows, :] = jnp.where(low, l[:BLOCK], l[BLOCK:])
            return carry

        lax.fori_loop(0, d * blocks_per_residue, body, 0, unroll=ATTN_UNROLL)

    m_all = jnp.maximum(jnp.maximum(mx[0], mx[1]), mx[2])
    num = jnp.zeros((SUPER, LANES), F32)
    den = jnp.zeros((SUPER, LANES), F32)
    for pat in range(len(DILATED_PATTERNS)):
        w = jnp.exp2(mx[pat] - m_all)
        num = num + w * acc[pat]
        den = den + w * sm[pat]
    o_ref[0] = (num / den).astype(BF16)


def _attend_prompt(q, k, v):
    b, s, _ = q.shape
    n_super = s // SUPER
    cur = pl.BlockSpec((1, SUPER, LANES), lambda bi, sb, hp: (bi, sb, hp))
    prev = pl.BlockSpec((1, SUPER, LANES), lambda bi, sb, hp: (bi, jnp.maximum(sb - 1, 0), hp))
    n_pat = len(DILATED_PATTERNS)
    return pl.pallas_call(
        _attn_kernel,
        grid=(b, n_super, N_HEADS // HEAD_PAIR),
        in_specs=[cur, prev, cur, prev, cur],
        out_specs=cur,
        out_shape=jax.ShapeDtypeStruct((b, s, ATTN_WIDTH), BF16),
        scratch_shapes=[pltpu.VMEM((SUPER, LANES), F32),
                        pltpu.VMEM((2 * SUPER, LANES), F32),
                        pltpu.VMEM((2 * SUPER, LANES), F32),
                        pltpu.VMEM((n_pat, SUPER, LANES), F32),
                        pltpu.VMEM((n_pat, SUPER, LANES), F32),
                        pltpu.VMEM((n_pat, SUPER, LANES), F32),
                        pltpu.VMEM((2, HEAD_PAIR * BLOCK, 2 * BLOCK), F32)],
        compiler_params=pltpu.CompilerParams(dimension_semantics=("arbitrary",) * 3,
                                             vmem_limit_bytes=VMEM_LIMIT),
        name="attn_prompt",
    )(q, k, k, v, v)


def _pool_groups(window_sum, cur, counts, wp_ref, scale):
    outs = []
    for g, w in enumerate(POOL_WINDOWS):
        tot = window_sum(g, 0)
        for j in range(1, w):
            tot = tot + window_sum(g, j)
        pooled = tot / counts[g] - cur(g)
        outs.append(jnp.dot(pooled.astype(BF16), wp_ref[g], preferred_element_type=F32))
    return jnp.concatenate(outs, axis=-1) * scale


def _merge_kernel(x_ref, o_ref, ga_ref, gp_ref, u_ref, uh_ref, wp_ref, ps_ref, wo_ref, fg_ref, y_ref, ue):
    i = pl.program_id(1)
    hist_rows = uh_ref.shape[1]
    rows = u_ref.shape[1]
    ue[0:hist_rows] = jnp.where(i == 0, 0.0, uh_ref[0])
    ue[hist_rows:hist_rows + rows] = u_ref[0]

    pos = i * rows + lax.broadcasted_iota(jnp.int32, (rows, 1), 0)
    counts = [jnp.minimum(pos + 1, w).astype(F32) for w in POOL_WINDOWS]

    def window_sum(g, j):
        return ue[pl.ds(hist_rows - j, rows), g * POOL_GROUP_WIDTH:(g + 1) * POOL_GROUP_WIDTH]

    pool = _pool_groups(window_sum, lambda g: window_sum(g, 0), counts, wp_ref, ps_ref[...])
    mix = jnp.concatenate([o_ref[0].astype(F32) * _silu(ga_ref[0].astype(F32)),
                           pool * _silu(gp_ref[0].astype(F32))], axis=-1).astype(BF16)
    h = x_ref[0] + jnp.dot(mix, wo_ref[...], preferred_element_type=F32)
    y_ref[0] = _rms(h, fg_ref[...])


def _merge_prompt(x, o, ga, gp, u, wp_bf, pool_scale, wo_bf, final_g):
    b, s, _ = x.shape
    hist_rows = 16
    per_tile = MERGE_ROWS // hist_rows
    tok = lambda width: pl.BlockSpec((1, MERGE_ROWS, width), lambda bi, i: (bi, i, 0))
    const = lambda shape: pl.BlockSpec(shape, lambda bi, i: (0,) * len(shape))
    return pl.pallas_call(
        _merge_kernel,
        grid=(b, s // MERGE_ROWS),
        in_specs=[tok(D_MODEL), tok(ATTN_WIDTH), tok(ATTN_WIDTH), tok(POOL_WIDTH), tok(POOL_WIDTH),
                  pl.BlockSpec((1, hist_rows, POOL_WIDTH),
                               lambda bi, i: (bi, jnp.maximum(i * per_tile - 1, 0), 0)),
                  const((len(POOL_WINDOWS), POOL_GROUP_WIDTH, POOL_GROUP_WIDTH)),
                  const((1, POOL_WIDTH)),
                  pl.BlockSpec((D_MODEL, D_MODEL), lambda bi, i: (0, 0), pipeline_mode=pl.Buffered(1)),
                  const((1, D_MODEL))],
        out_specs=tok(D_MODEL),
        out_shape=jax.ShapeDtypeStruct((b, s, D_MODEL), F32),
        scratch_shapes=[pltpu.VMEM((hist_rows + MERGE_ROWS, POOL_WIDTH), F32)],
        compiler_params=pltpu.CompilerParams(dimension_semantics=("arbitrary", "arbitrary"),
                                             vmem_limit_bytes=VMEM_LIMIT),
        name="merge_prompt",
    )(x, o, ga, gp, u, u, wp_bf, pool_scale, wo_bf, final_g)


def _sample_attn_kernel(q_ref, kn_ref, vn_ref, ck_ref, cv_ref, o_ref, ok_ref, ov_ref, *, n_new):
    buf = ck_ref.shape[-1]
    q_rows = q_ref.shape[2]
    tail_lane = lax.broadcasted_iota(jnp.int32, (HEAD_DIM, LANES), 1)

    def multiplicity(delta):
        c = jnp.zeros(delta.shape, F32)
        for window, d in DILATED_PATTERNS:
            assert d & (d - 1) == 0, "power-of-two dilation: residue test is a bit mask"
            c = c + jnp.where(((delta & (d - 1)) == 0) & (delta <= window), 1.0, 0.0)
        return c

    qi = lax.broadcasted_iota(jnp.int32, (q_rows, buf), 0)
    cc = multiplicity(buf + qi - lax.broadcasted_iota(jnp.int32, (q_rows, buf), 1))
    qn = lax.broadcasted_iota(jnp.int32, (q_rows, LANES), 0)
    nl = lax.broadcasted_iota(jnp.int32, (q_rows, LANES), 1)
    cn = jnp.where(nl <= qn, multiplicity(qn - jnp.minimum(nl, qn)), 0.0)
    nt = (((1,), (1,)), ((), ()))

    for hh in range(ck_ref.shape[1]):
        kc = ck_ref[0, hh]
        vc = cv_ref[0, hh]
        knt = kn_ref[0, hh]
        vnt = vn_ref[0, hh]
        qh = (q_ref[0, hh] * ATTN_SCALE).astype(BF16)
        sc = jnp.dot(qh, kc.astype(BF16), preferred_element_type=F32)
        sn = jnp.dot(qh, knt.astype(BF16), preferred_element_type=F32)
        sc = jnp.where(cc > 0, sc, NEG)
        sn = jnp.where(cn > 0, sn, NEG)
        m = jnp.maximum(jnp.max(sc, axis=1, keepdims=True), jnp.max(sn, axis=1, keepdims=True))
        pc = cc * jnp.exp(sc - m)
        pn = cn * jnp.exp(sn - m)
        den = jnp.sum(pc, axis=1, keepdims=True) + jnp.sum(pn, axis=1, keepdims=True)
        oc = (lax.dot_general(pc.astype(BF16), vc.astype(BF16), nt, preferred_element_type=F32)
              + lax.dot_general(pn.astype(BF16), vnt.astype(BF16), nt, preferred_element_type=F32))
        o_ref[0, hh] = oc / den

        for src, new, dst in ((kc, knt, ok_ref), (vc, vnt, ov_ref)):
            rolled = pltpu.roll(src, buf - n_new, 1)
            tail = rolled[:, buf - LANES:]
            placed = pltpu.roll(new, LANES - n_new, 1)
            dst[0, hh, :, 0:buf - LANES] = rolled[:, 0:buf - LANES]
            dst[0, hh, :, buf - LANES:buf] = jnp.where(tail_lane >= LANES - n_new, placed, tail)


def _attend_sample(q, knt, vnt, ck, cv, n_new):
    b, h, dh, buf = ck.shape
    q_rows = q.shape[2]
    rows = pl.BlockSpec((1, SAMPLE_HEADS, q_rows, dh), lambda bi, hg: (bi, hg, 0, 0))
    small = pl.BlockSpec((1, SAMPLE_HEADS, dh, LANES), lambda bi, hg: (bi, hg, 0, 0))
    big = pl.BlockSpec((1, SAMPLE_HEADS, dh, buf), lambda bi, hg: (bi, hg, 0, 0))
    return pl.pallas_call(
        functools.partial(_sample_attn_kernel, n_new=n_new),
        grid=(b, h // SAMPLE_HEADS),
        in_specs=[rows, small, small, big, big],
        out_specs=[rows, big, big],
        out_shape=[jax.ShapeDtypeStruct((b, h, q_rows, dh), F32),
                   jax.ShapeDtypeStruct((b, h, dh, buf), F32),
                   jax.ShapeDtypeStruct((b, h, dh, buf), F32)],
        compiler_params=pltpu.CompilerParams(dimension_semantics=("arbitrary", "arbitrary"),
                                             vmem_limit_bytes=VMEM_LIMIT),
        name="attn_sample",
    )(q, knt, vnt, ck, cv)


def _merge_sample_kernel(x_ref, o_ref, z_ref, hist_ref, wp_ref, ps_ref, wo_ref, fg_ref, y_ref, pool_ref, ue,
                         *, n_new, n_batch):
    u_col = 4 * ATTN_WIDTH
    ue[0:POOL_HIST] = hist_ref[...]
    ue[POOL_HIST:POOL_HIST + n_new] = z_ref[:, u_col:u_col + POOL_WIDTH].reshape(n_new, n_batch, POOL_WIDTH)
    pool_ref[...] = ue[n_new:n_new + POOL_HIST]

    rows = n_new * n_batch

    def window_sum(g, j):
        blk = ue[pl.ds(POOL_HIST - j, n_new), :, g * POOL_GROUP_WIDTH:(g + 1) * POOL_GROUP_WIDTH]
        return blk.reshape(rows, POOL_GROUP_WIDTH)

    pos = PAST_LEN + lax.broadcasted_iota(jnp.int32, (rows, 1), 0) // n_batch
    counts = [jnp.minimum(pos + 1, w).astype(F32) for w in POOL_WINDOWS]
    pool = _pool_groups(window_sum, lambda g: window_sum(g, 0), counts, wp_ref, ps_ref[...])
    ga = z_ref[:, 3 * ATTN_WIDTH:4 * ATTN_WIDTH]
    gp = z_ref[:, u_col + POOL_WIDTH:u_col + 2 * POOL_WIDTH]
    mix = jnp.concatenate([o_ref[...] * _silu(ga), pool * _silu(gp)], axis=-1).astype(BF16)
    h = x_ref[...] + jnp.dot(mix, wo_ref[...], preferred_element_type=F32)
    y_ref[...] = _rms(h, fg_ref[...])


def _merge_sample(x_tb, o_tb, z_tb, hist, wp_bf, pool_scale, wo_bf, final_g, n_new, n_batch):
    return pl.pallas_call(
        functools.partial(_merge_sample_kernel, n_new=n_new, n_batch=n_batch),
        out_shape=[jax.ShapeDtypeStruct(x_tb.shape, F32),
                   jax.ShapeDtypeStruct((POOL_HIST, n_batch, POOL_WIDTH), F32)],
        scratch_shapes=[pltpu.VMEM((POOL_HIST + n_new, n_batch, POOL_WIDTH), F32)],
        compiler_params=pltpu.CompilerParams(vmem_limit_bytes=VMEM_LIMIT),
        name="merge_sample",
    )(x_tb, o_tb, z_tb, hist, wp_bf, pool_scale, wo_bf, final_g)


def _to_feature_major(z_cols, n_new, n_batch):
    t = z_cols.reshape(n_new, n_batch, N_HEADS, HEAD_DIM).transpose(1, 2, 3, 0)
    return jnp.pad(t, ((0, 0), (0, 0), (0, 0), (0, LANES - n_new)))


def kernel(x_prompt, x_sample, cache_k, cache_v, state_pool, norm_g, w_in, w_pool, pool_scale, w_out, final_norm_g):
    depth = norm_g.shape[0]
    assert depth == 1, "single-layer step"
    b, s, _ = x_prompt.shape
    n_batch, n_new, _ = x_sample.shape
    buf = cache_k.shape[2]
    assert s % SUPER == 0 and s >= MAX_WINDOW and buf == MAX_WINDOW and n_new <= POOL_HIST

    g = norm_g[0][None, :]
    fg = final_norm_g[None, :]
    w_bf = w_in[0].astype(BF16)
    wo_bf = w_out[0].astype(BF16)
    wp_bf = w_pool[0].astype(BF16)
    ps = pool_scale[0][None, :]

    q, k, v, ga, u, gp, kt, vt = _project_prompt(x_prompt, g, w_bf)
    o = _attend_prompt(q, k, v)
    y_prompt = _merge_prompt(x_prompt, o, ga, gp, u, wp_bf, ps, wo_bf, fg)
    to_window = lambda t: t.reshape(b, N_HEADS, HEAD_DIM, MAX_WINDOW).transpose(0, 3, 1, 2)[None]
    k_prompt, v_prompt = to_window(kt), to_window(vt)
    pool_prompt = u[:, s - POOL_HIST:][None]

    x_tb = x_sample.transpose(1, 0, 2).reshape(n_new * n_batch, D_MODEL)
    z_tb = _project_sample(x_tb, g, w_bf)
    q_s = z_tb[:, 0:ATTN_WIDTH].reshape(n_new, n_batch, N_HEADS, HEAD_DIM).transpose(1, 2, 0, 3)
    q_s = jnp.pad(q_s, ((0, 0), (0, 0), (0, SUBLANES - n_new), (0, 0)))
    knt = _to_feature_major(z_tb[:, ATTN_WIDTH:2 * ATTN_WIDTH], n_new, n_batch)
    vnt = _to_feature_major(z_tb[:, 2 * ATTN_WIDTH:3 * ATTN_WIDTH], n_new, n_batch)
    ck = cache_k[0].transpose(0, 2, 3, 1)
    cv = cache_v[0].transpose(0, 2, 3, 1)
    o_s, ok, ov = _attend_sample(q_s, knt, vnt, ck, cv, n_new)
    o_tb = o_s[:, :, :n_new].transpose(2, 0, 1, 3).reshape(n_new * n_batch, ATTN_WIDTH)
    hist = state_pool[0].transpose(1, 0, 2)
    y_tb, pool_tb = _merge_sample(x_tb, o_tb, z_tb, hist, wp_bf, ps, wo_bf, fg, n_new, n_batch)
    y_sample = y_tb.reshape(n_new, n_batch, D_MODEL).transpose(1, 0, 2)
    k_sample = ok.transpose(0, 3, 1, 2)[None]
    v_sample = ov.transpose(0, 3, 1, 2)[None]
    pool_sample = pool_tb.transpose(1, 0, 2)[None]

    return (y_prompt, y_sample, k_prompt, v_prompt, pool_prompt, k_sample, v_sample, pool_sample)
```

```python
import functools

import jax
import jax.numpy as jnp
from jax import lax
from jax.experimental import pallas as pl
from jax.experimental.pallas import tpu as pltpu

F32 = jnp.float32
BF16 = jnp.bfloat16

D_MODEL = 2048
ATTN_WIDTH = 1024
HEAD_DIM = 64
N_HEADS = 16
POOL_WIDTH = 1024
POOL_WINDOWS = (2, 4, 8, 16)
POOL_GROUP_WIDTH = 256
POOL_HIST = 15
IN_WIDTH = 6144
DILATED_PATTERNS = ((128, 1), (512, 4), (2048, 16))
MAX_WINDOW = 2048
BLOCK = 128
PAST_LEN = 16384
EPS = 1e-6
ATTN_SCALE = HEAD_DIM ** -0.5
NEG = -1e30
LOG2E = 1.4426950408889634

LANES = 128
SUBLANES = 8
HEAD_PAIR = LANES // HEAD_DIM

PROJ_ROWS = 256
MERGE_ROWS = 256
SUPER = 2048
SAMPLE_HEADS = 4
CAST_COLS = 1024
VMEM_LIMIT = 56 * 1024 * 1024


def _silu(x):
    return x / (1.0 + jnp.exp(-x))


def _rms(x, g):
    ms = jnp.mean(x * x, axis=-1, keepdims=True)
    return x * lax.rsqrt(ms + EPS) * g


def _proj_kernel(x_ref, g_ref, w_ref, q_ref, k_ref, v_ref, ga_ref, u_ref, gp_ref, kt_ref, vt_ref,
                 *, first_kept_tile):
    i = pl.program_id(1)
    xn = _rms(x_ref[0], g_ref[...]).astype(BF16)

    def part(c):
        return jnp.dot(xn, w_ref[:, c * ATTN_WIDTH:(c + 1) * ATTN_WIDTH], preferred_element_type=F32)

    q_ref[0] = part(0) * (ATTN_SCALE * LOG2E)
    k = part(1)
    k_ref[0] = k
    v = part(2)
    v_ref[0] = v
    ga_ref[0] = part(3).astype(BF16)
    u_ref[0] = part(4)
    gp_ref[0] = part(5).astype(BF16)

    @pl.when(i >= first_kept_tile)
    def _():
        kt_ref[0] = k.T
        vt_ref[0] = v.T


def _project_prompt(x, g, w_bf):
    b, s, _ = x.shape
    tiles = s // PROJ_ROWS
    first_kept = (s - MAX_WINDOW) // PROJ_ROWS
    tok = lambda width: pl.BlockSpec((1, PROJ_ROWS, width), lambda bi, i: (bi, i, 0))
    feat = pl.BlockSpec((1, ATTN_WIDTH, PROJ_ROWS), lambda bi, i: (bi, 0, jnp.maximum(i - first_kept, 0)))
    return pl.pallas_call(
        functools.partial(_proj_kernel, first_kept_tile=first_kept),
        grid=(b, tiles),
        in_specs=[tok(D_MODEL),
                  pl.BlockSpec((1, D_MODEL), lambda bi, i: (0, 0)),
                  pl.BlockSpec((D_MODEL, IN_WIDTH), lambda bi, i: (0, 0), pipeline_mode=pl.Buffered(1))],
        out_specs=[tok(ATTN_WIDTH)] * 6 + [feat, feat],
        out_shape=[jax.ShapeDtypeStruct((b, s, ATTN_WIDTH), F32)] * 3
        + [jax.ShapeDtypeStruct((b, s, ATTN_WIDTH), BF16), jax.ShapeDtypeStruct((b, s, POOL_WIDTH), F32),
           jax.ShapeDtypeStruct((b, s, POOL_WIDTH), BF16)]
        + [jax.ShapeDtypeStruct((b, ATTN_WIDTH, MAX_WINDOW), F32)] * 2,
        compiler_params=pltpu.CompilerParams(dimension_semantics=("arbitrary", "arbitrary"),
                                             vmem_limit_bytes=VMEM_LIMIT),
        name="proj_prompt",
    )(x, g, w_bf)


def _proj_sample_kernel(x_ref, g_ref, w_ref, z_ref, wbf_ref):
    xn = _rms(x_ref[...], g_ref[...]).astype(BF16)
    w = w_ref[...].astype(BF16)
    wbf_ref[...] = w
    z_ref[...] = jnp.dot(xn, w, preferred_element_type=F32)


def _project_sample(x_tb, g, w):
    rows = x_tb.shape[0]
    cols = pl.BlockSpec((D_MODEL, CAST_COLS), lambda c: (0, c))
    return pl.pallas_call(
        _proj_sample_kernel,
        grid=(IN_WIDTH // CAST_COLS,),
        in_specs=[pl.BlockSpec((rows, D_MODEL), lambda c: (0, 0)),
                  pl.BlockSpec((1, D_MODEL), lambda c: (0, 0)),
                  cols],
        out_specs=[pl.BlockSpec((rows, CAST_COLS), lambda c: (0, c)), cols],
        out_shape=[jax.ShapeDtypeStruct((rows, IN_WIDTH), F32),
                   jax.ShapeDtypeStruct((D_MODEL, IN_WIDTH), BF16)],
        compiler_params=pltpu.CompilerParams(dimension_semantics=("arbitrary",),
                                             vmem_limit_bytes=VMEM_LIMIT),
        name="proj_sample",
    )(x_tb, g, w)


def _attn_kernel(q_ref, kp_ref, kc_ref, vp_ref, vc_ref, o_ref, acc, mx, sm, bias):
    first_super = (pl.program_id(1) == 0).astype(jnp.int32)

    qq = lax.broadcasted_iota(jnp.int32, (HEAD_PAIR * BLOCK, 2 * BLOCK), 0) % BLOCK
    kk = lax.broadcasted_iota(jnp.int32, (HEAD_PAIR * BLOCK, 2 * BLOCK), 1)
    band = (kk - qq >= 0) & (kk - qq <= BLOCK)
    bias[0] = jnp.where(band, 0.0, NEG)
    bias[1] = jnp.where(band & (kk >= BLOCK), 0.0, NEG)

    lane = lax.broadcasted_iota(jnp.int32, (BLOCK, LANES), 1)
    low = lane < HEAD_DIM
    nt = (((1,), (1,)), ((), ()))

    def keys(prev_ref, cur_ref, qs, d):
        span = d * BLOCK
        if qs >= span:
            return cur_ref[pl.ds(qs - span, 2 * BLOCK, stride=d), :]
        return jnp.concatenate([prev_ref[pl.ds(SUPER + qs - span, BLOCK, stride=d), :],
                                cur_ref[pl.ds(qs, BLOCK, stride=d), :]], axis=0)

    for pat, (_, d) in enumerate(DILATED_PATTERNS):
        for r in range(d):
            for j in range(SUPER // (d * BLOCK)):
                qs = r + d * BLOCK * j
                rows = pl.ds(qs, BLOCK, stride=d)
                qb = q_ref[rows, :]
                q2 = jnp.concatenate([jnp.where(low, qb, 0.0), jnp.where(low, 0.0, qb)], axis=0).astype(BF16)
                kb = keys(kp_ref, kc_ref, qs, d).astype(BF16)
                vb = keys(vp_ref, vc_ref, qs, d).astype(BF16)
                s = lax.dot_general(q2, kb, nt, preferred_element_type=F32)
                s = s + (bias[first_super] if j == 0 else bias[0])
                m = jnp.max(s, axis=1, keepdims=True)
                p = jnp.exp2(s - m)
                l = jnp.sum(p, axis=1, keepdims=True)
                o2 = jnp.dot(p.astype(BF16), vb, preferred_element_type=F32)
                acc[pat, rows, :] = jnp.where(low, o2[:BLOCK], o2[BLOCK:])
                mx[pat, rows, :] = jnp.where(low, m[:BLOCK], m[BLOCK:])
                sm[pat, rows, :] = jnp.where(low, l[:BLOCK], l[BLOCK:])

    m_all = jnp.maximum(jnp.maximum(mx[0], mx[1]), mx[2])
    num = jnp.zeros((SUPER, LANES), F32)
    den = jnp.zeros((SUPER, LANES), F32)
    for pat in range(len(DILATED_PATTERNS)):
        w = jnp.exp2(mx[pat] - m_all)
        num = num + w * acc[pat]
        den = den + w * sm[pat]
    o_ref[...] = (num / den).astype(BF16)


def _attend_prompt(q, k, v):
    b, s, _ = q.shape
    n_super = s // SUPER
    cur = pl.BlockSpec((None, SUPER, LANES), lambda bi, sb, hp: (bi, sb, hp))
    prev = pl.BlockSpec((None, SUPER, LANES), lambda bi, sb, hp: (bi, jnp.maximum(sb - 1, 0), hp))
    n_pat = len(DILATED_PATTERNS)
    return pl.pallas_call(
        _attn_kernel,
        grid=(b, n_super, N_HEADS // HEAD_PAIR),
        in_specs=[cur, prev, cur, prev, cur],
        out_specs=cur,
        out_shape=jax.ShapeDtypeStruct((b, s, ATTN_WIDTH), BF16),
        scratch_shapes=[pltpu.VMEM((n_pat, SUPER, LANES), F32),
                        pltpu.VMEM((n_pat, SUPER, LANES), F32),
                        pltpu.VMEM((n_pat, SUPER, LANES), F32),
                        pltpu.VMEM((2, HEAD_PAIR * BLOCK, 2 * BLOCK), F32)],
        compiler_params=pltpu.CompilerParams(dimension_semantics=("arbitrary",) * 3,
                                             vmem_limit_bytes=VMEM_LIMIT),
        name="attn_prompt",
    )(q, k, k, v, v)


def _pool_groups(window_sum, cur, counts, wp_ref, scale):
    outs = []
    for g, w in enumerate(POOL_WINDOWS):
        tot = window_sum(g, 0)
        for j in range(1, w):
            tot = tot + window_sum(g, j)
        pooled = tot / counts[g] - cur(g)
        outs.append(jnp.dot(pooled.astype(BF16), wp_ref[g], preferred_element_type=F32))
    return jnp.concatenate(outs, axis=-1) * scale


def _merge_rows(x_ref, o_ref, ga_ref, gp_ref, u_ref, uh_ref, wp_ref, ps_ref, wo_ref, fg_ref, y_ref, ue):
    i = pl.program_id(1)
    hist_rows = uh_ref.shape[1]
    rows = u_ref.shape[1]
    ue[0:hist_rows] = jnp.where(i == 0, 0.0, uh_ref[0])
    ue[hist_rows:hist_rows + rows] = u_ref[0]

    pos = i * rows + lax.broadcasted_iota(jnp.int32, (rows, 1), 0)
    counts = [jnp.minimum(pos + 1, w).astype(F32) for w in POOL_WINDOWS]

    def window_sum(g, j):
        return ue[pl.ds(hist_rows - j, rows), g * POOL_GROUP_WIDTH:(g + 1) * POOL_GROUP_WIDTH]

    pool = _pool_groups(window_sum, lambda g: window_sum(g, 0), counts, wp_ref, ps_ref[...])
    mix = jnp.concatenate([o_ref[0].astype(F32) * _silu(ga_ref[0].astype(F32)),
                           pool * _silu(gp_ref[0].astype(F32))], axis=-1).astype(BF16)
    h = x_ref[0] + jnp.dot(mix, wo_ref[...], preferred_element_type=F32)
    y_ref[0] = _rms(h, fg_ref[...])


def _merge_kernel(x_ref, o_ref, ga_ref, gp_ref, u_ref, uh_ref, wp_ref, ps_ref, wo_ref, fg_ref,
                  qs_ref, kn_ref, vn_ref, ck_ref, cv_ref,
                  y_ref, os_ref, ok_ref, ov_ref, ue, *, n_new):
    _merge_rows(x_ref, o_ref, ga_ref, gp_ref, u_ref, uh_ref, wp_ref, ps_ref, wo_ref, fg_ref, y_ref, ue)
    _sample_attend(qs_ref, kn_ref, vn_ref, ck_ref, cv_ref, os_ref, ok_ref, ov_ref, n_new)


def _merge_prompt_attend_sample(x, o, ga, gp, u, wp_bf, pool_scale, wo_bf, final_g, q_s, knt, vnt, ck, cv, n_new):
    b, s, _ = x.shape
    n_batch, h, dh, buf = ck.shape
    q_rows = q_s.shape[2]
    tiles = s // MERGE_ROWS
    head_groups = h // SAMPLE_HEADS
    assert b * tiles == n_batch * head_groups, "one sample head group per prompt tile"
    hist_rows = 16
    per_tile = MERGE_ROWS // hist_rows
    tok = lambda width: pl.BlockSpec((1, MERGE_ROWS, width), lambda bi, i: (bi, i, 0))
    const = lambda shape: pl.BlockSpec(shape, lambda bi, i: (0,) * len(shape))

    def heads(*tail):
        def index(bi, i):
            step = bi * tiles + i
            return (step // head_groups, step % head_groups, 0, 0)
        return pl.BlockSpec((1, SAMPLE_HEADS) + tail, index)

    rows, small, big = heads(q_rows, dh), heads(dh, LANES), heads(dh, buf)
    return pl.pallas_call(
        functools.partial(_merge_kernel, n_new=n_new),
        grid=(b, tiles),
        in_specs=[tok(D_MODEL), tok(ATTN_WIDTH), tok(ATTN_WIDTH), tok(POOL_WIDTH), tok(POOL_WIDTH),
                  pl.BlockSpec((1, hist_rows, POOL_WIDTH),
                               lambda bi, i: (bi, jnp.maximum(i * per_tile - 1, 0), 0)),
                  const((len(POOL_WINDOWS), POOL_GROUP_WIDTH, POOL_GROUP_WIDTH)),
                  const((1, POOL_WIDTH)),
                  pl.BlockSpec((D_MODEL, D_MODEL), lambda bi, i: (0, 0), pipeline_mode=pl.Buffered(1)),
                  const((1, D_MODEL)),
                  rows, small, small, big, big],
        out_specs=[tok(D_MODEL), rows, big, big],
        out_shape=[jax.ShapeDtypeStruct((b, s, D_MODEL), F32),
                   jax.ShapeDtypeStruct((n_batch, h, q_rows, dh), F32),
                   jax.ShapeDtypeStruct((n_batch, h, dh, buf), F32),
                   jax.ShapeDtypeStruct((n_batch, h, dh, buf), F32)],
        scratch_shapes=[pltpu.VMEM((hist_rows + MERGE_ROWS, POOL_WIDTH), F32)],
        compiler_params=pltpu.CompilerParams(dimension_semantics=("arbitrary", "arbitrary"),
                                             vmem_limit_bytes=VMEM_LIMIT),
        name="merge_prompt_attn_sample",
    )(x, o, ga, gp, u, u, wp_bf, pool_scale, wo_bf, final_g, q_s, knt, vnt, ck, cv)


def _sample_attend(q_ref, kn_ref, vn_ref, ck_ref, cv_ref, o_ref, ok_ref, ov_ref, n_new):
    buf = ck_ref.shape[-1]
    q_rows = q_ref.shape[2]
    tail_lane = lax.broadcasted_iota(jnp.int32, (HEAD_DIM, LANES), 1)

    def multiplicity(delta):
        c = jnp.zeros(delta.shape, F32)
        for window, d in DILATED_PATTERNS:
            assert d & (d - 1) == 0, "power-of-two dilation: residue test is a bit mask"
            c = c + jnp.where(((delta & (d - 1)) == 0) & (delta <= window), 1.0, 0.0)
        return c

    qi = lax.broadcasted_iota(jnp.int32, (q_rows, buf), 0)
    cc = multiplicity(buf + qi - lax.broadcasted_iota(jnp.int32, (q_rows, buf), 1))
    qn = lax.broadcasted_iota(jnp.int32, (q_rows, LANES), 0)
    nl = lax.broadcasted_iota(jnp.int32, (q_rows, LANES), 1)
    cn = jnp.where(nl <= qn, multiplicity(qn - jnp.minimum(nl, qn)), 0.0)
    nt = (((1,), (1,)), ((), ()))

    for hh in range(ck_ref.shape[1]):
        kc = ck_ref[0, hh]
        vc = cv_ref[0, hh]
        knt = kn_ref[0, hh]
        vnt = vn_ref[0, hh]
        qh = (q_ref[0, hh] * ATTN_SCALE).astype(BF16)
        sc = jnp.dot(qh, kc.astype(BF16), preferred_element_type=F32)
        sn = jnp.dot(qh, knt.astype(BF16), preferred_element_type=F32)
        sc = jnp.where(cc > 0, sc, NEG)
        sn = jnp.where(cn > 0, sn, NEG)
        m = jnp.maximum(jnp.max(sc, axis=1, keepdims=True), jnp.max(sn, axis=1, keepdims=True))
        pc = cc * jnp.exp(sc - m)
        pn = cn * jnp.exp(sn - m)
        den = jnp.sum(pc, axis=1, keepdims=True) + jnp.sum(pn, axis=1, keepdims=True)
        oc = (lax.dot_general(pc.astype(BF16), vc.astype(BF16), nt, preferred_element_type=F32)
              + lax.dot_general(pn.astype(BF16), vnt.astype(BF16), nt, preferred_element_type=F32))
        o_ref[0, hh] = oc / den

        for src, new, dst in ((kc, knt, ok_ref), (vc, vnt, ov_ref)):
            rolled = pltpu.roll(src, buf - n_new, 1)
            tail = rolled[:, buf - LANES:]
            placed = pltpu.roll(new, LANES - n_new, 1)
            dst[0, hh, :, 0:buf - LANES] = rolled[:, 0:buf - LANES]
            dst[0, hh, :, buf - LANES:buf] = jnp.where(tail_lane >= LANES - n_new, placed, tail)


def _merge_sample_kernel(x_ref, o_ref, z_ref, hist_ref, wp_ref, ps_ref, wo_ref, fg_ref, y_ref, pool_ref, ue,
                         *, n_new, n_batch):
    u_col = 4 * ATTN_WIDTH
    ue[0:POOL_HIST] = hist_ref[...]
    ue[POOL_HIST:POOL_HIST + n_new] = z_ref[:, u_col:u_col + POOL_WIDTH].reshape(n_new, n_batch, POOL_WIDTH)
    pool_ref[...] = ue[n_new:n_new + POOL_HIST]

    rows = n_new * n_batch

    def window_sum(g, j):
        blk = ue[pl.ds(POOL_HIST - j, n_new), :, g * POOL_GROUP_WIDTH:(g + 1) * POOL_GROUP_WIDTH]
        return blk.reshape(rows, POOL_GROUP_WIDTH)

    pos = PAST_LEN + lax.broadcasted_iota(jnp.int32, (rows, 1), 0) // n_batch
    counts = [jnp.minimum(pos + 1, w).astype(F32) for w in POOL_WINDOWS]
    pool = _pool_groups(window_sum, lambda g: window_sum(g, 0), counts, wp_ref, ps_ref[...])
    ga = z_ref[:, 3 * ATTN_WIDTH:4 * ATTN_WIDTH]
    gp = z_ref[:, u_col + POOL_WIDTH:u_col + 2 * POOL_WIDTH]
    mix = jnp.concatenate([o_ref[...] * _silu(ga), pool * _silu(gp)], axis=-1).astype(BF16)
    h = x_ref[...] + jnp.dot(mix, wo_ref[...], preferred_element_type=F32)
    y_ref[...] = _rms(h, fg_ref[...])


def _merge_sample(x_tb, o_tb, z_tb, hist, wp_bf, pool_scale, wo_bf, final_g, n_new, n_batch):
    return pl.pallas_call(
        functools.partial(_merge_sample_kernel, n_new=n_new, n_batch=n_batch),
        out_shape=[jax.ShapeDtypeStruct(x_tb.shape, F32),
                   jax.ShapeDtypeStruct((POOL_HIST, n_batch, POOL_WIDTH), F32)],
        scratch_shapes=[pltpu.VMEM((POOL_HIST + n_new, n_batch, POOL_WIDTH), F32)],
        compiler_params=pltpu.CompilerParams(vmem_limit_bytes=VMEM_LIMIT),
        name="merge_sample",
    )(x_tb, o_tb, z_tb, hist, wp_bf, pool_scale, wo_bf, final_g)


def _to_feature_major(z_cols, n_new, n_batch):
    t = z_cols.reshape(n_new, n_batch, N_HEADS, HEAD_DIM).transpose(1, 2, 3, 0)
    return jnp.pad(t, ((0, 0), (0, 0), (0, 0), (0, LANES - n_new)))


def kernel(x_prompt, x_sample, cache_k, cache_v, state_pool, norm_g, w_in, w_pool, pool_scale, w_out, final_norm_g):
    depth = norm_g.shape[0]
    assert depth == 1, "single-layer step"
    b, s, _ = x_prompt.shape
    n_batch, n_new, _ = x_sample.shape
    buf = cache_k.shape[2]
    assert s % SUPER == 0 and s >= MAX_WINDOW and buf == MAX_WINDOW and n_new <= POOL_HIST

    g = norm_g[0][None, :]
    fg = final_norm_g[None, :]
    wo_bf = w_out[0].astype(BF16)
    wp_bf = w_pool[0].astype(BF16)
    ps = pool_scale[0][None, :]

    x_tb = x_sample.transpose(1, 0, 2).reshape(n_new * n_batch, D_MODEL)
    z_tb, w_bf = _project_sample(x_tb, g, w_in[0])
    q_s = z_tb[:, 0:ATTN_WIDTH].reshape(n_new, n_batch, N_HEADS, HEAD_DIM).transpose(1, 2, 0, 3)
    q_s = jnp.pad(q_s, ((0, 0), (0, 0), (0, SUBLANES - n_new), (0, 0)))
    knt = _to_feature_major(z_tb[:, ATTN_WIDTH:2 * ATTN_WIDTH], n_new, n_batch)
    vnt = _to_feature_major(z_tb[:, 2 * ATTN_WIDTH:3 * ATTN_WIDTH], n_new, n_batch)
    ck = cache_k[0].transpose(0, 2, 3, 1)
    cv = cache_v[0].transpose(0, 2, 3, 1)

    q, k, v, ga, u, gp, kt, vt = _project_prompt(x_prompt, g, w_bf)
    o = _attend_prompt(q, k, v)
    y_prompt, o_s, ok, ov = _merge_prompt_attend_sample(x_prompt, o, ga, gp, u, wp_bf, ps, wo_bf, fg,
                                                        q_s, knt, vnt, ck, cv, n_new)
    to_window = lambda t: t.reshape(b, N_HEADS, HEAD_DIM, MAX_WINDOW).transpose(0, 3, 1, 2)[None]
    k_prompt, v_prompt = to_window(kt), to_window(vt)
    pool_prompt = u[:, s - POOL_HIST:][None]

    o_tb = o_s[:, :, :n_new].transpose(2, 0, 1, 3).reshape(n_new * n_batch, ATTN_WIDTH)
    hist = state_pool[0].transpose(1, 0, 2)
    y_tb, pool_tb = _merge_sample(x_tb, o_tb, z_tb, hist, wp_bf, ps, wo_bf, fg, n_new, n_batch)
    y_sample = y_tb.reshape(n_new, n_batch, D_MODEL).transpose(1, 0, 2)
    k_sample = ok.transpose(0, 3, 1, 2)[None]
    v_sample = ov.transpose(0, 3, 1, 2)[None]
    pool_sample = pool_tb.transpose(1, 0, 2)[None]

    return (y_prompt, y_sample, k_prompt, v_prompt, pool_prompt, k_sample, v_sample, pool_sample)
```

```python
import functools

import jax
import jax.numpy as jnp
from jax import lax
from jax.experimental import pallas as pl
from jax.experimental.pallas import tpu as pltpu

F32 = jnp.float32
BF16 = jnp.bfloat16

D_MODEL = 2048
ATTN_WIDTH = 1024
HEAD_DIM = 64
N_HEADS = 16
POOL_WIDTH = 1024
POOL_WINDOWS = (2, 4, 8, 16)
POOL_GROUP_WIDTH = 256
POOL_HIST = 15
IN_WIDTH = 6144
DILATED_PATTERNS = ((128, 1), (512, 4), (2048, 16))
MAX_WINDOW = 2048
BLOCK = 128
PAST_LEN = 16384
EPS = 1e-6
ATTN_SCALE = HEAD_DIM ** -0.5
NEG = -1e30
LOG2E = 1.4426950408889634

LANES = 128
SUBLANES = 8
HEAD_PAIR = LANES // HEAD_DIM
HIST_ROWS = 16

PROJ_ROWS = 256
MERGE_ROWS = 256
SUPER = 2048
SAMPLE_HEADS = 4
CAST_COLS = 1024
VMEM_LIMIT = 58 * 1024 * 1024


def _silu(x):
    return x / (1.0 + jnp.exp(-x))


def _rms(x, g):
    ms = jnp.mean(x * x, axis=-1, keepdims=True)
    return x * lax.rsqrt(ms + EPS) * g


def _group_cols(g):
    return slice(g * POOL_GROUP_WIDTH, (g + 1) * POOL_GROUP_WIDTH)


def _pooled(window_sum, counts):
    outs = []
    for g, w in enumerate(POOL_WINDOWS):
        cur = window_sum(g, 0)
        tot = cur
        for j in range(1, w):
            tot = tot + window_sum(g, j)
        outs.append(tot / counts[g] - cur)
    return outs


def _pool_project(pooled_group, wp_ref, scale):
    outs = [jnp.dot(pooled_group(g), wp_ref[g], preferred_element_type=F32) for g in range(len(POOL_WINDOWS))]
    return jnp.concatenate(outs, axis=-1) * scale


def _multiplicity(delta):
    c = jnp.zeros(delta.shape, F32)
    for window, d in DILATED_PATTERNS:
        assert d & (d - 1) == 0, "power-of-two dilation: residue test is a bit mask"
        c = c + jnp.where(((delta & (d - 1)) == 0) & (delta <= window), 1.0, 0.0)
    return c


def _shift_cache(src, new, dst_ref, hh, n_new):
    buf = src.shape[-1]
    tail_lane = lax.broadcasted_iota(jnp.int32, (HEAD_DIM, LANES), 1)
    rolled = pltpu.roll(src, buf - n_new, 1)
    tail = rolled[:, buf - LANES:]
    placed = pltpu.roll(new, LANES - n_new, 1)
    dst_ref[0, hh, :, 0:buf - LANES] = rolled[:, 0:buf - LANES]
    dst_ref[0, hh, :, buf - LANES:buf] = jnp.where(tail_lane >= LANES - n_new, placed, tail)


def _sample_scores(q_ref, kn_ref, ck_ref, pc_ref, pn_ref, ok_ref, n_new):
    buf = ck_ref.shape[-1]
    q_rows = q_ref.shape[2]
    qi = lax.broadcasted_iota(jnp.int32, (q_rows, buf), 0)
    cc = _multiplicity(buf + qi - lax.broadcasted_iota(jnp.int32, (q_rows, buf), 1))
    qn = lax.broadcasted_iota(jnp.int32, (q_rows, LANES), 0)
    nl = lax.broadcasted_iota(jnp.int32, (q_rows, LANES), 1)
    cn = jnp.where(nl <= qn, _multiplicity(qn - jnp.minimum(nl, qn)), 0.0)

    for hh in range(ck_ref.shape[1]):
        kc = ck_ref[0, hh]
        knt = kn_ref[0, hh]
        qh = (q_ref[0, hh] * ATTN_SCALE).astype(BF16)
        sc = jnp.dot(qh, kc.astype(BF16), preferred_element_type=F32)
        sn = jnp.dot(qh, knt.astype(BF16), preferred_element_type=F32)
        sc = jnp.where(cc > 0, sc, NEG)
        sn = jnp.where(cn > 0, sn, NEG)
        m = jnp.maximum(jnp.max(sc, axis=1, keepdims=True), jnp.max(sn, axis=1, keepdims=True))
        pc = cc * jnp.exp(sc - m)
        pn = cn * jnp.exp(sn - m)
        den = jnp.sum(pc, axis=1, keepdims=True) + jnp.sum(pn, axis=1, keepdims=True)
        pc_ref[0, hh] = pc / den
        pn_ref[0, hh] = pn / den
        _shift_cache(kc, knt, ok_ref, hh, n_new)


def _sample_values(pc_ref, pn_ref, vn_ref, cv_ref, o_ref, ov_ref, n_new):
    nt = (((1,), (1,)), ((), ()))
    for hh in range(cv_ref.shape[1]):
        vc = cv_ref[0, hh]
        vnt = vn_ref[0, hh]
        o_ref[0, hh] = (
            lax.dot_general(pc_ref[0, hh].astype(BF16), vc.astype(BF16), nt, preferred_element_type=F32)
            + lax.dot_general(pn_ref[0, hh].astype(BF16), vnt.astype(BF16), nt, preferred_element_type=F32))
        _shift_cache(vc, vnt, ov_ref, hh, n_new)


def _hosted_heads(tiles, head_groups):
    def spec(*tail):
        def index(bi, i):
            step = bi * tiles + i
            return (step // head_groups, step % head_groups, 0, 0)
        return pl.BlockSpec((1, SAMPLE_HEADS) + tail, index)
    return spec


def _proj_sample_kernel(x_ref, g_ref, w_ref, z_ref, wbf_ref):
    xn = _rms(x_ref[...], g_ref[...]).astype(BF16)
    w = w_ref[...].astype(BF16)
    wbf_ref[...] = w
    z_ref[...] = jnp.dot(xn, w, preferred_element_type=F32)


def _project_sample(x_tb, g, w):
    rows = x_tb.shape[0]
    cols = pl.BlockSpec((D_MODEL, CAST_COLS), lambda c: (0, c))
    return pl.pallas_call(
        _proj_sample_kernel,
        grid=(IN_WIDTH // CAST_COLS,),
        in_specs=[pl.BlockSpec((rows, D_MODEL), lambda c: (0, 0)),
                  pl.BlockSpec((1, D_MODEL), lambda c: (0, 0)),
                  cols],
        out_specs=[pl.BlockSpec((rows, CAST_COLS), lambda c: (0, c)), cols],
        out_shape=[jax.ShapeDtypeStruct((rows, IN_WIDTH), F32),
                   jax.ShapeDtypeStruct((D_MODEL, IN_WIDTH), BF16)],
        compiler_params=pltpu.CompilerParams(dimension_semantics=("arbitrary",),
                                             vmem_limit_bytes=VMEM_LIMIT),
        name="proj_sample",
    )(x_tb, g, w)


def _proj_kernel(x_ref, g_ref, w_ref, qs_ref, kn_ref, ck_ref,
                 q_ref, k_ref, v_ref, sg_ref, pooled_ref, sgp_ref, kt_ref, vt_ref, ulast_ref,
                 pc_ref, pn_ref, ok_ref, ue, *, first_kept_tile, n_new):
    i = pl.program_id(1)
    rows = x_ref.shape[1]

    @pl.when(i == 0)
    def _():
        ue[0:HIST_ROWS] = jnp.zeros((HIST_ROWS, POOL_WIDTH), F32)

    xn = _rms(x_ref[0], g_ref[...]).astype(BF16)

    def part(c):
        return jnp.dot(xn, w_ref[:, c * ATTN_WIDTH:(c + 1) * ATTN_WIDTH], preferred_element_type=F32)

    u = part(4)
    ue[HIST_ROWS:HIST_ROWS + rows] = u
    ga = part(3)
    pos = i * rows + lax.broadcasted_iota(jnp.int32, (rows, 1), 0)
    counts = [jnp.minimum(pos + 1, w).astype(F32) for w in POOL_WINDOWS]
    window_sum = lambda g, j: ue[pl.ds(HIST_ROWS - j, rows), _group_cols(g)]
    pooled_ref[0] = jnp.concatenate(_pooled(window_sum, counts), axis=-1).astype(BF16)
    last = u[rows - HIST_ROWS:rows]
    ue[0:HIST_ROWS] = last
    ulast_ref[0] = last
    gp = part(5)
    sg_ref[0] = _silu(ga).astype(BF16)
    q_ref[0] = part(0) * (ATTN_SCALE * LOG2E)
    sgp_ref[0] = _silu(gp).astype(BF16)
    k_ref[0] = part(1)
    v_ref[0] = part(2)

    _sample_scores(qs_ref, kn_ref, ck_ref, pc_ref, pn_ref, ok_ref, n_new)

    @pl.when(i >= first_kept_tile)
    def _():
        kt_ref[0] = k_ref[0].T
        vt_ref[0] = v_ref[0].T


def _project_prompt_score_sample(x, g, w_bf, q_s, knt, ck, n_new):
    b, s, _ = x.shape
    n_batch, h, dh, buf = ck.shape
    q_rows = q_s.shape[2]
    tiles = s // PROJ_ROWS
    head_groups = h // SAMPLE_HEADS
    assert b * tiles == n_batch * head_groups, "one sample head group per prompt tile"
    first_kept = (s - MAX_WINDOW) // PROJ_ROWS
    tok = lambda width: pl.BlockSpec((1, PROJ_ROWS, width), lambda bi, i: (bi, i, 0))
    feat = pl.BlockSpec((1, ATTN_WIDTH, PROJ_ROWS), lambda bi, i: (bi, 0, jnp.maximum(i - first_kept, 0)))
    heads = _hosted_heads(tiles, head_groups)
    f32 = lambda *shape: jax.ShapeDtypeStruct(shape, F32)
    bf16 = lambda *shape: jax.ShapeDtypeStruct(shape, BF16)
    return pl.pallas_call(
        functools.partial(_proj_kernel, first_kept_tile=first_kept, n_new=n_new),
        grid=(b, tiles),
        in_specs=[tok(D_MODEL),
                  pl.BlockSpec((1, D_MODEL), lambda bi, i: (0, 0)),
                  pl.BlockSpec((D_MODEL, IN_WIDTH), lambda bi, i: (0, 0), pipeline_mode=pl.Buffered(1)),
                  heads(q_rows, dh), heads(dh, LANES), heads(dh, buf)],
        out_specs=[tok(ATTN_WIDTH)] * 6 + [feat, feat,
                   pl.BlockSpec((1, HIST_ROWS, POOL_WIDTH), lambda bi, i: (bi, 0, 0)),
                   heads(q_rows, buf), heads(q_rows, LANES), heads(dh, buf)],
        out_shape=[f32(b, s, ATTN_WIDTH)] * 3
        + [bf16(b, s, ATTN_WIDTH), bf16(b, s, POOL_WIDTH), bf16(b, s, POOL_WIDTH)]
        + [f32(b, ATTN_WIDTH, MAX_WINDOW)] * 2
        + [f32(b, HIST_ROWS, POOL_WIDTH),
           f32(n_batch, h, q_rows, buf), f32(n_batch, h, q_rows, LANES), f32(n_batch, h, dh, buf)],
        scratch_shapes=[pltpu.VMEM((HIST_ROWS + PROJ_ROWS, POOL_WIDTH), F32)],
        compiler_params=pltpu.CompilerParams(dimension_semantics=("arbitrary", "arbitrary"),
                                             vmem_limit_bytes=VMEM_LIMIT),
        name="proj_prompt_score_sample",
    )(x, g, w_bf, q_s, knt, ck)


def _attn_kernel(q_ref, kp_ref, kc_ref, vp_ref, vc_ref, o_ref, acc, mx, sm, bias):
    first_super = (pl.program_id(1) == 0).astype(jnp.int32)

    qq = lax.broadcasted_iota(jnp.int32, (HEAD_PAIR * BLOCK, 2 * BLOCK), 0) % BLOCK
    kk = lax.broadcasted_iota(jnp.int32, (HEAD_PAIR * BLOCK, 2 * BLOCK), 1)
    band = (kk - qq >= 0) & (kk - qq <= BLOCK)
    bias[0] = jnp.where(band, 0.0, NEG)
    bias[1] = jnp.where(band & (kk >= BLOCK), 0.0, NEG)

    lane = lax.broadcasted_iota(jnp.int32, (BLOCK, LANES), 1)
    low = lane < HEAD_DIM
    nt = (((1,), (1,)), ((), ()))

    def keys(prev_ref, cur_ref, qs, d):
        span = d * BLOCK
        if qs >= span:
            return cur_ref[pl.ds(qs - span, 2 * BLOCK, stride=d), :]
        return jnp.concatenate([prev_ref[pl.ds(SUPER + qs - span, BLOCK, stride=d), :],
                                cur_ref[pl.ds(qs, BLOCK, stride=d), :]], axis=0)

    for pat, (_, d) in enumerate(DILATED_PATTERNS):
        for r in range(d):
            for j in range(SUPER // (d * BLOCK)):
                qs = r + d * BLOCK * j
                rows = pl.ds(qs, BLOCK, stride=d)
                qb = q_ref[rows, :]
                q2 = jnp.concatenate([jnp.where(low, qb, 0.0), jnp.where(low, 0.0, qb)], axis=0).astype(BF16)
                kb = keys(kp_ref, kc_ref, qs, d).astype(BF16)
                vb = keys(vp_ref, vc_ref, qs, d).astype(BF16)
                s = lax.dot_general(q2, kb, nt, preferred_element_type=F32)
                s = s + (bias[first_super] if j == 0 else bias[0])
                m = jnp.max(s, axis=1, keepdims=True)
                p = jnp.exp2(s - m)
                l = jnp.sum(p, axis=1, keepdims=True)
                o2 = jnp.dot(p.astype(BF16), vb, preferred_element_type=F32)
                acc[pat, rows, :] = jnp.where(low, o2[:BLOCK], o2[BLOCK:])
                mx[pat, rows, :] = jnp.where(low, m[:BLOCK], m[BLOCK:])
                sm[pat, rows, :] = jnp.where(low, l[:BLOCK], l[BLOCK:])

    m_all = jnp.maximum(jnp.maximum(mx[0], mx[1]), mx[2])
    num = jnp.zeros((SUPER, LANES), F32)
    den = jnp.zeros((SUPER, LANES), F32)
    for pat in range(len(DILATED_PATTERNS)):
        w = jnp.exp2(mx[pat] - m_all)
        num = num + w * acc[pat]
        den = den + w * sm[pat]
    o_ref[...] = (num / den).astype(BF16)


def _attend_prompt(q, k, v):
    b, s, _ = q.shape
    n_super = s // SUPER
    cur = pl.BlockSpec((None, SUPER, LANES), lambda bi, sb, hp: (bi, sb, hp))
    prev = pl.BlockSpec((None, SUPER, LANES), lambda bi, sb, hp: (bi, jnp.maximum(sb - 1, 0), hp))
    n_pat = len(DILATED_PATTERNS)
    return pl.pallas_call(
        _attn_kernel,
        grid=(b, n_super, N_HEADS // HEAD_PAIR),
        in_specs=[cur, prev, cur, prev, cur],
        out_specs=cur,
        out_shape=jax.ShapeDtypeStruct((b, s, ATTN_WIDTH), BF16),
        scratch_shapes=[pltpu.VMEM((n_pat, SUPER, LANES), F32),
                        pltpu.VMEM((n_pat, SUPER, LANES), F32),
                        pltpu.VMEM((n_pat, SUPER, LANES), F32),
                        pltpu.VMEM((2, HEAD_PAIR * BLOCK, 2 * BLOCK), F32)],
        compiler_params=pltpu.CompilerParams(dimension_semantics=("arbitrary",) * 3,
                                             vmem_limit_bytes=VMEM_LIMIT),
        name="attn_prompt",
    )(q, k, k, v, v)


def _merge_kernel(x_ref, o_ref, sg_ref, pooled_ref, sgp_ref, wp_ref, ps_ref, wo_ref, fg_ref,
                  pc_ref, pn_ref, vn_ref, cv_ref, y_ref, os_ref, ov_ref, *, n_new):
    pool = _pool_project(lambda g: pooled_ref[0, :, _group_cols(g)], wp_ref, ps_ref[...])
    mix = jnp.concatenate([o_ref[0] * sg_ref[0],
                           (pool * sgp_ref[0].astype(F32)).astype(BF16)], axis=-1)
    h = x_ref[0] + jnp.dot(mix, wo_ref[...], preferred_element_type=F32)
    y_ref[0] = _rms(h, fg_ref[...])
    _sample_values(pc_ref, pn_ref, vn_ref, cv_ref, os_ref, ov_ref, n_new)


def _merge_prompt_value_sample(x, o, sg, pooled, sgp, wp_bf, pool_scale, wo_bf, final_g, pc, pn, vnt, cv, n_new):
    b, s, _ = x.shape
    n_batch, h, dh, buf = cv.shape
    q_rows = pc.shape[2]
    tiles = s // MERGE_ROWS
    head_groups = h // SAMPLE_HEADS
    assert b * tiles == n_batch * head_groups, "one sample head group per prompt tile"
    tok = lambda width: pl.BlockSpec((1, MERGE_ROWS, width), lambda bi, i: (bi, i, 0))
    const = lambda shape: pl.BlockSpec(shape, lambda bi, i: (0,) * len(shape))
    heads = _hosted_heads(tiles, head_groups)
    return pl.pallas_call(
        functools.partial(_merge_kernel, n_new=n_new),
        grid=(b, tiles),
        in_specs=[tok(D_MODEL), tok(ATTN_WIDTH), tok(ATTN_WIDTH), tok(POOL_WIDTH), tok(POOL_WIDTH),
                  const((len(POOL_WINDOWS), POOL_GROUP_WIDTH, POOL_GROUP_WIDTH)),
                  const((1, POOL_WIDTH)),
                  pl.BlockSpec((D_MODEL, D_MODEL), lambda bi, i: (0, 0), pipeline_mode=pl.Buffered(1)),
                  const((1, D_MODEL)),
                  heads(q_rows, buf), heads(q_rows, LANES), heads(dh, LANES), heads(dh, buf)],
        out_specs=[tok(D_MODEL), heads(q_rows, dh), heads(dh, buf)],
        out_shape=[jax.ShapeDtypeStruct((b, s, D_MODEL), F32),
                   jax.ShapeDtypeStruct((n_batch, h, q_rows, dh), F32),
                   jax.ShapeDtypeStruct((n_batch, h, dh, buf), F32)],
        compiler_params=pltpu.CompilerParams(dimension_semantics=("arbitrary", "arbitrary"),
                                             vmem_limit_bytes=VMEM_LIMIT),
        name="merge_prompt_value_sample",
    )(x, o, sg, pooled, sgp, wp_bf, pool_scale, wo_bf, final_g, pc, pn, vnt, cv)


def _merge_sample_kernel(x_ref, o_ref, z_ref, hist_ref, wp_ref, ps_ref, wo_ref, fg_ref, y_ref, pool_ref, ue,
                         *, n_new, n_batch):
    u_col = 4 * ATTN_WIDTH
    ue[0:POOL_HIST] = hist_ref[...]
    ue[POOL_HIST:POOL_HIST + n_new] = z_ref[:, u_col:u_col + POOL_WIDTH].reshape(n_new, n_batch, POOL_WIDTH)
    pool_ref[...] = ue[n_new:n_new + POOL_HIST]

    rows = n_new * n_batch

    def window_sum(g, j):
        return ue[pl.ds(POOL_HIST - j, n_new), :, _group_cols(g)].reshape(rows, POOL_GROUP_WIDTH)

    pos = PAST_LEN + lax.broadcasted_iota(jnp.int32, (rows, 1), 0) // n_batch
    counts = [jnp.minimum(pos + 1, w).astype(F32) for w in POOL_WINDOWS]
    pooled = _pooled(window_sum, counts)
    pool = _pool_project(lambda g: pooled[g].astype(BF16), wp_ref, ps_ref[...])
    ga = z_ref[:, 3 * ATTN_WIDTH:4 * ATTN_WIDTH]
    gp = z_ref[:, u_col + POOL_WIDTH:u_col + 2 * POOL_WIDTH]
    mix = jnp.concatenate([o_ref[...] * _silu(ga), pool * _silu(gp)], axis=-1).astype(BF16)
    h = x_ref[...] + jnp.dot(mix, wo_ref[...], preferred_element_type=F32)
    y_ref[...] = _rms(h, fg_ref[...])


def _merge_sample(x_tb, o_tb, z_tb, hist, wp_bf, pool_scale, wo_bf, final_g, n_new, n_batch):
    return pl.pallas_call(
        functools.partial(_merge_sample_kernel, n_new=n_new, n_batch=n_batch),
        out_shape=[jax.ShapeDtypeStruct(x_tb.shape, F32),
                   jax.ShapeDtypeStruct((POOL_HIST, n_batch, POOL_WIDTH), F32)],
        scratch_shapes=[pltpu.VMEM((POOL_HIST + n_new, n_batch, POOL_WIDTH), F32)],
        compiler_params=pltpu.CompilerParams(vmem_limit_bytes=VMEM_LIMIT),
        name="merge_sample",
    )(x_tb, o_tb, z_tb, hist, wp_bf, pool_scale, wo_bf, final_g)


def _to_feature_major(z_cols, n_new, n_batch):
    t = z_cols.reshape(n_new, n_batch, N_HEADS, HEAD_DIM).transpose(1, 2, 3, 0)
    return jnp.pad(t, ((0, 0), (0, 0), (0, 0), (0, LANES - n_new)))


def kernel(x_prompt, x_sample, cache_k, cache_v, state_pool, norm_g, w_in, w_pool, pool_scale, w_out, final_norm_g):
    depth = norm_g.shape[0]
    assert depth == 1, "single-layer step"
    b, s, _ = x_prompt.shape
    n_batch, n_new, _ = x_sample.shape
    buf = cache_k.shape[2]
    assert s % SUPER == 0 and s >= MAX_WINDOW and buf == MAX_WINDOW and n_new <= POOL_HIST

    g = norm_g[0][None, :]
    fg = final_norm_g[None, :]
    wo_bf = w_out[0].astype(BF16)
    wp_bf = w_pool[0].astype(BF16)
    ps = pool_scale[0][None, :]

    x_tb = x_sample.transpose(1, 0, 2).reshape(n_new * n_batch, D_MODEL)
    z_tb, w_bf = _project_sample(x_tb, g, w_in[0])
    q_s = z_tb[:, 0:ATTN_WIDTH].reshape(n_new, n_batch, N_HEADS, HEAD_DIM).transpose(1, 2, 0, 3)
    q_s = jnp.pad(q_s, ((0, 0), (0, 0), (0, SUBLANES - n_new), (0, 0)))
    knt = _to_feature_major(z_tb[:, ATTN_WIDTH:2 * ATTN_WIDTH], n_new, n_batch)
    vnt = _to_feature_major(z_tb[:, 2 * ATTN_WIDTH:3 * ATTN_WIDTH], n_new, n_batch)
    ck = cache_k[0].transpose(0, 2, 3, 1)
    cv = cache_v[0].transpose(0, 2, 3, 1)

    q, k, v, sg, pooled, sgp, kt, vt, ulast, pc, pn, ok = _project_prompt_score_sample(
        x_prompt, g, w_bf, q_s, knt, ck, n_new)
    o = _attend_prompt(q, k, v)
    y_prompt, o_s, ov = _merge_prompt_value_sample(x_prompt, o, sg, pooled, sgp, wp_bf, ps, wo_bf, fg,
                                                   pc, pn, vnt, cv, n_new)
    to_window = lambda t: t.reshape(b, N_HEADS, HEAD_DIM, MAX_WINDOW).transpose(0, 3, 1, 2)[None]
    k_prompt, v_prompt = to_window(kt), to_window(vt)
    pool_prompt = ulast[:, HIST_ROWS - POOL_HIST:][None]

    o_tb = o_s[:, :, :n_new].transpose(2, 0, 1, 3).reshape(n_new * n_batch, ATTN_WIDTH)
    hist = state_pool[0].transpose(1, 0, 2)
    y_tb, pool_tb = _merge_sample(x_tb, o_tb, z_tb, hist, wp_bf, ps, wo_bf, fg, n_new, n_batch)
    y_sample = y_tb.reshape(n_new, n_batch, D_MODEL).transpose(1, 0, 2)
    k_sample = ok.transpose(0, 3, 1, 2)[None]
    v_sample = ov.transpose(0, 3, 1, 2)[None]
    pool_sample = pool_tb.transpose(1, 0, 2)[None]

    return (y_prompt, y_sample, k_prompt, v_prompt, pool_prompt, k_sample, v_sample, pool_sample)
```

```python
import functools

import jax
import jax.numpy as jnp
from jax import lax
from jax.experimental import pallas as pl
from jax.experimental.pallas import tpu as pltpu

F32 = jnp.float32
BF16 = jnp.bfloat16

D_MODEL = 2048
ATTN_WIDTH = 1024
HEAD_DIM = 64
N_HEADS = 16
POOL_WIDTH = 1024
POOL_WINDOWS = (2, 4, 8, 16)
POOL_GROUP_WIDTH = 256
POOL_HIST = 15
IN_WIDTH = 6144
DILATED_PATTERNS = ((128, 1), (512, 4), (2048, 16))
MAX_WINDOW = 2048
BLOCK = 128
PAST_LEN = 16384
EPS = 1e-6
ATTN_SCALE = HEAD_DIM ** -0.5
NEG = -1e30
LOG2E = 1.4426950408889634

LANES = 128
SUBLANES = 8
HEAD_PAIR = LANES // HEAD_DIM
HIST_ROWS = 16

PROJ_ROWS = 256
MERGE_ROWS = 256
SUPER = 2048
CAST_COLS = 1024
VMEM_LIMIT = 58 * 1024 * 1024


def _silu(x):
    return x / (1.0 + jnp.exp(-x))


def _rms(x, g):
    ms = jnp.mean(x * x, axis=-1, keepdims=True)
    return x * lax.rsqrt(ms + EPS) * g


def _group_cols(g):
    return slice(g * POOL_GROUP_WIDTH, (g + 1) * POOL_GROUP_WIDTH)


def _pooled(window_sum, counts):
    outs = []
    for g, w in enumerate(POOL_WINDOWS):
        cur = window_sum(g, 0)
        tot = cur
        for j in range(1, w):
            tot = tot + window_sum(g, j)
        outs.append(tot / counts[g] - cur)
    return outs


def _pool_project(pooled_group, wp_ref, scale):
    outs = [jnp.dot(pooled_group(g), wp_ref[g], preferred_element_type=F32) for g in range(len(POOL_WINDOWS))]
    return jnp.concatenate(outs, axis=-1) * scale


def _multiplicity(delta):
    c = jnp.zeros(delta.shape, F32)
    for window, d in DILATED_PATTERNS:
        assert d & (d - 1) == 0, "power-of-two dilation: residue test is a bit mask"
        c = c + jnp.where(((delta & (d - 1)) == 0) & (delta <= window), 1.0, 0.0)
    return c


def _shift_cache(src, new, dst_ref, hh, n_new):
    buf = src.shape[-1]
    tail_lane = lax.broadcasted_iota(jnp.int32, (HEAD_DIM, LANES), 1)
    rolled = pltpu.roll(src, buf - n_new, 1)
    tail = rolled[:, buf - LANES:]
    placed = pltpu.roll(new, LANES - n_new, 1)
    dst_ref[0, hh, :, 0:buf - LANES] = rolled[:, 0:buf - LANES]
    dst_ref[0, hh, :, buf - LANES:buf] = jnp.where(tail_lane >= LANES - n_new, placed, tail)


def _sample_scores(q_ref, kn_ref, ck_ref, pc_ref, pn_ref, ok_ref, n_new):
    buf = ck_ref.shape[-1]
    q_rows = q_ref.shape[2]
    qi = lax.broadcasted_iota(jnp.int32, (q_rows, buf), 0)
    cc = _multiplicity(buf + qi - lax.broadcasted_iota(jnp.int32, (q_rows, buf), 1))
    qn = lax.broadcasted_iota(jnp.int32, (q_rows, LANES), 0)
    nl = lax.broadcasted_iota(jnp.int32, (q_rows, LANES), 1)
    cn = jnp.where(nl <= qn, _multiplicity(qn - jnp.minimum(nl, qn)), 0.0)

    for hh in range(ck_ref.shape[1]):
        kc = ck_ref[0, hh]
        knt = kn_ref[0, hh]
        qh = (q_ref[0, hh] * ATTN_SCALE).astype(BF16)
        sc = jnp.dot(qh, kc.astype(BF16), preferred_element_type=F32)
        sn = jnp.dot(qh, knt.astype(BF16), preferred_element_type=F32)
        sc = jnp.where(cc > 0, sc, NEG)
        sn = jnp.where(cn > 0, sn, NEG)
        m = jnp.maximum(jnp.max(sc, axis=1, keepdims=True), jnp.max(sn, axis=1, keepdims=True))
        pc = cc * jnp.exp(sc - m)
        pn = cn * jnp.exp(sn - m)
        den = jnp.sum(pc, axis=1, keepdims=True) + jnp.sum(pn, axis=1, keepdims=True)
        pc_ref[0, hh] = pc / den
        pn_ref[0, hh] = pn / den
        _shift_cache(kc, knt, ok_ref, hh, n_new)


def _sample_values(pc_ref, pn_ref, vn_ref, cv_ref, o_ref, ov_ref, n_new):
    nt = (((1,), (1,)), ((), ()))
    for hh in range(cv_ref.shape[1]):
        vc = cv_ref[0, hh]
        vnt = vn_ref[0, hh]
        o_ref[0, hh] = (
            lax.dot_general(pc_ref[0, hh].astype(BF16), vc.astype(BF16), nt, preferred_element_type=F32)
            + lax.dot_general(pn_ref[0, hh].astype(BF16), vnt.astype(BF16), nt, preferred_element_type=F32))
        _shift_cache(vc, vnt, ov_ref, hh, n_new)


def _hosted_heads(grid, head_groups, heads_per_step):
    def spec(*tail):
        def index(*ids):
            step = 0
            for size, idx in zip(grid, ids, strict=True):
                step = step * size + idx
            return (step // head_groups, step % head_groups, 0, 0)
        return pl.BlockSpec((1, heads_per_step) + tail, index)
    return spec


def _proj_sample_kernel(x_ref, g_ref, w_ref, z_ref, wbf_ref):
    xn = _rms(x_ref[...], g_ref[...]).astype(BF16)
    w = w_ref[...].astype(BF16)
    wbf_ref[...] = w
    z_ref[...] = jnp.dot(xn, w, preferred_element_type=F32)


def _project_sample(x_tb, g, w):
    rows = x_tb.shape[0]
    cols = pl.BlockSpec((D_MODEL, CAST_COLS), lambda c: (0, c))
    return pl.pallas_call(
        _proj_sample_kernel,
        grid=(IN_WIDTH // CAST_COLS,),
        in_specs=[pl.BlockSpec((rows, D_MODEL), lambda c: (0, 0)),
                  pl.BlockSpec((1, D_MODEL), lambda c: (0, 0)),
                  cols],
        out_specs=[pl.BlockSpec((rows, CAST_COLS), lambda c: (0, c)), cols],
        out_shape=[jax.ShapeDtypeStruct((rows, IN_WIDTH), F32),
                   jax.ShapeDtypeStruct((D_MODEL, IN_WIDTH), BF16)],
        compiler_params=pltpu.CompilerParams(dimension_semantics=("arbitrary",),
                                             vmem_limit_bytes=VMEM_LIMIT),
        name="proj_sample",
    )(x_tb, g, w)


def _proj_kernel(x_ref, g_ref, w_ref, qs_ref, kn_ref, ck_ref,
                 q_ref, k_ref, v_ref, sg_ref, u_ref, sgp_ref, kt_ref, vt_ref, pc_ref, pn_ref, ok_ref, *, n_new):
    _sample_scores(qs_ref, kn_ref, ck_ref, pc_ref, pn_ref, ok_ref, n_new)

    xn = _rms(x_ref[0], g_ref[...]).astype(BF16)

    def part(c):
        return jnp.dot(xn, w_ref[:, c * ATTN_WIDTH:(c + 1) * ATTN_WIDTH], preferred_element_type=F32)

    sg_ref[0] = _silu(part(3)).astype(BF16)
    sgp_ref[0] = _silu(part(5)).astype(BF16)
    u_ref[0] = part(4)
    q_ref[0] = part(0) * (ATTN_SCALE * LOG2E)
    k = part(1)
    k_ref[0] = k
    v = part(2)
    v_ref[0] = v
    kt_ref[0] = k.T
    vt_ref[0] = v.T


def _project_prompt_score_sample(x, g, w_bf, q_s, knt, ck, n_new):
    b, s, _ = x.shape
    n_batch, h, dh, buf = ck.shape
    q_rows = q_s.shape[2]
    tiles = s // PROJ_ROWS
    hosted = n_batch * h // (b * tiles)
    assert hosted * b * tiles == n_batch * h and h % hosted == 0, "sample heads split evenly over the grid"
    first_kept = (s - MAX_WINDOW) // PROJ_ROWS
    tok = lambda width: pl.BlockSpec((1, PROJ_ROWS, width), lambda bi, i: (bi, i, 0))
    feat = pl.BlockSpec((1, ATTN_WIDTH, PROJ_ROWS), lambda bi, i: (bi, 0, jnp.maximum(i - first_kept, 0)))
    heads = _hosted_heads((b, tiles), h // hosted, hosted)
    f32 = lambda *shape: jax.ShapeDtypeStruct(shape, F32)
    bf16 = lambda *shape: jax.ShapeDtypeStruct(shape, BF16)
    return pl.pallas_call(
        functools.partial(_proj_kernel, n_new=n_new),
        grid=(b, tiles),
        in_specs=[tok(D_MODEL),
                  pl.BlockSpec((1, D_MODEL), lambda bi, i: (0, 0)),
                  pl.BlockSpec((D_MODEL, IN_WIDTH), lambda bi, i: (0, 0), pipeline_mode=pl.Buffered(1)),
                  heads(q_rows, dh), heads(dh, LANES), heads(dh, buf)],
        out_specs=[tok(ATTN_WIDTH)] * 6 + [feat, feat,
                   heads(q_rows, buf), heads(q_rows, LANES), heads(dh, buf)],
        out_shape=[f32(b, s, ATTN_WIDTH)] * 3
        + [bf16(b, s, ATTN_WIDTH), f32(b, s, POOL_WIDTH), bf16(b, s, POOL_WIDTH)]
        + [f32(b, ATTN_WIDTH, MAX_WINDOW)] * 2
        + [f32(n_batch, h, q_rows, buf), f32(n_batch, h, q_rows, LANES), f32(n_batch, h, dh, buf)],
        compiler_params=pltpu.CompilerParams(dimension_semantics=("arbitrary", "arbitrary"),
                                             vmem_limit_bytes=VMEM_LIMIT),
        name="proj_prompt_score_sample",
    )(x, g, w_bf, q_s, knt, ck)


def _attn_kernel(q_ref, kp_ref, kc_ref, vp_ref, vc_ref, sg_ref, o_ref, acc, mx, sm, bias):
    first_super = (pl.program_id(1) == 0).astype(jnp.int32)

    qq = lax.broadcasted_iota(jnp.int32, (HEAD_PAIR * BLOCK, 2 * BLOCK), 0) % BLOCK
    kk = lax.broadcasted_iota(jnp.int32, (HEAD_PAIR * BLOCK, 2 * BLOCK), 1)
    band = (kk - qq >= 0) & (kk - qq <= BLOCK)
    bias[0] = jnp.where(band, 0.0, NEG)
    bias[1] = jnp.where(band & (kk >= BLOCK), 0.0, NEG)

    lane = lax.broadcasted_iota(jnp.int32, (BLOCK, LANES), 1)
    low = lane < HEAD_DIM
    nt = (((1,), (1,)), ((), ()))

    def keys(prev_ref, cur_ref, qs, d):
        span = d * BLOCK
        if qs >= span:
            return cur_ref[pl.ds(qs - span, 2 * BLOCK, stride=d), :]
        return jnp.concatenate([prev_ref[pl.ds(SUPER + qs - span, BLOCK, stride=d), :],
                                cur_ref[pl.ds(qs, BLOCK, stride=d), :]], axis=0)

    for pat, (_, d) in enumerate(DILATED_PATTERNS):
        for r in range(d):
            for j in range(SUPER // (d * BLOCK)):
                qs = r + d * BLOCK * j
                rows = pl.ds(qs, BLOCK, stride=d)
                qb = q_ref[rows, :]
                q2 = jnp.concatenate([jnp.where(low, qb, 0.0), jnp.where(low, 0.0, qb)], axis=0).astype(BF16)
                kb = keys(kp_ref, kc_ref, qs, d).astype(BF16)
                vb = keys(vp_ref, vc_ref, qs, d).astype(BF16)
                s = lax.dot_general(q2, kb, nt, preferred_element_type=F32)
                s = s + (bias[first_super] if j == 0 else bias[0])
                m = jnp.max(s, axis=1, keepdims=True)
                p = jnp.exp2(s - m)
                l = jnp.sum(p, axis=1, keepdims=True)
                o2 = jnp.dot(p.astype(BF16), vb, preferred_element_type=F32)
                acc[pat, rows, :] = jnp.where(low, o2[:BLOCK], o2[BLOCK:])
                mx[pat, rows, :] = jnp.where(low, m[:BLOCK], m[BLOCK:])
                sm[pat, rows, :] = jnp.where(low, l[:BLOCK], l[BLOCK:])

    m_all = jnp.maximum(jnp.maximum(mx[0], mx[1]), mx[2])
    num = jnp.zeros((SUPER, LANES), F32)
    den = jnp.zeros((SUPER, LANES), F32)
    for pat in range(len(DILATED_PATTERNS)):
        w = jnp.exp2(mx[pat] - m_all)
        num = num + w * acc[pat]
        den = den + w * sm[pat]
    o_ref[...] = (num / den).astype(BF16) * sg_ref[...]


def _attend_prompt(q, k, v, sg):
    b, s, _ = q.shape
    n_super = s // SUPER
    cur = pl.BlockSpec((None, SUPER, LANES), lambda bi, sb, hp: (bi, sb, hp))
    prev = pl.BlockSpec((None, SUPER, LANES), lambda bi, sb, hp: (bi, jnp.maximum(sb - 1, 0), hp))
    n_pat = len(DILATED_PATTERNS)
    return pl.pallas_call(
        _attn_kernel,
        grid=(b, n_super, N_HEADS // HEAD_PAIR),
        in_specs=[cur, prev, cur, prev, cur, cur],
        out_specs=cur,
        out_shape=jax.ShapeDtypeStruct((b, s, ATTN_WIDTH), BF16),
        scratch_shapes=[pltpu.VMEM((n_pat, SUPER, LANES), F32),
                        pltpu.VMEM((n_pat, SUPER, LANES), F32),
                        pltpu.VMEM((n_pat, SUPER, LANES), F32),
                        pltpu.VMEM((2, HEAD_PAIR * BLOCK, 2 * BLOCK), F32)],
        compiler_params=pltpu.CompilerParams(dimension_semantics=("arbitrary",) * 3,
                                             vmem_limit_bytes=VMEM_LIMIT),
        name="attn_prompt",
    )(q, k, k, v, v, sg)


def _merge_kernel(x_ref, a_ref, u_ref, uh_ref, sgp_ref, wp_ref, ps_ref, wo_ref, fg_ref,
                  pc_ref, pn_ref, vn_ref, cv_ref, y_ref, os_ref, ov_ref, ue, *, n_new):
    i = pl.program_id(1)
    rows = u_ref.shape[1]
    ue[0:HIST_ROWS] = jnp.where(i == 0, 0.0, uh_ref[0])
    ue[HIST_ROWS:HIST_ROWS + rows] = u_ref[0]
    pos = i * rows + lax.broadcasted_iota(jnp.int32, (rows, 1), 0)
    counts = [jnp.minimum(pos + 1, w).astype(F32) for w in POOL_WINDOWS]
    pooled = _pooled(lambda g, j: ue[pl.ds(HIST_ROWS - j, rows), _group_cols(g)], counts)
    pool = _pool_project(lambda g: pooled[g].astype(BF16), wp_ref, ps_ref[...])
    mix = jnp.concatenate([a_ref[0], (pool * sgp_ref[0].astype(F32)).astype(BF16)], axis=-1)
    h = x_ref[0] + jnp.dot(mix, wo_ref[...], preferred_element_type=F32)
    y_ref[0] = _rms(h, fg_ref[...])
    _sample_values(pc_ref, pn_ref, vn_ref, cv_ref, os_ref, ov_ref, n_new)


def _merge_prompt_value_sample(x, mix_a, u, sgp, wp_bf, pool_scale, wo_bf, final_g, pc, pn, vnt, cv, n_new):
    b, s, _ = x.shape
    n_batch, h, dh, buf = cv.shape
    q_rows = pc.shape[2]
    tiles = s // MERGE_ROWS
    hosted = n_batch * h // (b * tiles)
    assert hosted * b * tiles == n_batch * h and h % hosted == 0, "sample heads split evenly over the grid"
    per_tile = MERGE_ROWS // HIST_ROWS
    tok = lambda width: pl.BlockSpec((1, MERGE_ROWS, width), lambda bi, i: (bi, i, 0))
    const = lambda shape: pl.BlockSpec(shape, lambda bi, i: (0,) * len(shape))
    heads = _hosted_heads((b, tiles), h // hosted, hosted)
    return pl.pallas_call(
        functools.partial(_merge_kernel, n_new=n_new),
        grid=(b, tiles),
        in_specs=[tok(D_MODEL), tok(ATTN_WIDTH), tok(POOL_WIDTH),
                  pl.BlockSpec((1, HIST_ROWS, POOL_WIDTH),
                               lambda bi, i: (bi, jnp.maximum(i * per_tile - 1, 0), 0)),
                  tok(POOL_WIDTH),
                  const((len(POOL_WINDOWS), POOL_GROUP_WIDTH, POOL_GROUP_WIDTH)),
                  const((1, POOL_WIDTH)),
                  pl.BlockSpec((D_MODEL, D_MODEL), lambda bi, i: (0, 0), pipeline_mode=pl.Buffered(1)),
                  const((1, D_MODEL)),
                  heads(q_rows, buf), heads(q_rows, LANES), heads(dh, LANES), heads(dh, buf)],
        out_specs=[tok(D_MODEL), heads(q_rows, dh), heads(dh, buf)],
        out_shape=[jax.ShapeDtypeStruct((b, s, D_MODEL), F32),
                   jax.ShapeDtypeStruct((n_batch, h, q_rows, dh), F32),
                   jax.ShapeDtypeStruct((n_batch, h, dh, buf), F32)],
        scratch_shapes=[pltpu.VMEM((HIST_ROWS + MERGE_ROWS, POOL_WIDTH), F32)],
        compiler_params=pltpu.CompilerParams(dimension_semantics=("arbitrary", "arbitrary"),
                                             vmem_limit_bytes=VMEM_LIMIT),
        name="merge_prompt_value_sample",
    )(x, mix_a, u, u, sgp, wp_bf, pool_scale, wo_bf, final_g, pc, pn, vnt, cv)


def _merge_sample_kernel(x_ref, o_ref, z_ref, hist_ref, wp_ref, ps_ref, wo_ref, fg_ref, y_ref, pool_ref, ue,
                         *, n_new, n_batch):
    u_col = 4 * ATTN_WIDTH
    ue[0:POOL_HIST] = hist_ref[...]
    ue[POOL_HIST:POOL_HIST + n_new] = z_ref[:, u_col:u_col + POOL_WIDTH].reshape(n_new, n_batch, POOL_WIDTH)
    pool_ref[...] = ue[n_new:n_new + POOL_HIST]

    rows = n_new * n_batch

    def window_sum(g, j):
        return ue[pl.ds(POOL_HIST - j, n_new), :, _group_cols(g)].reshape(rows, POOL_GROUP_WIDTH)

    pos = PAST_LEN + lax.broadcasted_iota(jnp.int32, (rows, 1), 0) // n_batch
    counts = [jnp.minimum(pos + 1, w).astype(F32) for w in POOL_WINDOWS]
    pooled = _pooled(window_sum, counts)
    pool = _pool_project(lambda g: pooled[g].astype(BF16), wp_ref, ps_ref[...])
    ga = z_ref[:, 3 * ATTN_WIDTH:4 * ATTN_WIDTH]
    gp = z_ref[:, u_col + POOL_WIDTH:u_col + 2 * POOL_WIDTH]
    mix = jnp.concatenate([o_ref[...] * _silu(ga), pool * _silu(gp)], axis=-1).astype(BF16)
    h = x_ref[...] + jnp.dot(mix, wo_ref[...], preferred_element_type=F32)
    y_ref[...] = _rms(h, fg_ref[...])


def _merge_sample(x_tb, o_tb, z_tb, hist, wp_bf, pool_scale, wo_bf, final_g, n_new, n_batch):
    return pl.pallas_call(
        functools.partial(_merge_sample_kernel, n_new=n_new, n_batch=n_batch),
        out_shape=[jax.ShapeDtypeStruct(x_tb.shape, F32),
                   jax.ShapeDtypeStruct((POOL_HIST, n_batch, POOL_WIDTH), F32)],
        scratch_shapes=[pltpu.VMEM((POOL_HIST + n_new, n_batch, POOL_WIDTH), F32)],
        compiler_params=pltpu.CompilerParams(vmem_limit_bytes=VMEM_LIMIT),
        name="merge_sample",
    )(x_tb, o_tb, z_tb, hist, wp_bf, pool_scale, wo_bf, final_g)


def _to_feature_major(z_cols, n_new, n_batch):
    t = z_cols.reshape(n_new, n_batch, N_HEADS, HEAD_DIM).transpose(1, 2, 3, 0)
    return jnp.pad(t, ((0, 0), (0, 0), (0, 0), (0, LANES - n_new)))


def kernel(x_prompt, x_sample, cache_k, cache_v, state_pool, norm_g, w_in, w_pool, pool_scale, w_out, final_norm_g):
    depth = norm_g.shape[0]
    assert depth == 1, "single-layer step"
    b, s, _ = x_prompt.shape
    n_batch, n_new, _ = x_sample.shape
    buf = cache_k.shape[2]
    assert s % SUPER == 0 and s >= MAX_WINDOW and buf == MAX_WINDOW and n_new <= POOL_HIST

    g = norm_g[0][None, :]
    fg = final_norm_g[None, :]
    wo_bf = w_out[0].astype(BF16)
    wp_bf = w_pool[0].astype(BF16)
    ps = pool_scale[0][None, :]

    x_tb = x_sample.transpose(1, 0, 2).reshape(n_new * n_batch, D_MODEL)
    z_tb, w_bf = _project_sample(x_tb, g, w_in[0])
    q_s = z_tb[:, 0:ATTN_WIDTH].reshape(n_new, n_batch, N_HEADS, HEAD_DIM).transpose(1, 2, 0, 3)
    q_s = jnp.pad(q_s, ((0, 0), (0, 0), (0, SUBLANES - n_new), (0, 0)))
    knt = _to_feature_major(z_tb[:, ATTN_WIDTH:2 * ATTN_WIDTH], n_new, n_batch)
    vnt = _to_feature_major(z_tb[:, 2 * ATTN_WIDTH:3 * ATTN_WIDTH], n_new, n_batch)
    ck = cache_k[0].transpose(0, 2, 3, 1)
    cv = cache_v[0].transpose(0, 2, 3, 1)

    q, k, v, sg, u, sgp, kt, vt, pc, pn, ok = _project_prompt_score_sample(x_prompt, g, w_bf, q_s, knt, ck, n_new)
    mix_a = _attend_prompt(q, k, v, sg)
    y_prompt, o_s, ov = _merge_prompt_value_sample(x_prompt, mix_a, u, sgp, wp_bf, ps, wo_bf, fg,
                                                   pc, pn, vnt, cv, n_new)
    to_window = lambda t: t.reshape(b, N_HEADS, HEAD_DIM, MAX_WINDOW).transpose(0, 3, 1, 2)[None]
    k_prompt, v_prompt = to_window(kt), to_window(vt)
    pool_prompt = u[:, s - POOL_HIST:][None]

    o_tb = o_s[:, :, :n_new].transpose(2, 0, 1, 3).reshape(n_new * n_batch, ATTN_WIDTH)
    hist = state_pool[0].transpose(1, 0, 2)
    y_tb, pool_tb = _merge_sample(x_tb, o_tb, z_tb, hist, wp_bf, ps, wo_bf, fg, n_new, n_batch)
    y_sample = y_tb.reshape(n_new, n_batch, D_MODEL).transpose(1, 0, 2)
    k_sample = ok.transpose(0, 3, 1, 2)[None]
    v_sample = ov.transpose(0, 3, 1, 2)[None]
    pool_sample = pool_tb.transpose(1, 0, 2)[None]

    return (y_prompt, y_sample, k_prompt, v_prompt, pool_prompt, k_sample, v_sample, pool_sample)
```

```python
import functools

import jax
import jax.numpy as jnp
from jax import lax
from jax.experimental import pallas as pl
from jax.experimental.pallas import tpu as pltpu

F32 = jnp.float32
BF16 = jnp.bfloat16

D_MODEL = 2048
ATTN_WIDTH = 1024
HEAD_DIM = 64
N_HEADS = 16
POOL_WIDTH = 1024
POOL_WINDOWS = (2, 4, 8, 16)
POOL_GROUP_WIDTH = 256
POOL_HIST = 15
IN_WIDTH = 6144
DILATED_PATTERNS = ((128, 1), (512, 4), (2048, 16))
MAX_WINDOW = 2048
BLOCK = 128
PAST_LEN = 16384
EPS = 1e-6
ATTN_SCALE = HEAD_DIM ** -0.5
NEG = -1e30
LOG2E = 1.4426950408889634

LANES = 128
SUBLANES = 8
HEAD_PAIR = LANES // HEAD_DIM
HIST_ROWS = 16

PROJ_ROWS = 256
MERGE_ROWS = 256
SUPER = 2048
CAST_COLS = 1024
VMEM_LIMIT = 58 * 1024 * 1024


def _silu(x):
    return x / (1.0 + jnp.exp(-x))


def _rms(x, g):
    ms = jnp.mean(x * x, axis=-1, keepdims=True)
    return x * lax.rsqrt(ms + EPS) * g


def _group_cols(g):
    return slice(g * POOL_GROUP_WIDTH, (g + 1) * POOL_GROUP_WIDTH)


def _pooled(window_sum, counts):
    outs = []
    for g, w in enumerate(POOL_WINDOWS):
        cur = window_sum(g, 0)
        tot = cur
        for j in range(1, w):
            tot = tot + window_sum(g, j)
        outs.append(tot / counts[g] - cur)
    return outs


def _pool_project(pooled_group, wp_ref, scale):
    outs = [jnp.dot(pooled_group(g), wp_ref[g], preferred_element_type=F32) for g in range(len(POOL_WINDOWS))]
    return jnp.concatenate(outs, axis=-1) * scale


def _multiplicity(delta):
    c = jnp.zeros(delta.shape, F32)
    for window, d in DILATED_PATTERNS:
        assert d & (d - 1) == 0, "power-of-two dilation: residue test is a bit mask"
        c = c + jnp.where(((delta & (d - 1)) == 0) & (delta <= window), 1.0, 0.0)
    return c


def _shift_cache(src, new, dst_ref, hh, n_new):
    buf = src.shape[-1]
    tail_lane = lax.broadcasted_iota(jnp.int32, (HEAD_DIM, LANES), 1)
    rolled = pltpu.roll(src, buf - n_new, 1)
    tail = rolled[:, buf - LANES:]
    placed = pltpu.roll(new, LANES - n_new, 1)
    dst_ref[0, hh, :, 0:buf - LANES] = rolled[:, 0:buf - LANES]
    dst_ref[0, hh, :, buf - LANES:buf] = jnp.where(tail_lane >= LANES - n_new, placed, tail)


def _sample_scores(q_ref, kn_ref, ck_ref, pc_ref, pn_ref, ok_ref, n_new):
    buf = ck_ref.shape[-1]
    q_rows = q_ref.shape[2]
    qi = lax.broadcasted_iota(jnp.int32, (q_rows, buf), 0)
    cc = _multiplicity(buf + qi - lax.broadcasted_iota(jnp.int32, (q_rows, buf), 1))
    qn = lax.broadcasted_iota(jnp.int32, (q_rows, LANES), 0)
    nl = lax.broadcasted_iota(jnp.int32, (q_rows, LANES), 1)
    cn = jnp.where(nl <= qn, _multiplicity(qn - jnp.minimum(nl, qn)), 0.0)

    for hh in range(ck_ref.shape[1]):
        kc = ck_ref[0, hh]
        knt = kn_ref[0, hh]
        qh = (q_ref[0, hh] * ATTN_SCALE).astype(BF16)
        sc = jnp.dot(qh, kc.astype(BF16), preferred_element_type=F32)
        sn = jnp.dot(qh, knt.astype(BF16), preferred_element_type=F32)
        sc = jnp.where(cc > 0, sc, NEG)
        sn = jnp.where(cn > 0, sn, NEG)
        m = jnp.maximum(jnp.max(sc, axis=1, keepdims=True), jnp.max(sn, axis=1, keepdims=True))
        pc = cc * jnp.exp(sc - m)
        pn = cn * jnp.exp(sn - m)
        den = jnp.sum(pc, axis=1, keepdims=True) + jnp.sum(pn, axis=1, keepdims=True)
        pc_ref[0, hh] = pc / den
        pn_ref[0, hh] = pn / den
        _shift_cache(kc, knt, ok_ref, hh, n_new)


def _sample_values(pc_ref, pn_ref, vn_ref, cv_ref, o_ref, ov_ref, n_new):
    nt = (((1,), (1,)), ((), ()))
    for hh in range(cv_ref.shape[1]):
        vc = cv_ref[0, hh]
        vnt = vn_ref[0, hh]
        o_ref[0, hh] = (
            lax.dot_general(pc_ref[0, hh].astype(BF16), vc.astype(BF16), nt, preferred_element_type=F32)
            + lax.dot_general(pn_ref[0, hh].astype(BF16), vnt.astype(BF16), nt, preferred_element_type=F32))
        _shift_cache(vc, vnt, ov_ref, hh, n_new)


def _hosted_heads(grid, head_groups, heads_per_step):
    def spec(*tail):
        def index(*ids):
            step = 0
            for size, idx in zip(grid, ids, strict=True):
                step = step * size + idx
            return (step // head_groups, step % head_groups, 0, 0)
        return pl.BlockSpec((1, heads_per_step) + tail, index)
    return spec


def _proj_sample_kernel(x_ref, g_ref, w_ref, z_ref, wbf_ref):
    xn = _rms(x_ref[...], g_ref[...]).astype(BF16)
    w = w_ref[...].astype(BF16)
    wbf_ref[...] = w
    z_ref[...] = jnp.dot(xn, w, preferred_element_type=F32)


def _project_sample(x_tb, g, w):
    rows = x_tb.shape[0]
    cols = pl.BlockSpec((D_MODEL, CAST_COLS), lambda c: (0, c))
    return pl.pallas_call(
        _proj_sample_kernel,
        grid=(IN_WIDTH // CAST_COLS,),
        in_specs=[pl.BlockSpec((rows, D_MODEL), lambda c: (0, 0)),
                  pl.BlockSpec((1, D_MODEL), lambda c: (0, 0)),
                  cols],
        out_specs=[pl.BlockSpec((rows, CAST_COLS), lambda c: (0, c)), cols],
        out_shape=[jax.ShapeDtypeStruct((rows, IN_WIDTH), F32),
                   jax.ShapeDtypeStruct((D_MODEL, IN_WIDTH), BF16)],
        compiler_params=pltpu.CompilerParams(dimension_semantics=("arbitrary",),
                                             vmem_limit_bytes=VMEM_LIMIT),
        name="proj_sample",
    )(x_tb, g, w)


def _proj_kernel(x_ref, g_ref, w_ref, qs_ref, kn_ref, ck_ref,
                 q_ref, k_ref, v_ref, sg_ref, u_ref, sgp_ref, kt_ref, vt_ref, pc_ref, pn_ref, ok_ref, *, n_new):
    _sample_scores(qs_ref, kn_ref, ck_ref, pc_ref, pn_ref, ok_ref, n_new)

    xn = _rms(x_ref[0], g_ref[...]).astype(BF16)

    def part(c):
        return jnp.dot(xn, w_ref[:, c * ATTN_WIDTH:(c + 1) * ATTN_WIDTH], preferred_element_type=F32)

    sg_ref[0] = _silu(part(3)).astype(BF16)
    sgp_ref[0] = _silu(part(5)).astype(BF16)
    u_ref[0] = part(4)
    q_ref[0] = part(0) * (ATTN_SCALE * LOG2E)
    k = part(1)
    k_ref[0] = k
    v = part(2)
    v_ref[0] = v
    kt_ref[0] = k.T
    vt_ref[0] = v.T


def _project_prompt_score_sample(x, g, w_bf, q_s, knt, ck, n_new):
    b, s, _ = x.shape
    n_batch, h, dh, buf = ck.shape
    q_rows = q_s.shape[2]
    tiles = s // PROJ_ROWS
    hosted = n_batch * h // (b * tiles)
    assert hosted * b * tiles == n_batch * h and h % hosted == 0, "sample heads split evenly over the grid"
    first_kept = (s - MAX_WINDOW) // PROJ_ROWS
    tok = lambda width: pl.BlockSpec((1, PROJ_ROWS, width), lambda bi, i: (bi, i, 0))
    feat = pl.BlockSpec((1, ATTN_WIDTH, PROJ_ROWS), lambda bi, i: (bi, 0, jnp.maximum(i - first_kept, 0)))
    heads = _hosted_heads((b, tiles), h // hosted, hosted)
    f32 = lambda *shape: jax.ShapeDtypeStruct(shape, F32)
    bf16 = lambda *shape: jax.ShapeDtypeStruct(shape, BF16)
    return pl.pallas_call(
        functools.partial(_proj_kernel, n_new=n_new),
        grid=(b, tiles),
        in_specs=[tok(D_MODEL),
                  pl.BlockSpec((1, D_MODEL), lambda bi, i: (0, 0)),
                  pl.BlockSpec((D_MODEL, IN_WIDTH), lambda bi, i: (0, 0), pipeline_mode=pl.Buffered(1)),
                  heads(q_rows, dh), heads(dh, LANES), heads(dh, buf)],
        out_specs=[tok(ATTN_WIDTH)] * 6 + [feat, feat,
                   heads(q_rows, buf), heads(q_rows, LANES), heads(dh, buf)],
        out_shape=[f32(b, s, ATTN_WIDTH)] * 3
        + [bf16(b, s, ATTN_WIDTH), f32(b, s, POOL_WIDTH), bf16(b, s, POOL_WIDTH)]
        + [f32(b, ATTN_WIDTH, MAX_WINDOW)] * 2
        + [f32(n_batch, h, q_rows, buf), f32(n_batch, h, q_rows, LANES), f32(n_batch, h, dh, buf)],
        compiler_params=pltpu.CompilerParams(dimension_semantics=("arbitrary", "arbitrary"),
                                             vmem_limit_bytes=VMEM_LIMIT),
        name="proj_prompt_score_sample",
    )(x, g, w_bf, q_s, knt, ck)


def _attn_kernel(q_ref, kp_ref, kc_ref, vp_ref, vc_ref, sg_ref, o_ref, acc, mx, sm, qsel, kmask):
    first_super = (pl.program_id(1) == 0).astype(jnp.int32)

    kk = lax.broadcasted_iota(jnp.int32, (2 * BLOCK, BLOCK), 0)
    qq = lax.broadcasted_iota(jnp.int32, (2 * BLOCK, BLOCK), 1)
    band = (kk - qq >= 0) & (kk - qq <= BLOCK)
    kmask[0] = jnp.where(band, 0.0, NEG).astype(BF16)
    kmask[1] = jnp.where(band & (kk >= BLOCK), 0.0, NEG).astype(BF16)
    row = lax.broadcasted_iota(jnp.int32, (HEAD_PAIR * BLOCK, BLOCK), 0) % BLOCK
    col = lax.broadcasted_iota(jnp.int32, (HEAD_PAIR * BLOCK, BLOCK), 1)
    qsel[...] = jnp.where(row == col, 1.0, 0.0).astype(BF16)

    lane = lax.broadcasted_iota(jnp.int32, (BLOCK, LANES), 1)
    low = lane < HEAD_DIM
    nt = (((1,), (1,)), ((), ()))

    def keys(prev_ref, cur_ref, qs, d):
        span = d * BLOCK
        if qs >= span:
            return cur_ref[pl.ds(qs - span, 2 * BLOCK, stride=d), :]
        return jnp.concatenate([prev_ref[pl.ds(SUPER + qs - span, BLOCK, stride=d), :],
                                cur_ref[pl.ds(qs, BLOCK, stride=d), :]], axis=0)

    for pat, (_, d) in enumerate(DILATED_PATTERNS):
        for r in range(d):
            for j in range(SUPER // (d * BLOCK)):
                qs = r + d * BLOCK * j
                rows = pl.ds(qs, BLOCK, stride=d)
                qb = q_ref[rows, :]
                q2 = jnp.concatenate([jnp.where(low, qb, 0.0), jnp.where(low, 0.0, qb)], axis=0).astype(BF16)
                kb = keys(kp_ref, kc_ref, qs, d).astype(BF16)
                vb = keys(vp_ref, vc_ref, qs, d).astype(BF16)
                km = kmask[first_super] if j == 0 else kmask[0]
                s = lax.dot_general(jnp.concatenate([q2, qsel[...]], axis=1),
                                    jnp.concatenate([kb, km], axis=1), nt, preferred_element_type=F32)
                m = jnp.max(s, axis=1, keepdims=True)
                p = jnp.exp2(s - m)
                l = jnp.sum(p, axis=1, keepdims=True)
                o2 = jnp.dot(p.astype(BF16), vb, preferred_element_type=F32)
                acc[pat, rows, :] = jnp.where(low, o2[:BLOCK], o2[BLOCK:])
                mx[pat, rows, :] = jnp.where(low, m[:BLOCK], m[BLOCK:])
                sm[pat, rows, :] = jnp.where(low, l[:BLOCK], l[BLOCK:])

    m_all = jnp.maximum(jnp.maximum(mx[0], mx[1]), mx[2])
    num = jnp.zeros((SUPER, LANES), F32)
    den = jnp.zeros((SUPER, LANES), F32)
    for pat in range(len(DILATED_PATTERNS)):
        w = jnp.exp2(mx[pat] - m_all)
        num = num + w * acc[pat]
        den = den + w * sm[pat]
    o_ref[...] = (num / den).astype(BF16) * sg_ref[...]


def _attend_prompt(q, k, v, sg):
    b, s, _ = q.shape
    n_super = s // SUPER
    cur = pl.BlockSpec((None, SUPER, LANES), lambda bi, sb, hp: (bi, sb, hp))
    prev = pl.BlockSpec((None, SUPER, LANES), lambda bi, sb, hp: (bi, jnp.maximum(sb - 1, 0), hp))
    n_pat = len(DILATED_PATTERNS)
    return pl.pallas_call(
        _attn_kernel,
        grid=(b, n_super, N_HEADS // HEAD_PAIR),
        in_specs=[cur, prev, cur, prev, cur, cur],
        out_specs=cur,
        out_shape=jax.ShapeDtypeStruct((b, s, ATTN_WIDTH), BF16),
        scratch_shapes=[pltpu.VMEM((n_pat, SUPER, LANES), F32),
                        pltpu.VMEM((n_pat, SUPER, LANES), F32),
                        pltpu.VMEM((n_pat, SUPER, LANES), F32),
                        pltpu.VMEM((HEAD_PAIR * BLOCK, BLOCK), BF16),
                        pltpu.VMEM((2, 2 * BLOCK, BLOCK), BF16)],
        compiler_params=pltpu.CompilerParams(dimension_semantics=("arbitrary",) * 3,
                                             vmem_limit_bytes=VMEM_LIMIT),
        name="attn_prompt",
    )(q, k, k, v, v, sg)


def _merge_kernel(x_ref, a_ref, u_ref, uh_ref, sgp_ref, wp_ref, ps_ref, wo_ref, fg_ref,
                  pc_ref, pn_ref, vn_ref, cv_ref, y_ref, os_ref, ov_ref, ue, *, n_new):
    i = pl.program_id(1)
    rows = u_ref.shape[1]
    ue[0:HIST_ROWS] = jnp.where(i == 0, 0.0, uh_ref[0])
    ue[HIST_ROWS:HIST_ROWS + rows] = u_ref[0]
    pos = i * rows + lax.broadcasted_iota(jnp.int32, (rows, 1), 0)
    counts = [jnp.minimum(pos + 1, w).astype(F32) for w in POOL_WINDOWS]
    pooled = _pooled(lambda g, j: ue[pl.ds(HIST_ROWS - j, rows), _group_cols(g)], counts)
    pool = _pool_project(lambda g: pooled[g].astype(BF16), wp_ref, ps_ref[...])
    mix = jnp.concatenate([a_ref[0], (pool * sgp_ref[0].astype(F32)).astype(BF16)], axis=-1)
    h = x_ref[0] + jnp.dot(mix, wo_ref[...], preferred_element_type=F32)
    y_ref[0] = _rms(h, fg_ref[...])
    _sample_values(pc_ref, pn_ref, vn_ref, cv_ref, os_ref, ov_ref, n_new)


def _merge_prompt_value_sample(x, mix_a, u, sgp, wp_bf, pool_scale, wo_bf, final_g, pc, pn, vnt, cv, n_new):
    b, s, _ = x.shape
    n_batch, h, dh, buf = cv.shape
    q_rows = pc.shape[2]
    tiles = s // MERGE_ROWS
    hosted = n_batch * h // (b * tiles)
    assert hosted * b * tiles == n_batch * h and h % hosted == 0, "sample heads split evenly over the grid"
    per_tile = MERGE_ROWS // HIST_ROWS
    tok = lambda width: pl.BlockSpec((1, MERGE_ROWS, width), lambda bi, i: (bi, i, 0))
    const = lambda shape: pl.BlockSpec(shape, lambda bi, i: (0,) * len(shape))
    heads = _hosted_heads((b, tiles), h // hosted, hosted)
    return pl.pallas_call(
        functools.partial(_merge_kernel, n_new=n_new),
        grid=(b, tiles),
        in_specs=[tok(D_MODEL), tok(ATTN_WIDTH), tok(POOL_WIDTH),
                  pl.BlockSpec((1, HIST_ROWS, POOL_WIDTH),
                               lambda bi, i: (bi, jnp.maximum(i * per_tile - 1, 0), 0)),
                  tok(POOL_WIDTH),
                  const((len(POOL_WINDOWS), POOL_GROUP_WIDTH, POOL_GROUP_WIDTH)),
                  const((1, POOL_WIDTH)),
                  pl.BlockSpec((D_MODEL, D_MODEL), lambda bi, i: (0, 0), pipeline_mode=pl.Buffered(1)),
                  const((1, D_MODEL)),
                  heads(q_rows, buf), heads(q_rows, LANES), heads(dh, LANES), heads(dh, buf)],
        out_specs=[tok(D_MODEL), heads(q_rows, dh), heads(dh, buf)],
        out_shape=[jax.ShapeDtypeStruct((b, s, D_MODEL), F32),
                   jax.ShapeDtypeStruct((n_batch, h, q_rows, dh), F32),
                   jax.ShapeDtypeStruct((n_batch, h, dh, buf), F32)],
        scratch_shapes=[pltpu.VMEM((HIST_ROWS + MERGE_ROWS, POOL_WIDTH), F32)],
        compiler_params=pltpu.CompilerParams(dimension_semantics=("arbitrary", "arbitrary"),
                                             vmem_limit_bytes=VMEM_LIMIT),
        name="merge_prompt_value_sample",
    )(x, mix_a, u, u, sgp, wp_bf, pool_scale, wo_bf, final_g, pc, pn, vnt, cv)


def _merge_sample_kernel(x_ref, o_ref, z_ref, hist_ref, wp_ref, ps_ref, wo_ref, fg_ref, y_ref, pool_ref, ue,
                         *, n_new, n_batch):
    u_col = 4 * ATTN_WIDTH
    ue[0:POOL_HIST] = hist_ref[...]
    ue[POOL_HIST:POOL_HIST + n_new] = z_ref[:, u_col:u_col + POOL_WIDTH].reshape(n_new, n_batch, POOL_WIDTH)
    pool_ref[...] = ue[n_new:n_new + POOL_HIST]

    rows = n_new * n_batch

    def window_sum(g, j):
        return ue[pl.ds(POOL_HIST - j, n_new), :, _group_cols(g)].reshape(rows, POOL_GROUP_WIDTH)

    pos = PAST_LEN + lax.broadcasted_iota(jnp.int32, (rows, 1), 0) // n_batch
    counts = [jnp.minimum(pos + 1, w).astype(F32) for w in POOL_WINDOWS]
    pooled = _pooled(window_sum, counts)
    pool = _pool_project(lambda g: pooled[g].astype(BF16), wp_ref, ps_ref[...])
    ga = z_ref[:, 3 * ATTN_WIDTH:4 * ATTN_WIDTH]
    gp = z_ref[:, u_col + POOL_WIDTH:u_col + 2 * POOL_WIDTH]
    mix = jnp.concatenate([o_ref[...] * _silu(ga), pool * _silu(gp)], axis=-1).astype(BF16)
    h = x_ref[...] + jnp.dot(mix, wo_ref[...], preferred_element_type=F32)
    y_ref[...] = _rms(h, fg_ref[...])


def _merge_sample(x_tb, o_tb, z_tb, hist, wp_bf, pool_scale, wo_bf, final_g, n_new, n_batch):
    return pl.pallas_call(
        functools.partial(_merge_sample_kernel, n_new=n_new, n_batch=n_batch),
        out_shape=[jax.ShapeDtypeStruct(x_tb.shape, F32),
                   jax.ShapeDtypeStruct((POOL_HIST, n_batch, POOL_WIDTH), F32)],
        scratch_shapes=[pltpu.VMEM((POOL_HIST + n_new, n_batch, POOL_WIDTH), F32)],
        compiler_params=pltpu.CompilerParams(vmem_limit_bytes=VMEM_LIMIT),
        name="merge_sample",
    )(x_tb, o_tb, z_tb, hist, wp_bf, pool_scale, wo_bf, final_g)


def _to_feature_major(z_cols, n_new, n_batch):
    t = z_cols.reshape(n_new, n_batch, N_HEADS, HEAD_DIM).transpose(1, 2, 3, 0)
    return jnp.pad(t, ((0, 0), (0, 0), (0, 0), (0, LANES - n_new)))


def kernel(x_prompt, x_sample, cache_k, cache_v, state_pool, norm_g, w_in, w_pool, pool_scale, w_out, final_norm_g):
    depth = norm_g.shape[0]
    assert depth == 1, "single-layer step"
    b, s, _ = x_prompt.shape
    n_batch, n_new, _ = x_sample.shape
    buf = cache_k.shape[2]
    assert s % SUPER == 0 and s >= MAX_WINDOW and buf == MAX_WINDOW and n_new <= POOL_HIST

    g = norm_g[0][None, :]
    fg = final_norm_g[None, :]
    wo_bf = w_out[0].astype(BF16)
    wp_bf = w_pool[0].astype(BF16)
    ps = pool_scale[0][None, :]

    x_tb = x_sample.transpose(1, 0, 2).reshape(n_new * n_batch, D_MODEL)
    z_tb, w_bf = _project_sample(x_tb, g, w_in[0])
    q_s = z_tb[:, 0:ATTN_WIDTH].reshape(n_new, n_batch, N_HEADS, HEAD_DIM).transpose(1, 2, 0, 3)
    q_s = jnp.pad(q_s, ((0, 0), (0, 0), (0, SUBLANES - n_new), (0, 0)))
    knt = _to_feature_major(z_tb[:, ATTN_WIDTH:2 * ATTN_WIDTH], n_new, n_batch)
    vnt = _to_feature_major(z_tb[:, 2 * ATTN_WIDTH:3 * ATTN_WIDTH], n_new, n_batch)
    ck = cache_k[0].transpose(0, 2, 3, 1)
    cv = cache_v[0].transpose(0, 2, 3, 1)

    q, k, v, sg, u, sgp, kt, vt, pc, pn, ok = _project_prompt_score_sample(x_prompt, g, w_bf, q_s, knt, ck, n_new)
    mix_a = _attend_prompt(q, k, v, sg)
    y_prompt, o_s, ov = _merge_prompt_value_sample(x_prompt, mix_a, u, sgp, wp_bf, ps, wo_bf, fg,
                                                   pc, pn, vnt, cv, n_new)
    to_window = lambda t: t.reshape(b, N_HEADS, HEAD_DIM, MAX_WINDOW).transpose(0, 3, 1, 2)[None]
    k_prompt, v_prompt = to_window(kt), to_window(vt)
    pool_prompt = u[:, s - POOL_HIST:][None]

    o_tb = o_s[:, :, :n_new].transpose(2, 0, 1, 3).reshape(n_new * n_batch, ATTN_WIDTH)
    hist = state_pool[0].transpose(1, 0, 2)
    y_tb, pool_tb = _merge_sample(x_tb, o_tb, z_tb, hist, wp_bf, ps, wo_bf, fg, n_new, n_batch)
    y_sample = y_tb.reshape(n_new, n_batch, D_MODEL).transpose(1, 0, 2)
    k_sample = ok.transpose(0, 3, 1, 2)[None]
    v_sample = ov.transpose(0, 3, 1, 2)[None]
    pool_sample = pool_tb.transpose(1, 0, 2)[None]

    return (y_prompt, y_sample, k_prompt, v_prompt, pool_prompt, k_sample, v_sample, pool_sample)
```

```python
import functools

import jax
import jax.numpy as jnp
from jax import lax
from jax.experimental import pallas as pl
from jax.experimental.pallas import tpu as pltpu

F32 = jnp.float32
BF16 = jnp.bfloat16

D_MODEL = 2048
ATTN_WIDTH = 1024
HEAD_DIM = 64
N_HEADS = 16
POOL_WIDTH = 1024
POOL_WINDOWS = (2, 4, 8, 16)
POOL_GROUP_WIDTH = 256
POOL_HIST = 15
IN_WIDTH = 6144
DILATED_PATTERNS = ((128, 1), (512, 4), (2048, 16))
MAX_WINDOW = 2048
BLOCK = 128
PAST_LEN = 16384
EPS = 1e-6
ATTN_SCALE = HEAD_DIM ** -0.5
NEG = -1e30
LOG2E = 1.4426950408889634

LANES = 128
SUBLANES = 8
HEAD_PAIR = LANES // HEAD_DIM
HIST_ROWS = 16

PROJ_ROWS = 256
MERGE_ROWS = 256
SUPER = 2048
CAST_COLS = 1024
VMEM_LIMIT = 58 * 1024 * 1024


def _silu(x):
    return x / (1.0 + jnp.exp(-x))


def _rms(x, g):
    ms = jnp.mean(x * x, axis=-1, keepdims=True)
    return x * lax.rsqrt(ms + EPS) * g


def _group_cols(g):
    return slice(g * POOL_GROUP_WIDTH, (g + 1) * POOL_GROUP_WIDTH)


def _pooled(window_sum, counts):
    outs = []
    for g, w in enumerate(POOL_WINDOWS):
        cur = window_sum(g, 0)
        tot = cur
        for j in range(1, w):
            tot = tot + window_sum(g, j)
        outs.append(tot / counts[g] - cur)
    return outs


def _pool_project(pooled_group, wp_ref, scale):
    outs = [jnp.dot(pooled_group(g), wp_ref[g], preferred_element_type=F32) for g in range(len(POOL_WINDOWS))]
    return jnp.concatenate(outs, axis=-1) * scale


def _multiplicity(delta):
    c = jnp.zeros(delta.shape, F32)
    for window, d in DILATED_PATTERNS:
        assert d & (d - 1) == 0, "power-of-two dilation: residue test is a bit mask"
        c = c + jnp.where(((delta & (d - 1)) == 0) & (delta <= window), 1.0, 0.0)
    return c


def _shift_cache(src, new, dst_ref, hh, n_new):
    buf = src.shape[-1]
    tail_lane = lax.broadcasted_iota(jnp.int32, (HEAD_DIM, LANES), 1)
    rolled = pltpu.roll(src, buf - n_new, 1)
    tail = rolled[:, buf - LANES:]
    placed = pltpu.roll(new, LANES - n_new, 1)
    dst_ref[0, hh, :, 0:buf - LANES] = rolled[:, 0:buf - LANES]
    dst_ref[0, hh, :, buf - LANES:buf] = jnp.where(tail_lane >= LANES - n_new, placed, tail)


def _sample_scores(q_ref, kn_ref, ck_ref, pc_ref, pn_ref, ok_ref, n_new):
    buf = ck_ref.shape[-1]
    q_rows = q_ref.shape[2]
    qi = lax.broadcasted_iota(jnp.int32, (q_rows, buf), 0)
    cc = _multiplicity(buf + qi - lax.broadcasted_iota(jnp.int32, (q_rows, buf), 1))
    qn = lax.broadcasted_iota(jnp.int32, (q_rows, LANES), 0)
    nl = lax.broadcasted_iota(jnp.int32, (q_rows, LANES), 1)
    cn = jnp.where(nl <= qn, _multiplicity(qn - jnp.minimum(nl, qn)), 0.0)

    for hh in range(ck_ref.shape[1]):
        kc = ck_ref[0, hh]
        knt = kn_ref[0, hh]
        qh = (q_ref[0, hh] * ATTN_SCALE).astype(BF16)
        sc = jnp.dot(qh, kc.astype(BF16), preferred_element_type=F32)
        sn = jnp.dot(qh, knt.astype(BF16), preferred_element_type=F32)
        sc = jnp.where(cc > 0, sc, NEG)
        sn = jnp.where(cn > 0, sn, NEG)
        m = jnp.maximum(jnp.max(sc, axis=1, keepdims=True), jnp.max(sn, axis=1, keepdims=True))
        pc = cc * jnp.exp(sc - m)
        pn = cn * jnp.exp(sn - m)
        den = jnp.sum(pc, axis=1, keepdims=True) + jnp.sum(pn, axis=1, keepdims=True)
        pc_ref[0, hh] = pc / den
        pn_ref[0, hh] = pn / den
        _shift_cache(kc, knt, ok_ref, hh, n_new)


def _sample_values(pc_ref, pn_ref, vn_ref, cv_ref, o_ref, ov_ref, n_new):
    nt = (((1,), (1,)), ((), ()))
    for hh in range(cv_ref.shape[1]):
        vc = cv_ref[0, hh]
        vnt = vn_ref[0, hh]
        o_ref[0, hh] = (
            lax.dot_general(pc_ref[0, hh].astype(BF16), vc.astype(BF16), nt, preferred_element_type=F32)
            + lax.dot_general(pn_ref[0, hh].astype(BF16), vnt.astype(BF16), nt, preferred_element_type=F32))
        _shift_cache(vc, vnt, ov_ref, hh, n_new)


def _hosted_heads(grid, head_groups, heads_per_step):
    def spec(*tail):
        def index(*ids):
            step = 0
            for size, idx in zip(grid, ids, strict=True):
                step = step * size + idx
            return (step // head_groups, step % head_groups, 0, 0)
        return pl.BlockSpec((1, heads_per_step) + tail, index)
    return spec


def _proj_sample_kernel(x_ref, g_ref, w_ref, z_ref, wbf_ref):
    xn = _rms(x_ref[...], g_ref[...]).astype(BF16)
    w = w_ref[...].astype(BF16)
    wbf_ref[...] = w
    z_ref[...] = jnp.dot(xn, w, preferred_element_type=F32)


def _project_sample(x_tb, g, w):
    rows = x_tb.shape[0]
    cols = pl.BlockSpec((D_MODEL, CAST_COLS), lambda c: (0, c))
    return pl.pallas_call(
        _proj_sample_kernel,
        grid=(IN_WIDTH // CAST_COLS,),
        in_specs=[pl.BlockSpec((rows, D_MODEL), lambda c: (0, 0)),
                  pl.BlockSpec((1, D_MODEL), lambda c: (0, 0)),
                  cols],
        out_specs=[pl.BlockSpec((rows, CAST_COLS), lambda c: (0, c)), cols],
        out_shape=[jax.ShapeDtypeStruct((rows, IN_WIDTH), F32),
                   jax.ShapeDtypeStruct((D_MODEL, IN_WIDTH), BF16)],
        compiler_params=pltpu.CompilerParams(dimension_semantics=("arbitrary",),
                                             vmem_limit_bytes=VMEM_LIMIT),
        name="proj_sample",
    )(x_tb, g, w)


def _proj_kernel(x_ref, g_ref, w_ref, qs_ref, kn_ref, ck_ref,
                 q_ref, k_ref, v_ref, sg_ref, u_ref, sgp_ref, kt_ref, vt_ref, pc_ref, pn_ref, ok_ref, *, n_new):
    _sample_scores(qs_ref, kn_ref, ck_ref, pc_ref, pn_ref, ok_ref, n_new)

    xn = _rms(x_ref[0], g_ref[...]).astype(BF16)

    def part(c):
        return jnp.dot(xn, w_ref[:, c * ATTN_WIDTH:(c + 1) * ATTN_WIDTH], preferred_element_type=F32)

    sg_ref[0] = _silu(part(3)).astype(BF16)
    sgp_ref[0] = _silu(part(5)).astype(BF16)
    u_ref[0] = part(4)
    q_ref[0] = part(0) * (ATTN_SCALE * LOG2E)
    k = part(1)
    k_ref[0] = k
    v = part(2)
    v_ref[0] = v
    kt_ref[0] = k.T
    vt_ref[0] = v.T


def _project_prompt_score_sample(x, g, w_bf, q_s, knt, ck, n_new):
    b, s, _ = x.shape
    n_batch, h, dh, buf = ck.shape
    q_rows = q_s.shape[2]
    tiles = s // PROJ_ROWS
    hosted = n_batch * h // (b * tiles)
    assert hosted * b * tiles == n_batch * h and h % hosted == 0, "sample heads split evenly over the grid"
    first_kept = (s - MAX_WINDOW) // PROJ_ROWS
    tok = lambda width: pl.BlockSpec((1, PROJ_ROWS, width), lambda bi, i: (bi, i, 0))
    feat = pl.BlockSpec((1, ATTN_WIDTH, PROJ_ROWS), lambda bi, i: (bi, 0, jnp.maximum(i - first_kept, 0)))
    heads = _hosted_heads((b, tiles), h // hosted, hosted)
    f32 = lambda *shape: jax.ShapeDtypeStruct(shape, F32)
    bf16 = lambda *shape: jax.ShapeDtypeStruct(shape, BF16)
    return pl.pallas_call(
        functools.partial(_proj_kernel, n_new=n_new),
        grid=(b, tiles),
        in_specs=[tok(D_MODEL),
                  pl.BlockSpec((1, D_MODEL), lambda bi, i: (0, 0)),
                  pl.BlockSpec((D_MODEL, IN_WIDTH), lambda bi, i: (0, 0), pipeline_mode=pl.Buffered(1)),
                  heads(q_rows, dh), heads(dh, LANES), heads(dh, buf)],
        out_specs=[tok(ATTN_WIDTH)] * 6 + [feat, feat,
                   heads(q_rows, buf), heads(q_rows, LANES), heads(dh, buf)],
        out_shape=[f32(b, s, ATTN_WIDTH)] * 3
        + [bf16(b, s, ATTN_WIDTH), f32(b, s, POOL_WIDTH), bf16(b, s, POOL_WIDTH)]
        + [f32(b, ATTN_WIDTH, MAX_WINDOW)] * 2
        + [f32(n_batch, h, q_rows, buf), f32(n_batch, h, q_rows, LANES), f32(n_batch, h, dh, buf)],
        compiler_params=pltpu.CompilerParams(dimension_semantics=("arbitrary", "arbitrary"),
                                             vmem_limit_bytes=VMEM_LIMIT),
        name="proj_prompt_score_sample",
    )(x, g, w_bf, q_s, knt, ck)


def _attn_kernel(q_ref, kp_ref, kc_ref, vp_ref, vc_ref, sg_ref, o_ref, acc, mx, sm, qsel, kmask):
    first_super = (pl.program_id(1) == 0).astype(jnp.int32)

    kk = lax.broadcasted_iota(jnp.int32, (2 * BLOCK, BLOCK), 0)
    qq = lax.broadcasted_iota(jnp.int32, (2 * BLOCK, BLOCK), 1)
    band = (kk - qq >= 0) & (kk - qq <= BLOCK)
    kmask[0] = jnp.where(band, 0.0, NEG).astype(BF16)
    kmask[1] = jnp.where(band & (kk >= BLOCK), 0.0, NEG).astype(BF16)
    row = lax.broadcasted_iota(jnp.int32, (HEAD_PAIR * BLOCK, BLOCK), 0) % BLOCK
    col = lax.broadcasted_iota(jnp.int32, (HEAD_PAIR * BLOCK, BLOCK), 1)
    qsel[...] = jnp.where(row == col, 1.0, 0.0).astype(BF16)

    lane = lax.broadcasted_iota(jnp.int32, (BLOCK, LANES), 1)
    low = lane < HEAD_DIM
    ones = jnp.ones((2 * BLOCK, LANES), BF16)
    nt = (((1,), (1,)), ((), ()))

    def keys(prev_ref, cur_ref, qs, d):
        span = d * BLOCK
        if qs >= span:
            return cur_ref[pl.ds(qs - span, 2 * BLOCK, stride=d), :]
        return jnp.concatenate([prev_ref[pl.ds(SUPER + qs - span, BLOCK, stride=d), :],
                                cur_ref[pl.ds(qs, BLOCK, stride=d), :]], axis=0)

    for pat, (_, d) in enumerate(DILATED_PATTERNS):
        for r in range(d):
            for j in range(SUPER // (d * BLOCK)):
                qs = r + d * BLOCK * j
                rows = pl.ds(qs, BLOCK, stride=d)
                qb = q_ref[rows, :]
                q2 = jnp.concatenate([jnp.where(low, qb, 0.0), jnp.where(low, 0.0, qb)], axis=0).astype(BF16)
                kb = keys(kp_ref, kc_ref, qs, d).astype(BF16)
                vb = keys(vp_ref, vc_ref, qs, d).astype(BF16)
                km = kmask[first_super] if j == 0 else kmask[0]
                s = lax.dot_general(jnp.concatenate([q2, qsel[...]], axis=1),
                                    jnp.concatenate([kb, km], axis=1), nt, preferred_element_type=F32)
                m = jnp.max(s, axis=1, keepdims=True)
                p = jnp.exp2(s - m)
                ol = jnp.dot(p.astype(BF16), jnp.concatenate([vb, ones], axis=1),
                             preferred_element_type=F32)
                acc[pat, rows, :] = jnp.where(low, ol[:BLOCK, :LANES], ol[BLOCK:, :LANES])
                mx[pat, rows, :] = jnp.where(low, m[:BLOCK], m[BLOCK:])
                sm[pat, rows, :] = jnp.where(low, ol[:BLOCK, LANES:], ol[BLOCK:, LANES:])

    m_all = jnp.maximum(jnp.maximum(mx[0], mx[1]), mx[2])
    num = jnp.zeros((SUPER, LANES), F32)
    den = jnp.zeros((SUPER, LANES), F32)
    for pat in range(len(DILATED_PATTERNS)):
        w = jnp.exp2(mx[pat] - m_all)
        num = num + w * acc[pat]
        den = den + w * sm[pat]
    o_ref[...] = (num / den).astype(BF16) * sg_ref[...]


def _attend_prompt(q, k, v, sg):
    b, s, _ = q.shape
    n_super = s // SUPER
    cur = pl.BlockSpec((None, SUPER, LANES), lambda bi, sb, hp: (bi, sb, hp))
    prev = pl.BlockSpec((None, SUPER, LANES), lambda bi, sb, hp: (bi, jnp.maximum(sb - 1, 0), hp))
    n_pat = len(DILATED_PATTERNS)
    return pl.pallas_call(
        _attn_kernel,
        grid=(b, n_super, N_HEADS // HEAD_PAIR),
        in_specs=[cur, prev, cur, prev, cur, cur],
        out_specs=cur,
        out_shape=jax.ShapeDtypeStruct((b, s, ATTN_WIDTH), BF16),
        scratch_shapes=[pltpu.VMEM((n_pat, SUPER, LANES), F32),
                        pltpu.VMEM((n_pat, SUPER, LANES), F32),
                        pltpu.VMEM((n_pat, SUPER, LANES), F32),
                        pltpu.VMEM((HEAD_PAIR * BLOCK, BLOCK), BF16),
                        pltpu.VMEM((2, 2 * BLOCK, BLOCK), BF16)],
        compiler_params=pltpu.CompilerParams(dimension_semantics=("arbitrary",) * 3,
                                             vmem_limit_bytes=VMEM_LIMIT),
        name="attn_prompt",
    )(q, k, k, v, v, sg)


def _merge_kernel(x_ref, a_ref, u_ref, uh_ref, sgp_ref, wp_ref, ps_ref, wo_ref, fg_ref,
                  pc_ref, pn_ref, vn_ref, cv_ref, y_ref, os_ref, ov_ref, ue, *, n_new):
    i = pl.program_id(1)
    rows = u_ref.shape[1]
    ue[0:HIST_ROWS] = jnp.where(i == 0, 0.0, uh_ref[0])
    ue[HIST_ROWS:HIST_ROWS + rows] = u_ref[0]
    pos = i * rows + lax.broadcasted_iota(jnp.int32, (rows, 1), 0)
    counts = [jnp.minimum(pos + 1, w).astype(F32) for w in POOL_WINDOWS]
    pooled = _pooled(lambda g, j: ue[pl.ds(HIST_ROWS - j, rows), _group_cols(g)], counts)
    pool = _pool_project(lambda g: pooled[g].astype(BF16), wp_ref, ps_ref[...])
    mix = jnp.concatenate([a_ref[0], (pool * sgp_ref[0].astype(F32)).astype(BF16)], axis=-1)
    h = x_ref[0] + jnp.dot(mix, wo_ref[...], preferred_element_type=F32)
    y_ref[0] = _rms(h, fg_ref[...])
    _sample_values(pc_ref, pn_ref, vn_ref, cv_ref, os_ref, ov_ref, n_new)


def _merge_prompt_value_sample(x, mix_a, u, sgp, wp_bf, pool_scale, wo_bf, final_g, pc, pn, vnt, cv, n_new):
    b, s, _ = x.shape
    n_batch, h, dh, buf = cv.shape
    q_rows = pc.shape[2]
    tiles = s // MERGE_ROWS
    hosted = n_batch * h // (b * tiles)
    assert hosted * b * tiles == n_batch * h and h % hosted == 0, "sample heads split evenly over the grid"
    per_tile = MERGE_ROWS // HIST_ROWS
    tok = lambda width: pl.BlockSpec((1, MERGE_ROWS, width), lambda bi, i: (bi, i, 0))
    const = lambda shape: pl.BlockSpec(shape, lambda bi, i: (0,) * len(shape))
    heads = _hosted_heads((b, tiles), h // hosted, hosted)
    return pl.pallas_call(
        functools.partial(_merge_kernel, n_new=n_new),
        grid=(b, tiles),
        in_specs=[tok(D_MODEL), tok(ATTN_WIDTH), tok(POOL_WIDTH),
                  pl.BlockSpec((1, HIST_ROWS, POOL_WIDTH),
                               lambda bi, i: (bi, jnp.maximum(i * per_tile - 1, 0), 0)),
                  tok(POOL_WIDTH),
                  const((len(POOL_WINDOWS), POOL_GROUP_WIDTH, POOL_GROUP_WIDTH)),
                  const((1, POOL_WIDTH)),
                  pl.BlockSpec((D_MODEL, D_MODEL), lambda bi, i: (0, 0), pipeline_mode=pl.Buffered(1)),
                  const((1, D_MODEL)),
                  heads(q_rows, buf), heads(q_rows, LANES), heads(dh, LANES), heads(dh, buf)],
        out_specs=[tok(D_MODEL), heads(q_rows, dh), heads(dh, buf)],
        out_shape=[jax.ShapeDtypeStruct((b, s, D_MODEL), F32),
                   jax.ShapeDtypeStruct((n_batch, h, q_rows, dh), F32),
                   jax.ShapeDtypeStruct((n_batch, h, dh, buf), F32)],
        scratch_shapes=[pltpu.VMEM((HIST_ROWS + MERGE_ROWS, POOL_WIDTH), F32)],
        compiler_params=pltpu.CompilerParams(dimension_semantics=("arbitrary", "arbitrary"),
                                             vmem_limit_bytes=VMEM_LIMIT),
        name="merge_prompt_value_sample",
    )(x, mix_a, u, u, sgp, wp_bf, pool_scale, wo_bf, final_g, pc, pn, vnt, cv)


def _merge_sample_kernel(x_ref, o_ref, z_ref, hist_ref, wp_ref, ps_ref, wo_ref, fg_ref, y_ref, pool_ref, ue,
                         *, n_new, n_batch):
    u_col = 4 * ATTN_WIDTH
    ue[0:POOL_HIST] = hist_ref[...]
    ue[POOL_HIST:POOL_HIST + n_new] = z_ref[:, u_col:u_col + POOL_WIDTH].reshape(n_new, n_batch, POOL_WIDTH)
    pool_ref[...] = ue[n_new:n_new + POOL_HIST]

    rows = n_new * n_batch

    def window_sum(g, j):
        return ue[pl.ds(POOL_HIST - j, n_new), :, _group_cols(g)].reshape(rows, POOL_GROUP_WIDTH)

    pos = PAST_LEN + lax.broadcasted_iota(jnp.int32, (rows, 1), 0) // n_batch
    counts = [jnp.minimum(pos + 1, w).astype(F32) for w in POOL_WINDOWS]
    pooled = _pooled(window_sum, counts)
    pool = _pool_project(lambda g: pooled[g].astype(BF16), wp_ref, ps_ref[...])
    ga = z_ref[:, 3 * ATTN_WIDTH:4 * ATTN_WIDTH]
    gp = z_ref[:, u_col + POOL_WIDTH:u_col + 2 * POOL_WIDTH]
    mix = jnp.concatenate([o_ref[...] * _silu(ga), pool * _silu(gp)], axis=-1).astype(BF16)
    h = x_ref[...] + jnp.dot(mix, wo_ref[...], preferred_element_type=F32)
    y_ref[...] = _rms(h, fg_ref[...])


def _merge_sample(x_tb, o_tb, z_tb, hist, wp_bf, pool_scale, wo_bf, final_g, n_new, n_batch):
    return pl.pallas_call(
        functools.partial(_merge_sample_kernel, n_new=n_new, n_batch=n_batch),
        out_shape=[jax.ShapeDtypeStruct(x_tb.shape, F32),
                   jax.ShapeDtypeStruct((POOL_HIST, n_batch, POOL_WIDTH), F32)],
        scratch_shapes=[pltpu.VMEM((POOL_HIST + n_new, n_batch, POOL_WIDTH), F32)],
        compiler_params=pltpu.CompilerParams(vmem_limit_bytes=VMEM_LIMIT),
        name="merge_sample",
    )(x_tb, o_tb, z_tb, hist, wp_bf, pool_scale, wo_bf, final_g)


def _to_feature_major(z_cols, n_new, n_batch):
    t = z_cols.reshape(n_new, n_batch, N_HEADS, HEAD_DIM).transpose(1, 2, 3, 0)
    return jnp.pad(t, ((0, 0), (0, 0), (0, 0), (0, LANES - n_new)))


def kernel(x_prompt, x_sample, cache_k, cache_v, state_pool, norm_g, w_in, w_pool, pool_scale, w_out, final_norm_g):
    depth = norm_g.shape[0]
    assert depth == 1, "single-layer step"
    b, s, _ = x_prompt.shape
    n_batch, n_new, _ = x_sample.shape
    buf = cache_k.shape[2]
    assert s % SUPER == 0 and s >= MAX_WINDOW and buf == MAX_WINDOW and n_new <= POOL_HIST

    g = norm_g[0][None, :]
    fg = final_norm_g[None, :]
    wo_bf = w_out[0].astype(BF16)
    wp_bf = w_pool[0].astype(BF16)
    ps = pool_scale[0][None, :]

    x_tb = x_sample.transpose(1, 0, 2).reshape(n_new * n_batch, D_MODEL)
    z_tb, w_bf = _project_sample(x_tb, g, w_in[0])
    q_s = z_tb[:, 0:ATTN_WIDTH].reshape(n_new, n_batch, N_HEADS, HEAD_DIM).transpose(1, 2, 0, 3)
    q_s = jnp.pad(q_s, ((0, 0), (0, 0), (0, SUBLANES - n_new), (0, 0)))
    knt = _to_feature_major(z_tb[:, ATTN_WIDTH:2 * ATTN_WIDTH], n_new, n_batch)
    vnt = _to_feature_major(z_tb[:, 2 * ATTN_WIDTH:3 * ATTN_WIDTH], n_new, n_batch)
    ck = cache_k[0].transpose(0, 2, 3, 1)
    cv = cache_v[0].transpose(0, 2, 3, 1)

    q, k, v, sg, u, sgp, kt, vt, pc, pn, ok = _project_prompt_score_sample(x_prompt, g, w_bf, q_s, knt, ck, n_new)
    mix_a = _attend_prompt(q, k, v, sg)
    y_prompt, o_s, ov = _merge_prompt_value_sample(x_prompt, mix_a, u, sgp, wp_bf, ps, wo_bf, fg,
                                                   pc, pn, vnt, cv, n_new)
    to_window = lambda t: t.reshape(b, N_HEADS, HEAD_DIM, MAX_WINDOW).transpose(0, 3, 1, 2)[None]
    k_prompt, v_prompt = to_window(kt), to_window(vt)
    pool_prompt = u[:, s - POOL_HIST:][None]

    o_tb = o_s[:, :, :n_new].transpose(2, 0, 1, 3).reshape(n_new * n_batch, ATTN_WIDTH)
    hist = state_pool[0].transpose(1, 0, 2)
    y_tb, pool_tb = _merge_sample(x_tb, o_tb, z_tb, hist, wp_bf, ps, wo_bf, fg, n_new, n_batch)
    y_sample = y_tb.reshape(n_new, n_batch, D_MODEL).transpose(1, 0, 2)
    k_sample = ok.transpose(0, 3, 1, 2)[None]
    v_sample = ov.transpose(0, 3, 1, 2)[None]
    pool_sample = pool_tb.transpose(1, 0, 2)[None]

    return (y_prompt, y_sample, k_prompt, v_prompt, pool_prompt, k_sample, v_sample, pool_sample)
```

```python
import functools

import jax
import jax.numpy as jnp
from jax import lax
from jax.experimental import pallas as pl
from jax.experimental.pallas import tpu as pltpu

F32 = jnp.float32
BF16 = jnp.bfloat16

D_MODEL = 2048
ATTN_WIDTH = 1024
HEAD_DIM = 64
N_HEADS = 16
POOL_WIDTH = 1024
POOL_WINDOWS = (2, 4, 8, 16)
POOL_GROUP_WIDTH = 256
POOL_HIST = 15
IN_WIDTH = 6144
DILATED_PATTERNS = ((128, 1), (512, 4), (2048, 16))
MAX_WINDOW = 2048
BLOCK = 128
PAST_LEN = 16384
EPS = 1e-6
ATTN_SCALE = HEAD_DIM ** -0.5
NEG = -1e30
LOG2E = 1.4426950408889634

LANES = 128
SUBLANES = 8
HEAD_PAIR = LANES // HEAD_DIM
HIST_ROWS = 16

PROJ_ROWS = 256
MERGE_ROWS = 256
SUPER = 2048
CAST_COLS = 1024
VMEM_LIMIT = 58 * 1024 * 1024


def _silu(x):
    return x / (1.0 + jnp.exp(-x))


def _rms(x, g):
    ms = jnp.mean(x * x, axis=-1, keepdims=True)
    return x * lax.rsqrt(ms + EPS) * g


def _group_cols(g):
    return slice(g * POOL_GROUP_WIDTH, (g + 1) * POOL_GROUP_WIDTH)


def _pooled(window_sum, counts):
    outs = []
    for g, w in enumerate(POOL_WINDOWS):
        cur = window_sum(g, 0)
        tot = cur
        for j in range(1, w):
            tot = tot + window_sum(g, j)
        outs.append(tot / counts[g] - cur)
    return outs


def _pool_project(pooled_group, wp_ref, scale):
    outs = [jnp.dot(pooled_group(g), wp_ref[g], preferred_element_type=F32) for g in range(len(POOL_WINDOWS))]
    return jnp.concatenate(outs, axis=-1) * scale


def _multiplicity(delta):
    c = jnp.zeros(delta.shape, F32)
    for window, d in DILATED_PATTERNS:
        assert d & (d - 1) == 0, "power-of-two dilation: residue test is a bit mask"
        c = c + jnp.where(((delta & (d - 1)) == 0) & (delta <= window), 1.0, 0.0)
    return c


def _shift_cache(src, new, dst_ref, hh, n_new):
    buf = src.shape[-1]
    tail_lane = lax.broadcasted_iota(jnp.int32, (HEAD_DIM, LANES), 1)
    rolled = pltpu.roll(src, buf - n_new, 1)
    tail = rolled[:, buf - LANES:]
    placed = pltpu.roll(new, LANES - n_new, 1)
    dst_ref[0, hh, :, 0:buf - LANES] = rolled[:, 0:buf - LANES]
    dst_ref[0, hh, :, buf - LANES:buf] = jnp.where(tail_lane >= LANES - n_new, placed, tail)


def _sample_scores(q_ref, kn_ref, ck_ref, pc_ref, pn_ref, ok_ref, n_new):
    buf = ck_ref.shape[-1]
    q_rows = q_ref.shape[2]
    qi = lax.broadcasted_iota(jnp.int32, (q_rows, buf), 0)
    cc = _multiplicity(buf + qi - lax.broadcasted_iota(jnp.int32, (q_rows, buf), 1))
    qn = lax.broadcasted_iota(jnp.int32, (q_rows, LANES), 0)
    nl = lax.broadcasted_iota(jnp.int32, (q_rows, LANES), 1)
    cn = jnp.where(nl <= qn, _multiplicity(qn - jnp.minimum(nl, qn)), 0.0)

    for hh in range(ck_ref.shape[1]):
        kc = ck_ref[0, hh]
        knt = kn_ref[0, hh]
        qh = (q_ref[0, hh] * ATTN_SCALE).astype(BF16)
        sc = jnp.dot(qh, kc.astype(BF16), preferred_element_type=F32)
        sn = jnp.dot(qh, knt.astype(BF16), preferred_element_type=F32)
        sc = jnp.where(cc > 0, sc, NEG)
        sn = jnp.where(cn > 0, sn, NEG)
        m = jnp.maximum(jnp.max(sc, axis=1, keepdims=True), jnp.max(sn, axis=1, keepdims=True))
        pc = cc * jnp.exp(sc - m)
        pn = cn * jnp.exp(sn - m)
        den = jnp.sum(pc, axis=1, keepdims=True) + jnp.sum(pn, axis=1, keepdims=True)
        pc_ref[0, hh] = pc / den
        pn_ref[0, hh] = pn / den
        _shift_cache(kc, knt, ok_ref, hh, n_new)


def _sample_values(pc_ref, pn_ref, vn_ref, cv_ref, o_ref, ov_ref, n_new):
    nt = (((1,), (1,)), ((), ()))
    for hh in range(cv_ref.shape[1]):
        vc = cv_ref[0, hh]
        vnt = vn_ref[0, hh]
        o_ref[0, hh] = (
            lax.dot_general(pc_ref[0, hh].astype(BF16), vc.astype(BF16), nt, preferred_element_type=F32)
            + lax.dot_general(pn_ref[0, hh].astype(BF16), vnt.astype(BF16), nt, preferred_element_type=F32))
        _shift_cache(vc, vnt, ov_ref, hh, n_new)


def _hosted_heads(grid, head_groups, heads_per_step):
    def spec(*tail):
        def index(*ids):
            step = 0
            for size, idx in zip(grid, ids, strict=True):
                step = step * size + idx
            return (step // head_groups, step % head_groups, 0, 0)
        return pl.BlockSpec((1, heads_per_step) + tail, index)
    return spec


def _proj_sample_kernel(x_ref, g_ref, w_ref, z_ref, wbf_ref):
    xn = _rms(x_ref[...], g_ref[...]).astype(BF16)
    w = w_ref[...].astype(BF16)
    wbf_ref[...] = w
    z_ref[...] = jnp.dot(xn, w, preferred_element_type=F32)


def _project_sample(x_tb, g, w):
    rows = x_tb.shape[0]
    cols = pl.BlockSpec((D_MODEL, CAST_COLS), lambda c: (0, c))
    return pl.pallas_call(
        _proj_sample_kernel,
        grid=(IN_WIDTH // CAST_COLS,),
        in_specs=[pl.BlockSpec((rows, D_MODEL), lambda c: (0, 0)),
                  pl.BlockSpec((1, D_MODEL), lambda c: (0, 0)),
                  cols],
        out_specs=[pl.BlockSpec((rows, CAST_COLS), lambda c: (0, c)), cols],
        out_shape=[jax.ShapeDtypeStruct((rows, IN_WIDTH), F32),
                   jax.ShapeDtypeStruct((D_MODEL, IN_WIDTH), BF16)],
        compiler_params=pltpu.CompilerParams(dimension_semantics=("arbitrary",),
                                             vmem_limit_bytes=VMEM_LIMIT),
        name="proj_sample",
    )(x_tb, g, w)


def _proj_kernel(x_ref, g_ref, w_ref, qs_ref, kn_ref, ck_ref,
                 q_ref, k_ref, v_ref, sg_ref, u_ref, sgp_ref, kt_ref, vt_ref, pc_ref, pn_ref, ok_ref, *, n_new):
    _sample_scores(qs_ref, kn_ref, ck_ref, pc_ref, pn_ref, ok_ref, n_new)

    xn = _rms(x_ref[0], g_ref[...]).astype(BF16)

    def part(c):
        return jnp.dot(xn, w_ref[:, c * ATTN_WIDTH:(c + 1) * ATTN_WIDTH], preferred_element_type=F32)

    sg_ref[0] = _silu(part(3)).astype(BF16)
    sgp_ref[0] = _silu(part(5)).astype(BF16)
    u_ref[0] = part(4)
    q_ref[0] = part(0) * (ATTN_SCALE * LOG2E)
    k = part(1)
    k_ref[0] = k
    v = part(2)
    v_ref[0] = v
    kt_ref[0] = k.T
    vt_ref[0] = v.T


def _project_prompt_score_sample(x, g, w_bf, q_s, knt, ck, n_new):
    b, s, _ = x.shape
    n_batch, h, dh, buf = ck.shape
    q_rows = q_s.shape[2]
    tiles = s // PROJ_ROWS
    hosted = n_batch * h // (b * tiles)
    assert hosted * b * tiles == n_batch * h and h % hosted == 0, "sample heads split evenly over the grid"
    first_kept = (s - MAX_WINDOW) // PROJ_ROWS
    tok = lambda width: pl.BlockSpec((1, PROJ_ROWS, width), lambda bi, i: (bi, i, 0))
    feat = pl.BlockSpec((1, ATTN_WIDTH, PROJ_ROWS), lambda bi, i: (bi, 0, jnp.maximum(i - first_kept, 0)))
    heads = _hosted_heads((b, tiles), h // hosted, hosted)
    f32 = lambda *shape: jax.ShapeDtypeStruct(shape, F32)
    bf16 = lambda *shape: jax.ShapeDtypeStruct(shape, BF16)
    return pl.pallas_call(
        functools.partial(_proj_kernel, n_new=n_new),
        grid=(b, tiles),
        in_specs=[tok(D_MODEL),
                  pl.BlockSpec((1, D_MODEL), lambda bi, i: (0, 0)),
                  pl.BlockSpec((D_MODEL, IN_WIDTH), lambda bi, i: (0, 0), pipeline_mode=pl.Buffered(1)),
                  heads(q_rows, dh), heads(dh, LANES), heads(dh, buf)],
        out_specs=[tok(ATTN_WIDTH)] * 6 + [feat, feat,
                   heads(q_rows, buf), heads(q_rows, LANES), heads(dh, buf)],
        out_shape=[f32(b, s, ATTN_WIDTH)] * 3
        + [bf16(b, s, ATTN_WIDTH), f32(b, s, POOL_WIDTH), bf16(b, s, POOL_WIDTH)]
        + [f32(b, ATTN_WIDTH, MAX_WINDOW)] * 2
        + [f32(n_batch, h, q_rows, buf), f32(n_batch, h, q_rows, LANES), f32(n_batch, h, dh, buf)],
        compiler_params=pltpu.CompilerParams(dimension_semantics=("arbitrary", "arbitrary"),
                                             vmem_limit_bytes=VMEM_LIMIT),
        name="proj_prompt_score_sample",
    )(x, g, w_bf, q_s, knt, ck)


def _attn_kernel(q_ref, kp_ref, kc_ref, vp_ref, vc_ref, sg_ref, o_ref, acc, mx, sm, qsel, kmask):
    first_super = (pl.program_id(1) == 0).astype(jnp.int32)

    kk = lax.broadcasted_iota(jnp.int32, (2 * BLOCK, BLOCK), 0)
    qq = lax.broadcasted_iota(jnp.int32, (2 * BLOCK, BLOCK), 1)
    band = (kk - qq >= 0) & (kk - qq <= BLOCK)
    kmask[0] = jnp.where(band, 0.0, NEG).astype(BF16)
    kmask[1] = jnp.where(band & (kk >= BLOCK), 0.0, NEG).astype(BF16)
    row = lax.broadcasted_iota(jnp.int32, (HEAD_PAIR * BLOCK, BLOCK), 0) % BLOCK
    col = lax.broadcasted_iota(jnp.int32, (HEAD_PAIR * BLOCK, BLOCK), 1)
    qsel[...] = jnp.where(row == col, 1.0, 0.0).astype(BF16)

    lane = lax.broadcasted_iota(jnp.int32, (BLOCK, LANES), 1)
    low = lane < HEAD_DIM
    ones = jnp.ones((2 * BLOCK, LANES), BF16)
    nt = (((1,), (1,)), ((), ()))

    def keys(prev_ref, cur_ref, qs, d):
        span = d * BLOCK
        if qs >= span:
            return cur_ref[pl.ds(qs - span, 2 * BLOCK, stride=d), :]
        return jnp.concatenate([prev_ref[pl.ds(SUPER + qs - span, BLOCK, stride=d), :],
                                cur_ref[pl.ds(qs, BLOCK, stride=d), :]], axis=0)

    blocks = [[(pat, d, r, j) for r in range(d) for j in range(SUPER // (d * BLOCK))]
              for pat, (_, d) in enumerate(DILATED_PATTERNS)]
    for pat, d, r, j in [blk for group in zip(*reversed(blocks), strict=True) for blk in group]:
        qs = r + d * BLOCK * j
        rows = pl.ds(qs, BLOCK, stride=d)
        qb = q_ref[rows, :]
        q2 = jnp.concatenate([jnp.where(low, qb, 0.0), jnp.where(low, 0.0, qb)], axis=0).astype(BF16)
        kb = keys(kp_ref, kc_ref, qs, d).astype(BF16)
        vb = keys(vp_ref, vc_ref, qs, d).astype(BF16)
        km = kmask[first_super] if j == 0 else kmask[0]
        s = lax.dot_general(jnp.concatenate([q2, qsel[...]], axis=1),
                            jnp.concatenate([kb, km], axis=1), nt, preferred_element_type=F32)
        m = jnp.max(s, axis=1, keepdims=True)
        p = jnp.exp2(s - m)
        ol = jnp.dot(p.astype(BF16), jnp.concatenate([vb, ones], axis=1),
                     preferred_element_type=F32)
        acc[pat, rows, :] = jnp.where(low, ol[:BLOCK, :LANES], ol[BLOCK:, :LANES])
        mx[pat, rows, :] = jnp.where(low, m[:BLOCK], m[BLOCK:])
        sm[pat, rows, :] = jnp.where(low, ol[:BLOCK, LANES:], ol[BLOCK:, LANES:])

    m_all = jnp.maximum(jnp.maximum(mx[0], mx[1]), mx[2])
    num = jnp.zeros((SUPER, LANES), F32)
    den = jnp.zeros((SUPER, LANES), F32)
    for pat in range(len(DILATED_PATTERNS)):
        w = jnp.exp2(mx[pat] - m_all)
        num = num + w * acc[pat]
        den = den + w * sm[pat]
    o_ref[...] = (num / den).astype(BF16) * sg_ref[...]


def _attend_prompt(q, k, v, sg):
    b, s, _ = q.shape
    n_super = s // SUPER
    cur = pl.BlockSpec((None, SUPER, LANES), lambda bi, sb, hp: (bi, sb, hp))
    prev = pl.BlockSpec((None, SUPER, LANES), lambda bi, sb, hp: (bi, jnp.maximum(sb - 1, 0), hp))
    n_pat = len(DILATED_PATTERNS)
    return pl.pallas_call(
        _attn_kernel,
        grid=(b, n_super, N_HEADS // HEAD_PAIR),
        in_specs=[cur, prev, cur, prev, cur, cur],
        out_specs=cur,
        out_shape=jax.ShapeDtypeStruct((b, s, ATTN_WIDTH), BF16),
        scratch_shapes=[pltpu.VMEM((n_pat, SUPER, LANES), F32),
                        pltpu.VMEM((n_pat, SUPER, LANES), F32),
                        pltpu.VMEM((n_pat, SUPER, LANES), F32),
                        pltpu.VMEM((HEAD_PAIR * BLOCK, BLOCK), BF16),
                        pltpu.VMEM((2, 2 * BLOCK, BLOCK), BF16)],
        compiler_params=pltpu.CompilerParams(dimension_semantics=("arbitrary",) * 3,
                                             vmem_limit_bytes=VMEM_LIMIT),
        name="attn_prompt",
    )(q, k, k, v, v, sg)


def _merge_kernel(x_ref, a_ref, u_ref, uh_ref, sgp_ref, wp_ref, ps_ref, wo_ref, fg_ref,
                  pc_ref, pn_ref, vn_ref, cv_ref, y_ref, os_ref, ov_ref, ue, *, n_new):
    i = pl.program_id(1)
    rows = u_ref.shape[1]
    ue[0:HIST_ROWS] = jnp.where(i == 0, 0.0, uh_ref[0])
    ue[HIST_ROWS:HIST_ROWS + rows] = u_ref[0]
    pos = i * rows + lax.broadcasted_iota(jnp.int32, (rows, 1), 0)
    counts = [jnp.minimum(pos + 1, w).astype(F32) for w in POOL_WINDOWS]
    pooled = _pooled(lambda g, j: ue[pl.ds(HIST_ROWS - j, rows), _group_cols(g)], counts)
    pool = _pool_project(lambda g: pooled[g].astype(BF16), wp_ref, ps_ref[...])
    mix = jnp.concatenate([a_ref[0], (pool * sgp_ref[0].astype(F32)).astype(BF16)], axis=-1)
    h = x_ref[0] + jnp.dot(mix, wo_ref[...], preferred_element_type=F32)
    y_ref[0] = _rms(h, fg_ref[...])
    _sample_values(pc_ref, pn_ref, vn_ref, cv_ref, os_ref, ov_ref, n_new)


def _merge_prompt_value_sample(x, mix_a, u, sgp, wp_bf, pool_scale, wo_bf, final_g, pc, pn, vnt, cv, n_new):
    b, s, _ = x.shape
    n_batch, h, dh, buf = cv.shape
    q_rows = pc.shape[2]
    tiles = s // MERGE_ROWS
    hosted = n_batch * h // (b * tiles)
    assert hosted * b * tiles == n_batch * h and h % hosted == 0, "sample heads split evenly over the grid"
    per_tile = MERGE_ROWS // HIST_ROWS
    tok = lambda width: pl.BlockSpec((1, MERGE_ROWS, width), lambda bi, i: (bi, i, 0))
    const = lambda shape: pl.BlockSpec(shape, lambda bi, i: (0,) * len(shape))
    heads = _hosted_heads((b, tiles), h // hosted, hosted)
    return pl.pallas_call(
        functools.partial(_merge_kernel, n_new=n_new),
        grid=(b, tiles),
        in_specs=[tok(D_MODEL), tok(ATTN_WIDTH), tok(POOL_WIDTH),
                  pl.BlockSpec((1, HIST_ROWS, POOL_WIDTH),
                               lambda bi, i: (bi, jnp.maximum(i * per_tile - 1, 0), 0)),
                  tok(POOL_WIDTH),
                  const((len(POOL_WINDOWS), POOL_GROUP_WIDTH, POOL_GROUP_WIDTH)),
                  const((1, POOL_WIDTH)),
                  pl.BlockSpec((D_MODEL, D_MODEL), lambda bi, i: (0, 0), pipeline_mode=pl.Buffered(1)),
                  const((1, D_MODEL)),
                  heads(q_rows, buf), heads(q_rows, LANES), heads(dh, LANES), heads(dh, buf)],
        out_specs=[tok(D_MODEL), heads(q_rows, dh), heads(dh, buf)],
        out_shape=[jax.ShapeDtypeStruct((b, s, D_MODEL), F32),
                   jax.ShapeDtypeStruct((n_batch, h, q_rows, dh), F32),
                   jax.ShapeDtypeStruct((n_batch, h, dh, buf), F32)],
        scratch_shapes=[pltpu.VMEM((HIST_ROWS + MERGE_ROWS, POOL_WIDTH), F32)],
        compiler_params=pltpu.CompilerParams(dimension_semantics=("arbitrary", "arbitrary"),
                                             vmem_limit_bytes=VMEM_LIMIT),
        name="merge_prompt_value_sample",
    )(x, mix_a, u, u, sgp, wp_bf, pool_scale, wo_bf, final_g, pc, pn, vnt, cv)


def _merge_sample_kernel(x_ref, o_ref, z_ref, hist_ref, wp_ref, ps_ref, wo_ref, fg_ref, y_ref, pool_ref, ue,
                         *, n_new, n_batch):
    u_col = 4 * ATTN_WIDTH
    ue[0:POOL_HIST] = hist_ref[...]
    ue[POOL_HIST:POOL_HIST + n_new] = z_ref[:, u_col:u_col + POOL_WIDTH].reshape(n_new, n_batch, POOL_WIDTH)
    pool_ref[...] = ue[n_new:n_new + POOL_HIST]

    rows = n_new * n_batch

    def window_sum(g, j):
        return ue[pl.ds(POOL_HIST - j, n_new), :, _group_cols(g)].reshape(rows, POOL_GROUP_WIDTH)

    pos = PAST_LEN + lax.broadcasted_iota(jnp.int32, (rows, 1), 0) // n_batch
    counts = [jnp.minimum(pos + 1, w).astype(F32) for w in POOL_WINDOWS]
    pooled = _pooled(window_sum, counts)
    pool = _pool_project(lambda g: pooled[g].astype(BF16), wp_ref, ps_ref[...])
    ga = z_ref[:, 3 * ATTN_WIDTH:4 * ATTN_WIDTH]
    gp = z_ref[:, u_col + POOL_WIDTH:u_col + 2 * POOL_WIDTH]
    mix = jnp.concatenate([o_ref[...] * _silu(ga), pool * _silu(gp)], axis=-1).astype(BF16)
    h = x_ref[...] + jnp.dot(mix, wo_ref[...], preferred_element_type=F32)
    y_ref[...] = _rms(h, fg_ref[...])


def _merge_sample(x_tb, o_tb, z_tb, hist, wp_bf, pool_scale, wo_bf, final_g, n_new, n_batch):
    return pl.pallas_call(
        functools.partial(_merge_sample_kernel, n_new=n_new, n_batch=n_batch),
        out_shape=[jax.ShapeDtypeStruct(x_tb.shape, F32),
                   jax.ShapeDtypeStruct((POOL_HIST, n_batch, POOL_WIDTH), F32)],
        scratch_shapes=[pltpu.VMEM((POOL_HIST + n_new, n_batch, POOL_WIDTH), F32)],
        compiler_params=pltpu.CompilerParams(vmem_limit_bytes=VMEM_LIMIT),
        name="merge_sample",
    )(x_tb, o_tb, z_tb, hist, wp_bf, pool_scale, wo_bf, final_g)


def _to_feature_major(z_cols, n_new, n_batch):
    t = z_cols.reshape(n_new, n_batch, N_HEADS, HEAD_DIM).transpose(1, 2, 3, 0)
    return jnp.pad(t, ((0, 0), (0, 0), (0, 0), (0, LANES - n_new)))


def kernel(x_prompt, x_sample, cache_k, cache_v, state_pool, norm_g, w_in, w_pool, pool_scale, w_out, final_norm_g):
    depth = norm_g.shape[0]
    assert depth == 1, "single-layer step"
    b, s, _ = x_prompt.shape
    n_batch, n_new, _ = x_sample.shape
    buf = cache_k.shape[2]
    assert s % SUPER == 0 and s >= MAX_WINDOW and buf == MAX_WINDOW and n_new <= POOL_HIST

    g = norm_g[0][None, :]
    fg = final_norm_g[None, :]
    wo_bf = w_out[0].astype(BF16)
    wp_bf = w_pool[0].astype(BF16)
    ps = pool_scale[0][None, :]

    x_tb = x_sample.transpose(1, 0, 2).reshape(n_new * n_batch, D_MODEL)
    z_tb, w_bf = _project_sample(x_tb, g, w_in[0])
    q_s = z_tb[:, 0:ATTN_WIDTH].reshape(n_new, n_batch, N_HEADS, HEAD_DIM).transpose(1, 2, 0, 3)
    q_s = jnp.pad(q_s, ((0, 0), (0, 0), (0, SUBLANES - n_new), (0, 0)))
    knt = _to_feature_major(z_tb[:, ATTN_WIDTH:2 * ATTN_WIDTH], n_new, n_batch)
    vnt = _to_feature_major(z_tb[:, 2 * ATTN_WIDTH:3 * ATTN_WIDTH], n_new, n_batch)
    ck = cache_k[0].transpose(0, 2, 3, 1)
    cv = cache_v[0].transpose(0, 2, 3, 1)

    q, k, v, sg, u, sgp, kt, vt, pc, pn, ok = _project_prompt_score_sample(x_prompt, g, w_bf, q_s, knt, ck, n_new)
    mix_a = _attend_prompt(q, k, v, sg)
    y_prompt, o_s, ov = _merge_prompt_value_sample(x_prompt, mix_a, u, sgp, wp_bf, ps, wo_bf, fg,
                                                   pc, pn, vnt, cv, n_new)
    to_window = lambda t: t.reshape(b, N_HEADS, HEAD_DIM, MAX_WINDOW).transpose(0, 3, 1, 2)[None]
    k_prompt, v_prompt = to_window(kt), to_window(vt)
    pool_prompt = u[:, s - POOL_HIST:][None]

    o_tb = o_s[:, :, :n_new].transpose(2, 0, 1, 3).reshape(n_new * n_batch, ATTN_WIDTH)
    hist = state_pool[0].transpose(1, 0, 2)
    y_tb, pool_tb = _merge_sample(x_tb, o_tb, z_tb, hist, wp_bf, ps, wo_bf, fg, n_new, n_batch)
    y_sample = y_tb.reshape(n_new, n_batch, D_MODEL).transpose(1, 0, 2)
    k_sample = ok.transpose(0, 3, 1, 2)[None]
    v_sample = ov.transpose(0, 3, 1, 2)[None]
    pool_sample = pool_tb.transpose(1, 0, 2)[None]

    return (y_prompt, y_sample, k_prompt, v_prompt, pool_prompt, k_sample, v_sample, pool_sample)
```

```python
import functools

import jax
import jax.numpy as jnp
from jax import lax
from jax.experimental import pallas as pl
from jax.experimental.pallas import tpu as pltpu

F32 = jnp.float32
BF16 = jnp.bfloat16

D_MODEL = 2048
ATTN_WIDTH = 1024
HEAD_DIM = 64
N_HEADS = 16
POOL_WIDTH = 1024
POOL_WINDOWS = (2, 4, 8, 16)
POOL_GROUP_WIDTH = 256
POOL_HIST = 15
IN_WIDTH = 6144
DILATED_PATTERNS = ((128, 1), (512, 4), (2048, 16))
MAX_WINDOW = 2048
BLOCK = 128
PAST_LEN = 16384
EPS = 1e-6
ATTN_SCALE = HEAD_DIM ** -0.5
NEG = -1e30
LOG2E = 1.4426950408889634

LANES = 128
SUBLANES = 8
HEAD_PAIR = LANES // HEAD_DIM
HIST_ROWS = 16

PROJ_ROWS = 256
MERGE_ROWS = 256
SUPER = 2048
CAST_COLS = 1024
VMEM_LIMIT = 58 * 1024 * 1024


def _silu(x):
    return x / (1.0 + jnp.exp(-x))


def _rms(x, g):
    ms = jnp.mean(x * x, axis=-1, keepdims=True)
    return x * lax.rsqrt(ms + EPS) * g


def _group_cols(g):
    return slice(g * POOL_GROUP_WIDTH, (g + 1) * POOL_GROUP_WIDTH)


def _pooled(window_sum, counts):
    outs = []
    for g, w in enumerate(POOL_WINDOWS):
        cur = window_sum(g, 0)
        tot = cur
        for j in range(1, w):
            tot = tot + window_sum(g, j)
        outs.append(tot / counts[g] - cur)
    return outs


def _pool_project(pooled_group, wp_ref, scale):
    outs = [jnp.dot(pooled_group(g), wp_ref[g], preferred_element_type=F32) for g in range(len(POOL_WINDOWS))]
    return jnp.concatenate(outs, axis=-1) * scale


def _multiplicity(delta):
    c = jnp.zeros(delta.shape, F32)
    for window, d in DILATED_PATTERNS:
        assert d & (d - 1) == 0, "power-of-two dilation: residue test is a bit mask"
        c = c + jnp.where(((delta & (d - 1)) == 0) & (delta <= window), 1.0, 0.0)
    return c


def _shift_cache(src, new, dst_ref, hh, n_new):
    buf = src.shape[-1]
    tail_lane = lax.broadcasted_iota(jnp.int32, (HEAD_DIM, LANES), 1)
    rolled = pltpu.roll(src, buf - n_new, 1)
    tail = rolled[:, buf - LANES:]
    placed = pltpu.roll(new, LANES - n_new, 1)
    dst_ref[0, hh, :, 0:buf - LANES] = rolled[:, 0:buf - LANES]
    dst_ref[0, hh, :, buf - LANES:buf] = jnp.where(tail_lane >= LANES - n_new, placed, tail)


def _sample_scores(q_ref, kn_ref, ck_ref, pc_ref, pn_ref, ok_ref, n_new):
    buf = ck_ref.shape[-1]
    q_rows = q_ref.shape[2]
    qi = lax.broadcasted_iota(jnp.int32, (q_rows, buf), 0)
    cc = _multiplicity(buf + qi - lax.broadcasted_iota(jnp.int32, (q_rows, buf), 1))
    qn = lax.broadcasted_iota(jnp.int32, (q_rows, LANES), 0)
    nl = lax.broadcasted_iota(jnp.int32, (q_rows, LANES), 1)
    cn = jnp.where(nl <= qn, _multiplicity(qn - jnp.minimum(nl, qn)), 0.0)

    for hh in range(ck_ref.shape[1]):
        kc = ck_ref[0, hh]
        knt = kn_ref[0, hh]
        qh = (q_ref[0, hh] * ATTN_SCALE).astype(BF16)
        sc = jnp.dot(qh, kc.astype(BF16), preferred_element_type=F32)
        sn = jnp.dot(qh, knt.astype(BF16), preferred_element_type=F32)
        sc = jnp.where(cc > 0, sc, NEG)
        sn = jnp.where(cn > 0, sn, NEG)
        m = jnp.maximum(jnp.max(sc, axis=1, keepdims=True), jnp.max(sn, axis=1, keepdims=True))
        pc = cc * jnp.exp(sc - m)
        pn = cn * jnp.exp(sn - m)
        den = jnp.sum(pc, axis=1, keepdims=True) + jnp.sum(pn, axis=1, keepdims=True)
        pc_ref[0, hh] = pc / den
        pn_ref[0, hh] = pn / den
        _shift_cache(kc, knt, ok_ref, hh, n_new)


def _sample_values(pc_ref, pn_ref, vn_ref, cv_ref, o_ref, ov_ref, n_new):
    nt = (((1,), (1,)), ((), ()))
    for hh in range(cv_ref.shape[1]):
        vc = cv_ref[0, hh]
        vnt = vn_ref[0, hh]
        o_ref[0, hh] = (
            lax.dot_general(pc_ref[0, hh].astype(BF16), vc.astype(BF16), nt, preferred_element_type=F32)
            + lax.dot_general(pn_ref[0, hh].astype(BF16), vnt.astype(BF16), nt, preferred_element_type=F32))
        _shift_cache(vc, vnt, ov_ref, hh, n_new)


def _hosted_heads(grid, head_groups, heads_per_step):
    def spec(*tail):
        def index(*ids):
            step = 0
            for size, idx in zip(grid, ids, strict=True):
                step = step * size + idx
            return (step // head_groups, step % head_groups, 0, 0)
        return pl.BlockSpec((1, heads_per_step) + tail, index)
    return spec


def _proj_sample_kernel(x_ref, g_ref, w_ref, z_ref, wbf_ref):
    xn = _rms(x_ref[...], g_ref[...]).astype(BF16)
    w = w_ref[...].astype(BF16)
    wbf_ref[...] = w
    z_ref[...] = jnp.dot(xn, w, preferred_element_type=F32)


def _project_sample(x_tb, g, w):
    rows = x_tb.shape[0]
    cols = pl.BlockSpec((D_MODEL, CAST_COLS), lambda c: (0, c))
    return pl.pallas_call(
        _proj_sample_kernel,
        grid=(IN_WIDTH // CAST_COLS,),
        in_specs=[pl.BlockSpec((rows, D_MODEL), lambda c: (0, 0)),
                  pl.BlockSpec((1, D_MODEL), lambda c: (0, 0)),
                  cols],
        out_specs=[pl.BlockSpec((rows, CAST_COLS), lambda c: (0, c)), cols],
        out_shape=[jax.ShapeDtypeStruct((rows, IN_WIDTH), F32),
                   jax.ShapeDtypeStruct((D_MODEL, IN_WIDTH), BF16)],
        compiler_params=pltpu.CompilerParams(dimension_semantics=("arbitrary",),
                                             vmem_limit_bytes=VMEM_LIMIT),
        name="proj_sample",
    )(x_tb, g, w)


def _proj_kernel(x_ref, g_ref, w_ref, qs_ref, kn_ref, ck_ref,
                 q_ref, k_ref, v_ref, sg_ref, u_ref, sgp_ref, kt_ref, vt_ref, pc_ref, pn_ref, ok_ref, *, n_new):
    _sample_scores(qs_ref, kn_ref, ck_ref, pc_ref, pn_ref, ok_ref, n_new)

    xn = _rms(x_ref[0], g_ref[...]).astype(BF16)

    def part(c):
        return jnp.dot(xn, w_ref[:, c * ATTN_WIDTH:(c + 1) * ATTN_WIDTH], preferred_element_type=F32)

    k = part(1)
    k_ref[0] = k
    kt_ref[0] = k.T
    sg_ref[0] = _silu(part(3)).astype(BF16)
    v = part(2)
    v_ref[0] = v
    vt_ref[0] = v.T
    sgp_ref[0] = _silu(part(5)).astype(BF16)
    u_ref[0] = part(4)
    q_ref[0] = part(0) * (ATTN_SCALE * LOG2E)


def _project_prompt_score_sample(x, g, w_bf, q_s, knt, ck, n_new):
    b, s, _ = x.shape
    n_batch, h, dh, buf = ck.shape
    q_rows = q_s.shape[2]
    tiles = s // PROJ_ROWS
    hosted = n_batch * h // (b * tiles)
    assert hosted * b * tiles == n_batch * h and h % hosted == 0, "sample heads split evenly over the grid"
    first_kept = (s - MAX_WINDOW) // PROJ_ROWS
    tok = lambda width: pl.BlockSpec((1, PROJ_ROWS, width), lambda bi, i: (bi, i, 0))
    feat = pl.BlockSpec((1, ATTN_WIDTH, PROJ_ROWS), lambda bi, i: (bi, 0, jnp.maximum(i - first_kept, 0)))
    heads = _hosted_heads((b, tiles), h // hosted, hosted)
    f32 = lambda *shape: jax.ShapeDtypeStruct(shape, F32)
    bf16 = lambda *shape: jax.ShapeDtypeStruct(shape, BF16)
    return pl.pallas_call(
        functools.partial(_proj_kernel, n_new=n_new),
        grid=(b, tiles),
        in_specs=[tok(D_MODEL),
                  pl.BlockSpec((1, D_MODEL), lambda bi, i: (0, 0)),
                  pl.BlockSpec((D_MODEL, IN_WIDTH), lambda bi, i: (0, 0), pipeline_mode=pl.Buffered(1)),
                  heads(q_rows, dh), heads(dh, LANES), heads(dh, buf)],
        out_specs=[tok(ATTN_WIDTH)] * 6 + [feat, feat,
                   heads(q_rows, buf), heads(q_rows, LANES), heads(dh, buf)],
        out_shape=[f32(b, s, ATTN_WIDTH)] * 3
        + [bf16(b, s, ATTN_WIDTH), f32(b, s, POOL_WIDTH), bf16(b, s, POOL_WIDTH)]
        + [f32(b, ATTN_WIDTH, MAX_WINDOW)] * 2
        + [f32(n_batch, h, q_rows, buf), f32(n_batch, h, q_rows, LANES), f32(n_batch, h, dh, buf)],
        compiler_params=pltpu.CompilerParams(dimension_semantics=("arbitrary", "arbitrary"),
                                             vmem_limit_bytes=VMEM_LIMIT),
        name="proj_prompt_score_sample",
    )(x, g, w_bf, q_s, knt, ck)


def _attn_kernel(q_ref, kp_ref, kc_ref, vp_ref, vc_ref, sg_ref, o_ref, acc, mx, sm, qsel, kmask):
    first_super = (pl.program_id(1) == 0).astype(jnp.int32)

    kk = lax.broadcasted_iota(jnp.int32, (2 * BLOCK, BLOCK), 0)
    qq = lax.broadcasted_iota(jnp.int32, (2 * BLOCK, BLOCK), 1)
    band = (kk - qq >= 0) & (kk - qq <= BLOCK)
    kmask[0] = jnp.where(band, 0.0, NEG).astype(BF16)
    kmask[1] = jnp.where(band & (kk >= BLOCK), 0.0, NEG).astype(BF16)
    row = lax.broadcasted_iota(jnp.int32, (HEAD_PAIR * BLOCK, BLOCK), 0) % BLOCK
    col = lax.broadcasted_iota(jnp.int32, (HEAD_PAIR * BLOCK, BLOCK), 1)
    qsel[...] = jnp.where(row == col, 1.0, 0.0).astype(BF16)

    lane = lax.broadcasted_iota(jnp.int32, (BLOCK, LANES), 1)
    low = lane < HEAD_DIM
    ones = jnp.ones((2 * BLOCK, LANES), BF16)
    nt = (((1,), (1,)), ((), ()))

    def keys(prev_ref, cur_ref, qs, d):
        span = d * BLOCK
        if qs >= span:
            return cur_ref[pl.ds(qs - span, 2 * BLOCK, stride=d), :]
        return jnp.concatenate([prev_ref[pl.ds(SUPER + qs - span, BLOCK, stride=d), :],
                                cur_ref[pl.ds(qs, BLOCK, stride=d), :]], axis=0)

    blocks = [[(pat, d, r, j) for r in range(d) for j in range(SUPER // (d * BLOCK))]
              for pat, (_, d) in enumerate(DILATED_PATTERNS)]
    for pat, d, r, j in [blk for group in zip(*reversed(blocks), strict=True) for blk in group]:
        qs = r + d * BLOCK * j
        rows = pl.ds(qs, BLOCK, stride=d)
        qb = q_ref[rows, :]
        q2 = jnp.concatenate([jnp.where(low, qb, 0.0), jnp.where(low, 0.0, qb)], axis=0).astype(BF16)
        kb = keys(kp_ref, kc_ref, qs, d).astype(BF16)
        vb = keys(vp_ref, vc_ref, qs, d).astype(BF16)
        km = kmask[first_super] if j == 0 else kmask[0]
        s = lax.dot_general(jnp.concatenate([q2, qsel[...]], axis=1),
                            jnp.concatenate([kb, km], axis=1), nt, preferred_element_type=F32)
        m = jnp.max(s, axis=1, keepdims=True)
        p = jnp.exp2(s - m)
        ol = jnp.dot(p.astype(BF16), jnp.concatenate([vb, ones], axis=1),
                     preferred_element_type=F32)
        acc[pat, rows, :] = jnp.where(low, ol[:BLOCK, :LANES], ol[BLOCK:, :LANES])
        mx[pat, rows, :] = jnp.where(low, m[:BLOCK], m[BLOCK:])
        sm[pat, rows, :] = jnp.where(low, ol[:BLOCK, LANES:], ol[BLOCK:, LANES:])

    m_all = jnp.maximum(jnp.maximum(mx[0], mx[1]), mx[2])
    num = jnp.zeros((SUPER, LANES), F32)
    den = jnp.zeros((SUPER, LANES), F32)
    for pat in range(len(DILATED_PATTERNS)):
        w = jnp.exp2(mx[pat] - m_all)
        num = num + w * acc[pat]
        den = den + w * sm[pat]
    o_ref[...] = (num / den).astype(BF16) * sg_ref[...]


def _attend_prompt(q, k, v, sg):
    b, s, _ = q.shape
    n_super = s // SUPER
    cur = pl.BlockSpec((None, SUPER, LANES), lambda bi, sb, hp: (bi, sb, hp))
    prev = pl.BlockSpec((None, SUPER, LANES), lambda bi, sb, hp: (bi, jnp.maximum(sb - 1, 0), hp))
    n_pat = len(DILATED_PATTERNS)
    return pl.pallas_call(
        _attn_kernel,
        grid=(b, n_super, N_HEADS // HEAD_PAIR),
        in_specs=[cur, prev, cur, prev, cur, cur],
        out_specs=cur,
        out_shape=jax.ShapeDtypeStruct((b, s, ATTN_WIDTH), BF16),
        scratch_shapes=[pltpu.VMEM((n_pat, SUPER, LANES), F32),
                        pltpu.VMEM((n_pat, SUPER, LANES), F32),
                        pltpu.VMEM((n_pat, SUPER, LANES), F32),
                        pltpu.VMEM((HEAD_PAIR * BLOCK, BLOCK), BF16),
                        pltpu.VMEM((2, 2 * BLOCK, BLOCK), BF16)],
        compiler_params=pltpu.CompilerParams(dimension_semantics=("arbitrary",) * 3,
                                             vmem_limit_bytes=VMEM_LIMIT),
        name="attn_prompt",
    )(q, k, k, v, v, sg)


def _merge_kernel(x_ref, a_ref, u_ref, uh_ref, sgp_ref, wp_ref, ps_ref, wo_ref, fg_ref,
                  pc_ref, pn_ref, vn_ref, cv_ref, y_ref, os_ref, ov_ref, ue, *, n_new):
    i = pl.program_id(1)
    rows = u_ref.shape[1]
    ue[0:HIST_ROWS] = jnp.where(i == 0, 0.0, uh_ref[0])
    ue[HIST_ROWS:HIST_ROWS + rows] = u_ref[0]
    pos = i * rows + lax.broadcasted_iota(jnp.int32, (rows, 1), 0)
    counts = [jnp.minimum(pos + 1, w).astype(F32) for w in POOL_WINDOWS]
    pooled = _pooled(lambda g, j: ue[pl.ds(HIST_ROWS - j, rows), _group_cols(g)], counts)
    pool = _pool_project(lambda g: pooled[g].astype(BF16), wp_ref, ps_ref[...])
    mix = jnp.concatenate([a_ref[0], (pool * sgp_ref[0].astype(F32)).astype(BF16)], axis=-1)
    h = x_ref[0] + jnp.dot(mix, wo_ref[...], preferred_element_type=F32)
    y_ref[0] = _rms(h, fg_ref[...])
    _sample_values(pc_ref, pn_ref, vn_ref, cv_ref, os_ref, ov_ref, n_new)


def _merge_prompt_value_sample(x, mix_a, u, sgp, wp_bf, pool_scale, wo_bf, final_g, pc, pn, vnt, cv, n_new):
    b, s, _ = x.shape
    n_batch, h, dh, buf = cv.shape
    q_rows = pc.shape[2]
    tiles = s // MERGE_ROWS
    hosted = n_batch * h // (b * tiles)
    assert hosted * b * tiles == n_batch * h and h % hosted == 0, "sample heads split evenly over the grid"
    per_tile = MERGE_ROWS // HIST_ROWS
    tok = lambda width: pl.BlockSpec((1, MERGE_ROWS, width), lambda bi, i: (bi, i, 0))
    const = lambda shape: pl.BlockSpec(shape, lambda bi, i: (0,) * len(shape))
    heads = _hosted_heads((b, tiles), h // hosted, hosted)
    return pl.pallas_call(
        functools.partial(_merge_kernel, n_new=n_new),
        grid=(b, tiles),
        in_specs=[tok(D_MODEL), tok(ATTN_WIDTH), tok(POOL_WIDTH),
                  pl.BlockSpec((1, HIST_ROWS, POOL_WIDTH),
                               lambda bi, i: (bi, jnp.maximum(i * per_tile - 1, 0), 0)),
                  tok(POOL_WIDTH),
                  const((len(POOL_WINDOWS), POOL_GROUP_WIDTH, POOL_GROUP_WIDTH)),
                  const((1, POOL_WIDTH)),
                  pl.BlockSpec((D_MODEL, D_MODEL), lambda bi, i: (0, 0), pipeline_mode=pl.Buffered(1)),
                  const((1, D_MODEL)),
                  heads(q_rows, buf), heads(q_rows, LANES), heads(dh, LANES), heads(dh, buf)],
        out_specs=[tok(D_MODEL), heads(q_rows, dh), heads(dh, buf)],
        out_shape=[jax.ShapeDtypeStruct((b, s, D_MODEL), F32),
                   jax.ShapeDtypeStruct((n_batch, h, q_rows, dh), F32),
                   jax.ShapeDtypeStruct((n_batch, h, dh, buf), F32)],
        scratch_shapes=[pltpu.VMEM((HIST_ROWS + MERGE_ROWS, POOL_WIDTH), F32)],
        compiler_params=pltpu.CompilerParams(dimension_semantics=("arbitrary", "arbitrary"),
                                             vmem_limit_bytes=VMEM_LIMIT),
        name="merge_prompt_value_sample",
    )(x, mix_a, u, u, sgp, wp_bf, pool_scale, wo_bf, final_g, pc, pn, vnt, cv)


def _merge_sample_kernel(x_ref, o_ref, z_ref, hist_ref, wp_ref, ps_ref, wo_ref, fg_ref, y_ref, pool_ref, ue,
                         *, n_new, n_batch):
    u_col = 4 * ATTN_WIDTH
    ue[0:POOL_HIST] = hist_ref[...]
    ue[POOL_HIST:POOL_HIST + n_new] = z_ref[:, u_col:u_col + POOL_WIDTH].reshape(n_new, n_batch, POOL_WIDTH)
    pool_ref[...] = ue[n_new:n_new + POOL_HIST]

    rows = n_new * n_batch

    def window_sum(g, j):
        return ue[pl.ds(POOL_HIST - j, n_new), :, _group_cols(g)].reshape(rows, POOL_GROUP_WIDTH)

    pos = PAST_LEN + lax.broadcasted_iota(jnp.int32, (rows, 1), 0) // n_batch
    counts = [jnp.minimum(pos + 1, w).astype(F32) for w in POOL_WINDOWS]
    pooled = _pooled(window_sum, counts)
    pool = _pool_project(lambda g: pooled[g].astype(BF16), wp_ref, ps_ref[...])
    ga = z_ref[:, 3 * ATTN_WIDTH:4 * ATTN_WIDTH]
    gp = z_ref[:, u_col + POOL_WIDTH:u_col + 2 * POOL_WIDTH]
    mix = jnp.concatenate([o_ref[...] * _silu(ga), pool * _silu(gp)], axis=-1).astype(BF16)
    h = x_ref[...] + jnp.dot(mix, wo_ref[...], preferred_element_type=F32)
    y_ref[...] = _rms(h, fg_ref[...])


def _merge_sample(x_tb, o_tb, z_tb, hist, wp_bf, pool_scale, wo_bf, final_g, n_new, n_batch):
    return pl.pallas_call(
        functools.partial(_merge_sample_kernel, n_new=n_new, n_batch=n_batch),
        out_shape=[jax.ShapeDtypeStruct(x_tb.shape, F32),
                   jax.ShapeDtypeStruct((POOL_HIST, n_batch, POOL_WIDTH), F32)],
        scratch_shapes=[pltpu.VMEM((POOL_HIST + n_new, n_batch, POOL_WIDTH), F32)],
        compiler_params=pltpu.CompilerParams(vmem_limit_bytes=VMEM_LIMIT),
        name="merge_sample",
    )(x_tb, o_tb, z_tb, hist, wp_bf, pool_scale, wo_bf, final_g)


def _to_feature_major(z_cols, n_new, n_batch):
    t = z_cols.reshape(n_new, n_batch, N_HEADS, HEAD_DIM).transpose(1, 2, 3, 0)
    return jnp.pad(t, ((0, 0), (0, 0), (0, 0), (0, LANES - n_new)))


def kernel(x_prompt, x_sample, cache_k, cache_v, state_pool, norm_g, w_in, w_pool, pool_scale, w_out, final_norm_g):
    depth = norm_g.shape[0]
    assert depth == 1, "single-layer step"
    b, s, _ = x_prompt.shape
    n_batch, n_new, _ = x_sample.shape
    buf = cache_k.shape[2]
    assert s % SUPER == 0 and s >= MAX_WINDOW and buf == MAX_WINDOW and n_new <= POOL_HIST

    g = norm_g[0][None, :]
    fg = final_norm_g[None, :]
    wo_bf = w_out[0].astype(BF16)
    wp_bf = w_pool[0].astype(BF16)
    ps = pool_scale[0][None, :]

    x_tb = x_sample.transpose(1, 0, 2).reshape(n_new * n_batch, D_MODEL)
    z_tb, w_bf = _project_sample(x_tb, g, w_in[0])
    q_s = z_tb[:, 0:ATTN_WIDTH].reshape(n_new, n_batch, N_HEADS, HEAD_DIM).transpose(1, 2, 0, 3)
    q_s = jnp.pad(q_s, ((0, 0), (0, 0), (0, SUBLANES - n_new), (0, 0)))
    knt = _to_feature_major(z_tb[:, ATTN_WIDTH:2 * ATTN_WIDTH], n_new, n_batch)
    vnt = _to_feature_major(z_tb[:, 2 * ATTN_WIDTH:3 * ATTN_WIDTH], n_new, n_batch)
    ck = cache_k[0].transpose(0, 2, 3, 1)
    cv = cache_v[0].transpose(0, 2, 3, 1)

    q, k, v, sg, u, sgp, kt, vt, pc, pn, ok = _project_prompt_score_sample(x_prompt, g, w_bf, q_s, knt, ck, n_new)
    mix_a = _attend_prompt(q, k, v, sg)
    y_prompt, o_s, ov = _merge_prompt_value_sample(x_prompt, mix_a, u, sgp, wp_bf, ps, wo_bf, fg,
                                                   pc, pn, vnt, cv, n_new)
    to_window = lambda t: t.reshape(b, N_HEADS, HEAD_DIM, MAX_WINDOW).transpose(0, 3, 1, 2)[None]
    k_prompt, v_prompt = to_window(kt), to_window(vt)
    pool_prompt = u[:, s - POOL_HIST:][None]

    o_tb = o_s[:, :, :n_new].transpose(2, 0, 1, 3).reshape(n_new * n_batch, ATTN_WIDTH)
    hist = state_pool[0].transpose(1, 0, 2)
    y_tb, pool_tb = _merge_sample(x_tb, o_tb, z_tb, hist, wp_bf, ps, wo_bf, fg, n_new, n_batch)
    y_sample = y_tb.reshape(n_new, n_batch, D_MODEL).transpose(1, 0, 2)
    k_sample = ok.transpose(0, 3, 1, 2)[None]
    v_sample = ov.transpose(0, 3, 1, 2)[None]
    pool_sample = pool_tb.transpose(1, 0, 2)[None]

    return (y_prompt, y_sample, k_prompt, v_prompt, pool_prompt, k_sample, v_sample, pool_sample)
```

```python
import functools

import jax
import jax.numpy as jnp
from jax import lax
from jax.experimental import pallas as pl
from jax.experimental.pallas import tpu as pltpu

F32 = jnp.float32
BF16 = jnp.bfloat16

D_MODEL = 2048
ATTN_WIDTH = 1024
HEAD_DIM = 64
N_HEADS = 16
POOL_WIDTH = 1024
POOL_WINDOWS = (2, 4, 8, 16)
POOL_GROUP_WIDTH = 256
POOL_HIST = 15
IN_WIDTH = 6144
DILATED_PATTERNS = ((128, 1), (512, 4), (2048, 16))
MAX_WINDOW = 2048
BLOCK = 128
PAST_LEN = 16384
EPS = 1e-6
ATTN_SCALE = HEAD_DIM ** -0.5
NEG = -1e30
LOG2E = 1.4426950408889634

LANES = 128
SUBLANES = 8
HEAD_PAIR = LANES // HEAD_DIM
HIST_ROWS = 16

PROJ_ROWS = 256
MERGE_ROWS = 256
SUPER = 2048
CAST_COLS = 1024
VMEM_LIMIT = 58 * 1024 * 1024


def _silu(x):
    return x / (1.0 + jnp.exp(-x))


def _rms(x, g):
    ms = jnp.mean(x * x, axis=-1, keepdims=True)
    return x * lax.rsqrt(ms + EPS) * g


def _group_cols(g):
    return slice(g * POOL_GROUP_WIDTH, (g + 1) * POOL_GROUP_WIDTH)


def _pooled(window_sum, counts):
    outs = []
    for g, w in enumerate(POOL_WINDOWS):
        cur = window_sum(g, 0)
        tot = cur
        for j in range(1, w):
            tot = tot + window_sum(g, j)
        outs.append(tot / counts[g] - cur)
    return outs


def _pool_project(pooled_group, wp_ref, scale):
    outs = [jnp.dot(pooled_group(g), wp_ref[g], preferred_element_type=F32) for g in range(len(POOL_WINDOWS))]
    return jnp.concatenate(outs, axis=-1) * scale


def _multiplicity(delta):
    c = jnp.zeros(delta.shape, F32)
    for window, d in DILATED_PATTERNS:
        assert d & (d - 1) == 0, "power-of-two dilation: residue test is a bit mask"
        c = c + jnp.where(((delta & (d - 1)) == 0) & (delta <= window), 1.0, 0.0)
    return c


def _shift_cache(src, new, dst_ref, hh, n_new):
    buf = src.shape[-1]
    tail_lane = lax.broadcasted_iota(jnp.int32, (HEAD_DIM, LANES), 1)
    rolled = pltpu.roll(src, buf - n_new, 1)
    tail = rolled[:, buf - LANES:]
    placed = pltpu.roll(new, LANES - n_new, 1)
    dst_ref[0, hh, :, 0:buf - LANES] = rolled[:, 0:buf - LANES]
    dst_ref[0, hh, :, buf - LANES:buf] = jnp.where(tail_lane >= LANES - n_new, placed, tail)


def _sample_scores(q_ref, kn_ref, ck_ref, pc_ref, pn_ref, ok_ref, n_new):
    buf = ck_ref.shape[-1]
    q_rows = q_ref.shape[2]
    qi = lax.broadcasted_iota(jnp.int32, (q_rows, buf), 0)
    cc = _multiplicity(buf + qi - lax.broadcasted_iota(jnp.int32, (q_rows, buf), 1))
    qn = lax.broadcasted_iota(jnp.int32, (q_rows, LANES), 0)
    nl = lax.broadcasted_iota(jnp.int32, (q_rows, LANES), 1)
    cn = jnp.where(nl <= qn, _multiplicity(qn - jnp.minimum(nl, qn)), 0.0)

    for hh in range(ck_ref.shape[1]):
        kc = ck_ref[0, hh]
        knt = kn_ref[0, hh]
        qh = (q_ref[0, hh] * ATTN_SCALE).astype(BF16)
        sc = jnp.dot(qh, kc.astype(BF16), preferred_element_type=F32)
        sn = jnp.dot(qh, knt.astype(BF16), preferred_element_type=F32)
        sc = jnp.where(cc > 0, sc, NEG)
        sn = jnp.where(cn > 0, sn, NEG)
        m = jnp.maximum(jnp.max(sc, axis=1, keepdims=True), jnp.max(sn, axis=1, keepdims=True))
        pc = cc * jnp.exp(sc - m)
        pn = cn * jnp.exp(sn - m)
        den = jnp.sum(pc, axis=1, keepdims=True) + jnp.sum(pn, axis=1, keepdims=True)
        pc_ref[0, hh] = pc / den
        pn_ref[0, hh] = pn / den
        _shift_cache(kc, knt, ok_ref, hh, n_new)


def _sample_values(pc_ref, pn_ref, vn_ref, cv_ref, o_ref, ov_ref, n_new):
    nt = (((1,), (1,)), ((), ()))
    for hh in range(cv_ref.shape[1]):
        vc = cv_ref[0, hh]
        vnt = vn_ref[0, hh]
        o_ref[0, hh] = (
            lax.dot_general(pc_ref[0, hh].astype(BF16), vc.astype(BF16), nt, preferred_element_type=F32)
            + lax.dot_general(pn_ref[0, hh].astype(BF16), vnt.astype(BF16), nt, preferred_element_type=F32))
        _shift_cache(vc, vnt, ov_ref, hh, n_new)


def _hosted_heads(grid, head_groups, heads_per_step):
    def spec(*tail):
        def index(*ids):
            step = 0
            for size, idx in zip(grid, ids, strict=True):
                step = step * size + idx
            return (step // head_groups, step % head_groups, 0, 0)
        return pl.BlockSpec((1, heads_per_step) + tail, index)
    return spec


def _proj_sample_kernel(x_ref, g_ref, w_ref, z_ref, wbf_ref):
    xn = _rms(x_ref[...], g_ref[...]).astype(BF16)
    w = w_ref[...].astype(BF16)
    wbf_ref[...] = w
    z_ref[...] = jnp.dot(xn, w, preferred_element_type=F32)


def _project_sample(x_tb, g, w):
    rows = x_tb.shape[0]
    cols = pl.BlockSpec((D_MODEL, CAST_COLS), lambda c: (0, c))
    return pl.pallas_call(
        _proj_sample_kernel,
        grid=(IN_WIDTH // CAST_COLS,),
        in_specs=[pl.BlockSpec((rows, D_MODEL), lambda c: (0, 0)),
                  pl.BlockSpec((1, D_MODEL), lambda c: (0, 0)),
                  cols],
        out_specs=[pl.BlockSpec((rows, CAST_COLS), lambda c: (0, c)), cols],
        out_shape=[jax.ShapeDtypeStruct((rows, IN_WIDTH), F32),
                   jax.ShapeDtypeStruct((D_MODEL, IN_WIDTH), BF16)],
        compiler_params=pltpu.CompilerParams(dimension_semantics=("arbitrary",),
                                             vmem_limit_bytes=VMEM_LIMIT),
        name="proj_sample",
    )(x_tb, g, w)


def _proj_kernel(x_ref, g_ref, w_ref, qs_ref, kn_ref, ck_ref,
                 q_ref, k_ref, v_ref, sg_ref, u_ref, sgp_ref, kt_ref, vt_ref, pc_ref, pn_ref, ok_ref, *, n_new):
    _sample_scores(qs_ref, kn_ref, ck_ref, pc_ref, pn_ref, ok_ref, n_new)

    xn = _rms(x_ref[0], g_ref[...]).astype(BF16)

    def part(c):
        return jnp.dot(xn, w_ref[:, c * ATTN_WIDTH:(c + 1) * ATTN_WIDTH], preferred_element_type=F32)

    k = part(1)
    k_ref[0] = k
    kt_ref[0] = k.T
    sg_ref[0] = _silu(part(3)).astype(BF16)
    v = part(2)
    v_ref[0] = v
    vt_ref[0] = v.T
    sgp_ref[0] = _silu(part(5)).astype(BF16)
    u_ref[0] = part(4)
    q_ref[0] = part(0) * (ATTN_SCALE * LOG2E)


def _project_prompt_score_sample(x, g, w_bf, q_s, knt, ck, n_new):
    b, s, _ = x.shape
    n_batch, h, dh, buf = ck.shape
    q_rows = q_s.shape[2]
    tiles = s // PROJ_ROWS
    hosted = n_batch * h // (b * tiles)
    assert hosted * b * tiles == n_batch * h and h % hosted == 0, "sample heads split evenly over the grid"
    first_kept = (s - MAX_WINDOW) // PROJ_ROWS
    tok = lambda width: pl.BlockSpec((1, PROJ_ROWS, width), lambda bi, i: (bi, i, 0))
    feat = pl.BlockSpec((1, ATTN_WIDTH, PROJ_ROWS), lambda bi, i: (bi, 0, jnp.maximum(i - first_kept, 0)))
    heads = _hosted_heads((b, tiles), h // hosted, hosted)
    f32 = lambda *shape: jax.ShapeDtypeStruct(shape, F32)
    bf16 = lambda *shape: jax.ShapeDtypeStruct(shape, BF16)
    return pl.pallas_call(
        functools.partial(_proj_kernel, n_new=n_new),
        grid=(b, tiles),
        in_specs=[tok(D_MODEL),
                  pl.BlockSpec((1, D_MODEL), lambda bi, i: (0, 0)),
                  pl.BlockSpec((D_MODEL, IN_WIDTH), lambda bi, i: (0, 0), pipeline_mode=pl.Buffered(1)),
                  heads(q_rows, dh), heads(dh, LANES), heads(dh, buf)],
        out_specs=[tok(ATTN_WIDTH)] * 6 + [feat, feat,
                   heads(q_rows, buf), heads(q_rows, LANES), heads(dh, buf)],
        out_shape=[f32(b, s, ATTN_WIDTH)] * 3
        + [bf16(b, s, ATTN_WIDTH), f32(b, s, POOL_WIDTH), bf16(b, s, POOL_WIDTH)]
        + [f32(b, ATTN_WIDTH, MAX_WINDOW)] * 2
        + [f32(n_batch, h, q_rows, buf), f32(n_batch, h, q_rows, LANES), f32(n_batch, h, dh, buf)],
        compiler_params=pltpu.CompilerParams(dimension_semantics=("arbitrary", "arbitrary"),
                                             vmem_limit_bytes=VMEM_LIMIT),
        name="proj_prompt_score_sample",
    )(x, g, w_bf, q_s, knt, ck)


def _attn_kernel(q_ref, k_ref, v_ref, sg_ref, o_ref, acc, mx, sm, qsel, kmask):
    seq = q_ref.shape[0]

    kk = lax.broadcasted_iota(jnp.int32, (2 * BLOCK, BLOCK), 0)
    qq = lax.broadcasted_iota(jnp.int32, (2 * BLOCK, BLOCK), 1)
    band = (kk - qq >= 0) & (kk - qq <= BLOCK)
    kmask[0] = jnp.where(band, 0.0, NEG).astype(BF16)
    kmask[1] = jnp.where(band & (kk >= BLOCK), 0.0, NEG).astype(BF16)
    row = lax.broadcasted_iota(jnp.int32, (HEAD_PAIR * BLOCK, BLOCK), 0) % BLOCK
    col = lax.broadcasted_iota(jnp.int32, (HEAD_PAIR * BLOCK, BLOCK), 1)
    qsel[...] = jnp.where(row == col, 1.0, 0.0).astype(BF16)

    lane = lax.broadcasted_iota(jnp.int32, (BLOCK, LANES), 1)
    low = lane < HEAD_DIM
    ones = jnp.ones((2 * BLOCK, LANES), BF16)
    nt = (((1,), (1,)), ((), ()))

    def keys(ref, qs, d):
        span = d * BLOCK
        if qs >= span:
            return ref[pl.ds(qs - span, 2 * BLOCK, stride=d), :]
        newer = ref[pl.ds(qs, BLOCK, stride=d), :]
        return jnp.concatenate([newer, newer], axis=0)

    blocks = [[(pat, d, r, j) for r in range(d) for j in range(seq // (d * BLOCK))]
              for pat, (_, d) in enumerate(DILATED_PATTERNS)]
    for pat, d, r, j in [blk for group in zip(*reversed(blocks), strict=True) for blk in group]:
        qs = r + d * BLOCK * j
        rows = pl.ds(qs, BLOCK, stride=d)
        qb = q_ref[rows, :]
        q2 = jnp.concatenate([jnp.where(low, qb, 0.0), jnp.where(low, 0.0, qb)], axis=0).astype(BF16)
        kb = keys(k_ref, qs, d).astype(BF16)
        vb = keys(v_ref, qs, d).astype(BF16)
        s = lax.dot_general(jnp.concatenate([q2, qsel[...]], axis=1),
                            jnp.concatenate([kb, kmask[1 if j == 0 else 0]], axis=1), nt,
                            preferred_element_type=F32)
        m = jnp.max(s, axis=1, keepdims=True)
        p = jnp.exp2(s - m)
        ol = jnp.dot(p.astype(BF16), jnp.concatenate([vb, ones], axis=1),
                     preferred_element_type=F32)
        acc[pat, rows, :] = jnp.where(low, ol[:BLOCK, :LANES], ol[BLOCK:, :LANES])
        mx[pat, rows, :] = jnp.where(low, m[:BLOCK], m[BLOCK:])
        sm[pat, rows, :] = jnp.where(low, ol[:BLOCK, LANES:], ol[BLOCK:, LANES:])

    m_all = jnp.maximum(jnp.maximum(mx[0], mx[1]), mx[2])
    num = jnp.zeros((seq, LANES), F32)
    den = jnp.zeros((seq, LANES), F32)
    for pat in range(len(DILATED_PATTERNS)):
        w = jnp.exp2(mx[pat] - m_all)
        num = num + w * acc[pat]
        den = den + w * sm[pat]
    o_ref[...] = (num / den).astype(BF16) * sg_ref[...]


def _attend_prompt(q, k, v, sg):
    b, s, _ = q.shape
    whole = pl.BlockSpec((None, s, LANES), lambda bi, hp: (bi, 0, hp))
    n_pat = len(DILATED_PATTERNS)
    return pl.pallas_call(
        _attn_kernel,
        grid=(b, N_HEADS // HEAD_PAIR),
        in_specs=[whole] * 4,
        out_specs=whole,
        out_shape=jax.ShapeDtypeStruct((b, s, ATTN_WIDTH), BF16),
        scratch_shapes=[pltpu.VMEM((n_pat, s, LANES), F32),
                        pltpu.VMEM((n_pat, s, LANES), F32),
                        pltpu.VMEM((n_pat, s, LANES), F32),
                        pltpu.VMEM((HEAD_PAIR * BLOCK, BLOCK), BF16),
                        pltpu.VMEM((2, 2 * BLOCK, BLOCK), BF16)],
        compiler_params=pltpu.CompilerParams(dimension_semantics=("arbitrary",) * 2,
                                             vmem_limit_bytes=VMEM_LIMIT),
        name="attn_prompt",
    )(q, k, v, sg)


def _merge_kernel(x_ref, a_ref, u_ref, uh_ref, sgp_ref, wp_ref, ps_ref, wo_ref, fg_ref,
                  pc_ref, pn_ref, vn_ref, cv_ref, y_ref, os_ref, ov_ref, ue, *, n_new):
    i = pl.program_id(1)
    rows = u_ref.shape[1]
    ue[0:HIST_ROWS] = jnp.where(i == 0, 0.0, uh_ref[0])
    ue[HIST_ROWS:HIST_ROWS + rows] = u_ref[0]
    pos = i * rows + lax.broadcasted_iota(jnp.int32, (rows, 1), 0)
    counts = [jnp.minimum(pos + 1, w).astype(F32) for w in POOL_WINDOWS]
    pooled = _pooled(lambda g, j: ue[pl.ds(HIST_ROWS - j, rows), _group_cols(g)], counts)
    pool = _pool_project(lambda g: pooled[g].astype(BF16), wp_ref, ps_ref[...])
    mix = jnp.concatenate([a_ref[0], (pool * sgp_ref[0].astype(F32)).astype(BF16)], axis=-1)
    h = x_ref[0] + jnp.dot(mix, wo_ref[...], preferred_element_type=F32)
    y_ref[0] = _rms(h, fg_ref[...])
    _sample_values(pc_ref, pn_ref, vn_ref, cv_ref, os_ref, ov_ref, n_new)


def _merge_prompt_value_sample(x, mix_a, u, sgp, wp_bf, pool_scale, wo_bf, final_g, pc, pn, vnt, cv, n_new):
    b, s, _ = x.shape
    n_batch, h, dh, buf = cv.shape
    q_rows = pc.shape[2]
    tiles = s // MERGE_ROWS
    hosted = n_batch * h // (b * tiles)
    assert hosted * b * tiles == n_batch * h and h % hosted == 0, "sample heads split evenly over the grid"
    per_tile = MERGE_ROWS // HIST_ROWS
    tok = lambda width: pl.BlockSpec((1, MERGE_ROWS, width), lambda bi, i: (bi, i, 0))
    const = lambda shape: pl.BlockSpec(shape, lambda bi, i: (0,) * len(shape))
    heads = _hosted_heads((b, tiles), h // hosted, hosted)
    return pl.pallas_call(
        functools.partial(_merge_kernel, n_new=n_new),
        grid=(b, tiles),
        in_specs=[tok(D_MODEL), tok(ATTN_WIDTH), tok(POOL_WIDTH),
                  pl.BlockSpec((1, HIST_ROWS, POOL_WIDTH),
                               lambda bi, i: (bi, jnp.maximum(i * per_tile - 1, 0), 0)),
                  tok(POOL_WIDTH),
                  const((len(POOL_WINDOWS), POOL_GROUP_WIDTH, POOL_GROUP_WIDTH)),
                  const((1, POOL_WIDTH)),
                  pl.BlockSpec((D_MODEL, D_MODEL), lambda bi, i: (0, 0), pipeline_mode=pl.Buffered(1)),
                  const((1, D_MODEL)),
                  heads(q_rows, buf), heads(q_rows, LANES), heads(dh, LANES), heads(dh, buf)],
        out_specs=[tok(D_MODEL), heads(q_rows, dh), heads(dh, buf)],
        out_shape=[jax.ShapeDtypeStruct((b, s, D_MODEL), F32),
                   jax.ShapeDtypeStruct((n_batch, h, q_rows, dh), F32),
                   jax.ShapeDtypeStruct((n_batch, h, dh, buf), F32)],
        scratch_shapes=[pltpu.VMEM((HIST_ROWS + MERGE_ROWS, POOL_WIDTH), F32)],
        compiler_params=pltpu.CompilerParams(dimension_semantics=("arbitrary", "arbitrary"),
                                             vmem_limit_bytes=VMEM_LIMIT),
        name="merge_prompt_value_sample",
    )(x, mix_a, u, u, sgp, wp_bf, pool_scale, wo_bf, final_g, pc, pn, vnt, cv)


def _merge_sample_kernel(x_ref, o_ref, z_ref, hist_ref, wp_ref, ps_ref, wo_ref, fg_ref, y_ref, pool_ref, ue,
                         *, n_new, n_batch):
    u_col = 4 * ATTN_WIDTH
    ue[0:POOL_HIST] = hist_ref[...]
    ue[POOL_HIST:POOL_HIST + n_new] = z_ref[:, u_col:u_col + POOL_WIDTH].reshape(n_new, n_batch, POOL_WIDTH)
    pool_ref[...] = ue[n_new:n_new + POOL_HIST]

    rows = n_new * n_batch

    def window_sum(g, j):
        return ue[pl.ds(POOL_HIST - j, n_new), :, _group_cols(g)].reshape(rows, POOL_GROUP_WIDTH)

    pos = PAST_LEN + lax.broadcasted_iota(jnp.int32, (rows, 1), 0) // n_batch
    counts = [jnp.minimum(pos + 1, w).astype(F32) for w in POOL_WINDOWS]
    pooled = _pooled(window_sum, counts)
    pool = _pool_project(lambda g: pooled[g].astype(BF16), wp_ref, ps_ref[...])
    ga = z_ref[:, 3 * ATTN_WIDTH:4 * ATTN_WIDTH]
    gp = z_ref[:, u_col + POOL_WIDTH:u_col + 2 * POOL_WIDTH]
    mix = jnp.concatenate([o_ref[...] * _silu(ga), pool * _silu(gp)], axis=-1).astype(BF16)
    h = x_ref[...] + jnp.dot(mix, wo_ref[...], preferred_element_type=F32)
    y_ref[...] = _rms(h, fg_ref[...])


def _merge_sample(x_tb, o_tb, z_tb, hist, wp_bf, pool_scale, wo_bf, final_g, n_new, n_batch):
    return pl.pallas_call(
        functools.partial(_merge_sample_kernel, n_new=n_new, n_batch=n_batch),
        out_shape=[jax.ShapeDtypeStruct(x_tb.shape, F32),
                   jax.ShapeDtypeStruct((POOL_HIST, n_batch, POOL_WIDTH), F32)],
        scratch_shapes=[pltpu.VMEM((POOL_HIST + n_new, n_batch, POOL_WIDTH), F32)],
        compiler_params=pltpu.CompilerParams(vmem_limit_bytes=VMEM_LIMIT),
        name="merge_sample",
    )(x_tb, o_tb, z_tb, hist, wp_bf, pool_scale, wo_bf, final_g)


def _to_feature_major(z_cols, n_new, n_batch):
    t = z_cols.reshape(n_new, n_batch, N_HEADS, HEAD_DIM).transpose(1, 2, 3, 0)
    return jnp.pad(t, ((0, 0), (0, 0), (0, 0), (0, LANES - n_new)))


def kernel(x_prompt, x_sample, cache_k, cache_v, state_pool, norm_g, w_in, w_pool, pool_scale, w_out, final_norm_g):
    depth = norm_g.shape[0]
    assert depth == 1, "single-layer step"
    b, s, _ = x_prompt.shape
    n_batch, n_new, _ = x_sample.shape
    buf = cache_k.shape[2]
    assert s % SUPER == 0 and s >= MAX_WINDOW and buf == MAX_WINDOW and n_new <= POOL_HIST

    g = norm_g[0][None, :]
    fg = final_norm_g[None, :]
    wo_bf = w_out[0].astype(BF16)
    wp_bf = w_pool[0].astype(BF16)
    ps = pool_scale[0][None, :]

    x_tb = x_sample.transpose(1, 0, 2).reshape(n_new * n_batch, D_MODEL)
    z_tb, w_bf = _project_sample(x_tb, g, w_in[0])
    q_s = z_tb[:, 0:ATTN_WIDTH].reshape(n_new, n_batch, N_HEADS, HEAD_DIM).transpose(1, 2, 0, 3)
    q_s = jnp.pad(q_s, ((0, 0), (0, 0), (0, SUBLANES - n_new), (0, 0)))
    knt = _to_feature_major(z_tb[:, ATTN_WIDTH:2 * ATTN_WIDTH], n_new, n_batch)
    vnt = _to_feature_major(z_tb[:, 2 * ATTN_WIDTH:3 * ATTN_WIDTH], n_new, n_batch)
    ck = cache_k[0].transpose(0, 2, 3, 1)
    cv = cache_v[0].transpose(0, 2, 3, 1)

    q, k, v, sg, u, sgp, kt, vt, pc, pn, ok = _project_prompt_score_sample(x_prompt, g, w_bf, q_s, knt, ck, n_new)
    mix_a = _attend_prompt(q, k, v, sg)
    y_prompt, o_s, ov = _merge_prompt_value_sample(x_prompt, mix_a, u, sgp, wp_bf, ps, wo_bf, fg,
                                                   pc, pn, vnt, cv, n_new)
    to_window = lambda t: t.reshape(b, N_HEADS, HEAD_DIM, MAX_WINDOW).transpose(0, 3, 1, 2)[None]
    k_prompt, v_prompt = to_window(kt), to_window(vt)
    pool_prompt = u[:, s - POOL_HIST:][None]

    o_tb = o_s[:, :, :n_new].transpose(2, 0, 1, 3).reshape(n_new * n_batch, ATTN_WIDTH)
    hist = state_pool[0].transpose(1, 0, 2)
    y_tb, pool_tb = _merge_sample(x_tb, o_tb, z_tb, hist, wp_bf, ps, wo_bf, fg, n_new, n_batch)
    y_sample = y_tb.reshape(n_new, n_batch, D_MODEL).transpose(1, 0, 2)
    k_sample = ok.transpose(0, 3, 1, 2)[None]
    v_sample = ov.transpose(0, 3, 1, 2)[None]
    pool_sample = pool_tb.transpose(1, 0, 2)[None]

    return (y_prompt, y_sample, k_prompt, v_prompt, pool_prompt, k_sample, v_sample, pool_sample)
```

```python
import functools

import jax
import jax.numpy as jnp
from jax import lax
from jax.experimental import pallas as pl
from jax.experimental.pallas import tpu as pltpu

F32 = jnp.float32
BF16 = jnp.bfloat16

D_MODEL = 2048
ATTN_WIDTH = 1024
HEAD_DIM = 64
N_HEADS = 16
POOL_WIDTH = 1024
POOL_WINDOWS = (2, 4, 8, 16)
POOL_GROUP_WIDTH = 256
POOL_HIST = 15
IN_WIDTH = 6144
DILATED_PATTERNS = ((128, 1), (512, 4), (2048, 16))
MAX_WINDOW = 2048
BLOCK = 128
PAST_LEN = 16384
EPS = 1e-6
ATTN_SCALE = HEAD_DIM ** -0.5
NEG = -1e30
LOG2E = 1.4426950408889634

LANES = 128
SUBLANES = 8
HEAD_PAIR = LANES // HEAD_DIM
HIST_ROWS = 16

PROJ_ROWS = 256
MERGE_ROWS = 256
SUPER = 2048
CAST_COLS = 1024
VMEM_LIMIT = 58 * 1024 * 1024


def _silu(x):
    return x / (1.0 + jnp.exp(-x))


def _rms(x, g):
    ms = jnp.mean(x * x, axis=-1, keepdims=True)
    return x * lax.rsqrt(ms + EPS) * g


def _group_cols(g):
    return slice(g * POOL_GROUP_WIDTH, (g + 1) * POOL_GROUP_WIDTH)


def _pooled(window_sum, counts):
    outs = []
    for g, w in enumerate(POOL_WINDOWS):
        cur = window_sum(g, 0)
        tot = cur
        for j in range(1, w):
            tot = tot + window_sum(g, j)
        outs.append(tot / counts[g] - cur)
    return outs


def _pool_project(pooled_group, wp_ref, scale):
    outs = [jnp.dot(pooled_group(g), wp_ref[g], preferred_element_type=F32) for g in range(len(POOL_WINDOWS))]
    return jnp.concatenate(outs, axis=-1) * scale


def _multiplicity(delta):
    c = jnp.zeros(delta.shape, F32)
    for window, d in DILATED_PATTERNS:
        assert d & (d - 1) == 0, "power-of-two dilation: residue test is a bit mask"
        c = c + jnp.where(((delta & (d - 1)) == 0) & (delta <= window), 1.0, 0.0)
    return c


def _shift_cache(src, new, dst_ref, hh, n_new):
    buf = src.shape[-1]
    tail_lane = lax.broadcasted_iota(jnp.int32, (HEAD_DIM, LANES), 1)
    rolled = pltpu.roll(src, buf - n_new, 1)
    tail = rolled[:, buf - LANES:]
    placed = pltpu.roll(new, LANES - n_new, 1)
    dst_ref[0, hh, :, 0:buf - LANES] = rolled[:, 0:buf - LANES]
    dst_ref[0, hh, :, buf - LANES:buf] = jnp.where(tail_lane >= LANES - n_new, placed, tail)


def _sample_scores(q_ref, kn_ref, ck_ref, pc_ref, pn_ref, ok_ref, n_new):
    buf = ck_ref.shape[-1]
    q_rows = q_ref.shape[2]
    qi = lax.broadcasted_iota(jnp.int32, (q_rows, buf), 0)
    cc = _multiplicity(buf + qi - lax.broadcasted_iota(jnp.int32, (q_rows, buf), 1))
    qn = lax.broadcasted_iota(jnp.int32, (q_rows, LANES), 0)
    nl = lax.broadcasted_iota(jnp.int32, (q_rows, LANES), 1)
    cn = jnp.where(nl <= qn, _multiplicity(qn - jnp.minimum(nl, qn)), 0.0)

    for hh in range(ck_ref.shape[1]):
        kc = ck_ref[0, hh]
        knt = kn_ref[0, hh]
        qh = (q_ref[0, hh] * ATTN_SCALE).astype(BF16)
        sc = jnp.dot(qh, kc.astype(BF16), preferred_element_type=F32)
        sn = jnp.dot(qh, knt.astype(BF16), preferred_element_type=F32)
        sc = jnp.where(cc > 0, sc, NEG)
        sn = jnp.where(cn > 0, sn, NEG)
        m = jnp.maximum(jnp.max(sc, axis=1, keepdims=True), jnp.max(sn, axis=1, keepdims=True))
        pc = cc * jnp.exp(sc - m)
        pn = cn * jnp.exp(sn - m)
        den = jnp.sum(pc, axis=1, keepdims=True) + jnp.sum(pn, axis=1, keepdims=True)
        pc_ref[0, hh] = pc / den
        pn_ref[0, hh] = pn / den
        _shift_cache(kc, knt, ok_ref, hh, n_new)


def _sample_values(pc_ref, pn_ref, vn_ref, cv_ref, o_ref, ov_ref, n_new):
    nt = (((1,), (1,)), ((), ()))
    for hh in range(cv_ref.shape[1]):
        vc = cv_ref[0, hh]
        vnt = vn_ref[0, hh]
        o_ref[0, hh] = (
            lax.dot_general(pc_ref[0, hh].astype(BF16), vc.astype(BF16), nt, preferred_element_type=F32)
            + lax.dot_general(pn_ref[0, hh].astype(BF16), vnt.astype(BF16), nt, preferred_element_type=F32))
        _shift_cache(vc, vnt, ov_ref, hh, n_new)


def _hosted_heads(grid, head_groups, heads_per_step):
    def spec(*tail):
        def index(*ids):
            step = 0
            for size, idx in zip(grid, ids, strict=True):
                step = step * size + idx
            return (step // head_groups, step % head_groups, 0, 0)
        return pl.BlockSpec((1, heads_per_step) + tail, index)
    return spec


def _proj_sample_kernel(x_ref, g_ref, w_ref, z_ref, wbf_ref):
    xn = _rms(x_ref[...], g_ref[...]).astype(BF16)
    w = w_ref[...].astype(BF16)
    wbf_ref[...] = w
    z_ref[...] = jnp.dot(xn, w, preferred_element_type=F32)


def _project_sample(x_tb, g, w):
    rows = x_tb.shape[0]
    cols = pl.BlockSpec((D_MODEL, CAST_COLS), lambda c: (0, c))
    return pl.pallas_call(
        _proj_sample_kernel,
        grid=(IN_WIDTH // CAST_COLS,),
        in_specs=[pl.BlockSpec((rows, D_MODEL), lambda c: (0, 0)),
                  pl.BlockSpec((1, D_MODEL), lambda c: (0, 0)),
                  cols],
        out_specs=[pl.BlockSpec((rows, CAST_COLS), lambda c: (0, c)), cols],
        out_shape=[jax.ShapeDtypeStruct((rows, IN_WIDTH), F32),
                   jax.ShapeDtypeStruct((D_MODEL, IN_WIDTH), BF16)],
        compiler_params=pltpu.CompilerParams(dimension_semantics=("arbitrary",),
                                             vmem_limit_bytes=VMEM_LIMIT),
        name="proj_sample",
    )(x_tb, g, w)


def _proj_kernel(x_ref, g_ref, w_ref, qs_ref, kn_ref, ck_ref,
                 q_ref, k_ref, v_ref, sg_ref, u_ref, sgp_ref, kt_ref, vt_ref, ulast_ref,
                 pc_ref, pn_ref, ok_ref, *, n_new):
    _sample_scores(qs_ref, kn_ref, ck_ref, pc_ref, pn_ref, ok_ref, n_new)

    xn = _rms(x_ref[0], g_ref[...]).astype(BF16)

    def part(c):
        return jnp.dot(xn, w_ref[:, c * ATTN_WIDTH:(c + 1) * ATTN_WIDTH], preferred_element_type=F32)

    k = part(1)
    k_ref[0] = k
    kt_ref[0] = k.T
    sg_ref[0] = _silu(part(3)).astype(BF16)
    v = part(2)
    v_ref[0] = v
    vt_ref[0] = v.T
    sgp_ref[0] = _silu(part(5)).astype(BF16)
    u = part(4)
    u_ref[0] = u.astype(BF16)
    ulast_ref[0] = u[u.shape[0] - HIST_ROWS:]
    q_ref[0] = part(0) * (ATTN_SCALE * LOG2E)


def _project_prompt_score_sample(x, g, w_bf, q_s, knt, ck, n_new):
    b, s, _ = x.shape
    n_batch, h, dh, buf = ck.shape
    q_rows = q_s.shape[2]
    tiles = s // PROJ_ROWS
    hosted = n_batch * h // (b * tiles)
    assert hosted * b * tiles == n_batch * h and h % hosted == 0, "sample heads split evenly over the grid"
    first_kept = (s - MAX_WINDOW) // PROJ_ROWS
    tok = lambda width: pl.BlockSpec((1, PROJ_ROWS, width), lambda bi, i: (bi, i, 0))
    feat = pl.BlockSpec((1, ATTN_WIDTH, PROJ_ROWS), lambda bi, i: (bi, 0, jnp.maximum(i - first_kept, 0)))
    heads = _hosted_heads((b, tiles), h // hosted, hosted)
    f32 = lambda *shape: jax.ShapeDtypeStruct(shape, F32)
    bf16 = lambda *shape: jax.ShapeDtypeStruct(shape, BF16)
    return pl.pallas_call(
        functools.partial(_proj_kernel, n_new=n_new),
        grid=(b, tiles),
        in_specs=[tok(D_MODEL),
                  pl.BlockSpec((1, D_MODEL), lambda bi, i: (0, 0)),
                  pl.BlockSpec((D_MODEL, IN_WIDTH), lambda bi, i: (0, 0), pipeline_mode=pl.Buffered(1)),
                  heads(q_rows, dh), heads(dh, LANES), heads(dh, buf)],
        out_specs=[tok(ATTN_WIDTH)] * 6 + [feat, feat,
                   pl.BlockSpec((1, HIST_ROWS, POOL_WIDTH), lambda bi, i: (bi, 0, 0)),
                   heads(q_rows, buf), heads(q_rows, LANES), heads(dh, buf)],
        out_shape=[f32(b, s, ATTN_WIDTH)] * 3
        + [bf16(b, s, ATTN_WIDTH), bf16(b, s, POOL_WIDTH), bf16(b, s, POOL_WIDTH)]
        + [f32(b, ATTN_WIDTH, MAX_WINDOW)] * 2
        + [f32(b, HIST_ROWS, POOL_WIDTH)]
        + [f32(n_batch, h, q_rows, buf), f32(n_batch, h, q_rows, LANES), f32(n_batch, h, dh, buf)],
        compiler_params=pltpu.CompilerParams(dimension_semantics=("arbitrary", "arbitrary"),
                                             vmem_limit_bytes=VMEM_LIMIT),
        name="proj_prompt_score_sample",
    )(x, g, w_bf, q_s, knt, ck)


def _attn_kernel(q_ref, k_ref, v_ref, sg_ref, o_ref, acc, mx, sm, qsel, kmask):
    seq = q_ref.shape[0]

    kk = lax.broadcasted_iota(jnp.int32, (2 * BLOCK, BLOCK), 0)
    qq = lax.broadcasted_iota(jnp.int32, (2 * BLOCK, BLOCK), 1)
    band = (kk - qq >= 0) & (kk - qq <= BLOCK)
    kmask[0] = jnp.where(band, 0.0, NEG).astype(BF16)
    kmask[1] = jnp.where(band & (kk >= BLOCK), 0.0, NEG).astype(BF16)
    row = lax.broadcasted_iota(jnp.int32, (HEAD_PAIR * BLOCK, BLOCK), 0) % BLOCK
    col = lax.broadcasted_iota(jnp.int32, (HEAD_PAIR * BLOCK, BLOCK), 1)
    qsel[...] = jnp.where(row == col, 1.0, 0.0).astype(BF16)

    lane = lax.broadcasted_iota(jnp.int32, (BLOCK, LANES), 1)
    low = lane < HEAD_DIM
    ones = jnp.ones((2 * BLOCK, LANES), BF16)
    nt = (((1,), (1,)), ((), ()))

    def keys(ref, qs, d):
        span = d * BLOCK
        if qs >= span:
            return ref[pl.ds(qs - span, 2 * BLOCK, stride=d), :]
        newer = ref[pl.ds(qs, BLOCK, stride=d), :]
        return jnp.concatenate([newer, newer], axis=0)

    blocks = [[(pat, d, r, j) for r in range(d) for j in range(seq // (d * BLOCK))]
              for pat, (_, d) in enumerate(DILATED_PATTERNS)]
    for pat, d, r, j in [blk for group in zip(*reversed(blocks), strict=True) for blk in group]:
        qs = r + d * BLOCK * j
        rows = pl.ds(qs, BLOCK, stride=d)
        qb = q_ref[rows, :]
        q2 = jnp.concatenate([jnp.where(low, qb, 0.0), jnp.where(low, 0.0, qb)], axis=0).astype(BF16)
        kb = keys(k_ref, qs, d).astype(BF16)
        vb = keys(v_ref, qs, d).astype(BF16)
        s = lax.dot_general(jnp.concatenate([q2, qsel[...]], axis=1),
                            jnp.concatenate([kb, kmask[1 if j == 0 else 0]], axis=1), nt,
                            preferred_element_type=F32)
        m = jnp.max(s, axis=1, keepdims=True)
        p = jnp.exp2(s - m)
        ol = jnp.dot(p.astype(BF16), jnp.concatenate([vb, ones], axis=1),
                     preferred_element_type=F32)
        acc[pat, rows, :] = jnp.where(low, ol[:BLOCK, :LANES], ol[BLOCK:, :LANES])
        mx[pat, rows, :] = jnp.where(low, m[:BLOCK], m[BLOCK:])
        sm[pat, rows, :] = jnp.where(low, ol[:BLOCK, LANES:], ol[BLOCK:, LANES:])

    m_all = jnp.maximum(jnp.maximum(mx[0], mx[1]), mx[2])
    num = jnp.zeros((seq, LANES), F32)
    den = jnp.zeros((seq, LANES), F32)
    for pat in range(len(DILATED_PATTERNS)):
        w = jnp.exp2(mx[pat] - m_all)
        num = num + w * acc[pat]
        den = den + w * sm[pat]
    o_ref[...] = (num / den).astype(BF16) * sg_ref[...]


def _attend_prompt(q, k, v, sg):
    b, s, _ = q.shape
    whole = pl.BlockSpec((None, s, LANES), lambda bi, hp: (bi, 0, hp))
    n_pat = len(DILATED_PATTERNS)
    return pl.pallas_call(
        _attn_kernel,
        grid=(b, N_HEADS // HEAD_PAIR),
        in_specs=[whole] * 4,
        out_specs=whole,
        out_shape=jax.ShapeDtypeStruct((b, s, ATTN_WIDTH), BF16),
        scratch_shapes=[pltpu.VMEM((n_pat, s, LANES), F32),
                        pltpu.VMEM((n_pat, s, LANES), F32),
                        pltpu.VMEM((n_pat, s, LANES), F32),
                        pltpu.VMEM((HEAD_PAIR * BLOCK, BLOCK), BF16),
                        pltpu.VMEM((2, 2 * BLOCK, BLOCK), BF16)],
        compiler_params=pltpu.CompilerParams(dimension_semantics=("arbitrary",) * 2,
                                             vmem_limit_bytes=VMEM_LIMIT),
        name="attn_prompt",
    )(q, k, v, sg)


def _merge_kernel(x_ref, a_ref, u_ref, uh_ref, sgp_ref, wp_ref, ps_ref, wo_ref, fg_ref,
                  pc_ref, pn_ref, vn_ref, cv_ref, y_ref, os_ref, ov_ref, ue, *, n_new):
    i = pl.program_id(1)
    rows = u_ref.shape[1]
    ue[0:HIST_ROWS] = jnp.where(i == 0, 0.0, uh_ref[0].astype(F32))
    ue[HIST_ROWS:HIST_ROWS + rows] = u_ref[0].astype(F32)
    pos = i * rows + lax.broadcasted_iota(jnp.int32, (rows, 1), 0)
    counts = [jnp.minimum(pos + 1, w).astype(F32) for w in POOL_WINDOWS]
    pooled = _pooled(lambda g, j: ue[pl.ds(HIST_ROWS - j, rows), _group_cols(g)], counts)
    pool = _pool_project(lambda g: pooled[g].astype(BF16), wp_ref, ps_ref[...])
    mix = jnp.concatenate([a_ref[0], (pool * sgp_ref[0].astype(F32)).astype(BF16)], axis=-1)
    h = x_ref[0] + jnp.dot(mix, wo_ref[...], preferred_element_type=F32)
    y_ref[0] = _rms(h, fg_ref[...])
    _sample_values(pc_ref, pn_ref, vn_ref, cv_ref, os_ref, ov_ref, n_new)


def _merge_prompt_value_sample(x, mix_a, u, sgp, wp_bf, pool_scale, wo_bf, final_g, pc, pn, vnt, cv, n_new):
    b, s, _ = x.shape
    n_batch, h, dh, buf = cv.shape
    q_rows = pc.shape[2]
    tiles = s // MERGE_ROWS
    hosted = n_batch * h // (b * tiles)
    assert hosted * b * tiles == n_batch * h and h % hosted == 0, "sample heads split evenly over the grid"
    per_tile = MERGE_ROWS // HIST_ROWS
    tok = lambda width: pl.BlockSpec((1, MERGE_ROWS, width), lambda bi, i: (bi, i, 0))
    const = lambda shape: pl.BlockSpec(shape, lambda bi, i: (0,) * len(shape))
    heads = _hosted_heads((b, tiles), h // hosted, hosted)
    return pl.pallas_call(
        functools.partial(_merge_kernel, n_new=n_new),
        grid=(b, tiles),
        in_specs=[tok(D_MODEL), tok(ATTN_WIDTH), tok(POOL_WIDTH),
                  pl.BlockSpec((1, HIST_ROWS, POOL_WIDTH),
                               lambda bi, i: (bi, jnp.maximum(i * per_tile - 1, 0), 0)),
                  tok(POOL_WIDTH),
                  const((len(POOL_WINDOWS), POOL_GROUP_WIDTH, POOL_GROUP_WIDTH)),
                  const((1, POOL_WIDTH)),
                  pl.BlockSpec((D_MODEL, D_MODEL), lambda bi, i: (0, 0), pipeline_mode=pl.Buffered(1)),
                  const((1, D_MODEL)),
                  heads(q_rows, buf), heads(q_rows, LANES), heads(dh, LANES), heads(dh, buf)],
        out_specs=[tok(D_MODEL), heads(q_rows, dh), heads(dh, buf)],
        out_shape=[jax.ShapeDtypeStruct((b, s, D_MODEL), F32),
                   jax.ShapeDtypeStruct((n_batch, h, q_rows, dh), F32),
                   jax.ShapeDtypeStruct((n_batch, h, dh, buf), F32)],
        scratch_shapes=[pltpu.VMEM((HIST_ROWS + MERGE_ROWS, POOL_WIDTH), F32)],
        compiler_params=pltpu.CompilerParams(dimension_semantics=("arbitrary", "arbitrary"),
                                             vmem_limit_bytes=VMEM_LIMIT),
        name="merge_prompt_value_sample",
    )(x, mix_a, u, u, sgp, wp_bf, pool_scale, wo_bf, final_g, pc, pn, vnt, cv)


def _merge_sample_kernel(x_ref, o_ref, z_ref, hist_ref, wp_ref, ps_ref, wo_ref, fg_ref, y_ref, pool_ref, ue,
                         *, n_new, n_batch):
    u_col = 4 * ATTN_WIDTH
    ue[0:POOL_HIST] = hist_ref[...]
    ue[POOL_HIST:POOL_HIST + n_new] = z_ref[:, u_col:u_col + POOL_WIDTH].reshape(n_new, n_batch, POOL_WIDTH)
    pool_ref[...] = ue[n_new:n_new + POOL_HIST]

    rows = n_new * n_batch

    def window_sum(g, j):
        return ue[pl.ds(POOL_HIST - j, n_new), :, _group_cols(g)].reshape(rows, POOL_GROUP_WIDTH)

    pos = PAST_LEN + lax.broadcasted_iota(jnp.int32, (rows, 1), 0) // n_batch
    counts = [jnp.minimum(pos + 1, w).astype(F32) for w in POOL_WINDOWS]
    pooled = _pooled(window_sum, counts)
    pool = _pool_project(lambda g: pooled[g].astype(BF16), wp_ref, ps_ref[...])
    ga = z_ref[:, 3 * ATTN_WIDTH:4 * ATTN_WIDTH]
    gp = z_ref[:, u_col + POOL_WIDTH:u_col + 2 * POOL_WIDTH]
    mix = jnp.concatenate([o_ref[...] * _silu(ga), pool * _silu(gp)], axis=-1).astype(BF16)
    h = x_ref[...] + jnp.dot(mix, wo_ref[...], preferred_element_type=F32)
    y_ref[...] = _rms(h, fg_ref[...])


def _merge_sample(x_tb, o_tb, z_tb, hist, wp_bf, pool_scale, wo_bf, final_g, n_new, n_batch):
    return pl.pallas_call(
        functools.partial(_merge_sample_kernel, n_new=n_new, n_batch=n_batch),
        out_shape=[jax.ShapeDtypeStruct(x_tb.shape, F32),
                   jax.ShapeDtypeStruct((POOL_HIST, n_batch, POOL_WIDTH), F32)],
        scratch_shapes=[pltpu.VMEM((POOL_HIST + n_new, n_batch, POOL_WIDTH), F32)],
        compiler_params=pltpu.CompilerParams(vmem_limit_bytes=VMEM_LIMIT),
        name="merge_sample",
    )(x_tb, o_tb, z_tb, hist, wp_bf, pool_scale, wo_bf, final_g)


def _to_feature_major(z_cols, n_new, n_batch):
    t = z_cols.reshape(n_new, n_batch, N_HEADS, HEAD_DIM).transpose(1, 2, 3, 0)
    return jnp.pad(t, ((0, 0), (0, 0), (0, 0), (0, LANES - n_new)))


def kernel(x_prompt, x_sample, cache_k, cache_v, state_pool, norm_g, w_in, w_pool, pool_scale, w_out, final_norm_g):
    depth = norm_g.shape[0]
    assert depth == 1, "single-layer step"
    b, s, _ = x_prompt.shape
    n_batch, n_new, _ = x_sample.shape
    buf = cache_k.shape[2]
    assert s % SUPER == 0 and s >= MAX_WINDOW and buf == MAX_WINDOW and n_new <= POOL_HIST

    g = norm_g[0][None, :]
    fg = final_norm_g[None, :]
    wo_bf = w_out[0].astype(BF16)
    wp_bf = w_pool[0].astype(BF16)
    ps = pool_scale[0][None, :]

    x_tb = x_sample.transpose(1, 0, 2).reshape(n_new * n_batch, D_MODEL)
    z_tb, w_bf = _project_sample(x_tb, g, w_in[0])
    q_s = z_tb[:, 0:ATTN_WIDTH].reshape(n_new, n_batch, N_HEADS, HEAD_DIM).transpose(1, 2, 0, 3)
    q_s = jnp.pad(q_s, ((0, 0), (0, 0), (0, SUBLANES - n_new), (0, 0)))
    knt = _to_feature_major(z_tb[:, ATTN_WIDTH:2 * ATTN_WIDTH], n_new, n_batch)
    vnt = _to_feature_major(z_tb[:, 2 * ATTN_WIDTH:3 * ATTN_WIDTH], n_new, n_batch)
    ck = cache_k[0].transpose(0, 2, 3, 1)
    cv = cache_v[0].transpose(0, 2, 3, 1)

    q, k, v, sg, u, sgp, kt, vt, ulast, pc, pn, ok = _project_prompt_score_sample(
        x_prompt, g, w_bf, q_s, knt, ck, n_new)
    mix_a = _attend_prompt(q, k, v, sg)
    y_prompt, o_s, ov = _merge_prompt_value_sample(x_prompt, mix_a, u, sgp, wp_bf, ps, wo_bf, fg,
                                                   pc, pn, vnt, cv, n_new)
    to_window = lambda t: t.reshape(b, N_HEADS, HEAD_DIM, MAX_WINDOW).transpose(0, 3, 1, 2)[None]
    k_prompt, v_prompt = to_window(kt), to_window(vt)
    pool_prompt = ulast[:, HIST_ROWS - POOL_HIST:][None]

    o_tb = o_s[:, :, :n_new].transpose(2, 0, 1, 3).reshape(n_new * n_batch, ATTN_WIDTH)
    hist = state_pool[0].transpose(1, 0, 2)
    y_tb, pool_tb = _merge_sample(x_tb, o_tb, z_tb, hist, wp_bf, ps, wo_bf, fg, n_new, n_batch)
    y_sample = y_tb.reshape(n_new, n_batch, D_MODEL).transpose(1, 0, 2)
    k_sample = ok.transpose(0, 3, 1, 2)[None]
    v_sample = ov.transpose(0, 3, 1, 2)[None]
    pool_sample = pool_tb.transpose(1, 0, 2)[None]

    return (y_prompt, y_sample, k_prompt, v_prompt, pool_prompt, k_sample, v_sample, pool_sample)
```

```python
import functools

import jax
import jax.numpy as jnp
from jax import lax
from jax.experimental import pallas as pl
from jax.experimental.pallas import tpu as pltpu

F32 = jnp.float32
BF16 = jnp.bfloat16

D_MODEL = 2048
ATTN_WIDTH = 1024
HEAD_DIM = 64
N_HEADS = 16
POOL_WIDTH = 1024
POOL_WINDOWS = (2, 4, 8, 16)
POOL_GROUP_WIDTH = 256
POOL_HIST = 15
IN_WIDTH = 6144
DILATED_PATTERNS = ((128, 1), (512, 4), (2048, 16))
MAX_WINDOW = 2048
BLOCK = 128
PAST_LEN = 16384
EPS = 1e-6
ATTN_SCALE = HEAD_DIM ** -0.5
NEG = -1e30
LOG2E = 1.4426950408889634

LANES = 128
SUBLANES = 8
HEAD_PAIR = LANES // HEAD_DIM
HIST_ROWS = 16

PROJ_ROWS = 256
MERGE_ROWS = 512
SUPER = 2048
CAST_COLS = 1024
VMEM_LIMIT = 60 * 1024 * 1024


def _silu(x):
    return x / (1.0 + jnp.exp(-x))


def _rms(x, g):
    ms = jnp.mean(x * x, axis=-1, keepdims=True)
    return x * lax.rsqrt(ms + EPS) * g


def _group_cols(g):
    return slice(g * POOL_GROUP_WIDTH, (g + 1) * POOL_GROUP_WIDTH)


def _pooled(window_sum, counts):
    outs = []
    for g, w in enumerate(POOL_WINDOWS):
        cur = window_sum(g, 0)
        tot = cur
        for j in range(1, w):
            tot = tot + window_sum(g, j)
        outs.append(tot / counts[g] - cur)
    return outs


def _pool_project(pooled_group, wp_ref, scale):
    outs = [jnp.dot(pooled_group(g), wp_ref[g], preferred_element_type=F32) for g in range(len(POOL_WINDOWS))]
    return jnp.concatenate(outs, axis=-1) * scale


def _multiplicity(delta):
    c = jnp.zeros(delta.shape, F32)
    for window, d in DILATED_PATTERNS:
        assert d & (d - 1) == 0, "power-of-two dilation: residue test is a bit mask"
        c = c + jnp.where(((delta & (d - 1)) == 0) & (delta <= window), 1.0, 0.0)
    return c


def _shift_cache(src, new, dst_ref, hh, n_new):
    buf = src.shape[-1]
    tail_lane = lax.broadcasted_iota(jnp.int32, (HEAD_DIM, LANES), 1)
    rolled = pltpu.roll(src, buf - n_new, 1)
    tail = rolled[:, buf - LANES:]
    placed = pltpu.roll(new, LANES - n_new, 1)
    dst_ref[0, hh, :, 0:buf - LANES] = rolled[:, 0:buf - LANES]
    dst_ref[0, hh, :, buf - LANES:buf] = jnp.where(tail_lane >= LANES - n_new, placed, tail)


def _sample_scores(q_ref, kn_ref, ck_ref, pc_ref, pn_ref, ok_ref, n_new):
    buf = ck_ref.shape[-1]
    q_rows = q_ref.shape[2]
    qi = lax.broadcasted_iota(jnp.int32, (q_rows, buf), 0)
    cc = _multiplicity(buf + qi - lax.broadcasted_iota(jnp.int32, (q_rows, buf), 1))
    qn = lax.broadcasted_iota(jnp.int32, (q_rows, LANES), 0)
    nl = lax.broadcasted_iota(jnp.int32, (q_rows, LANES), 1)
    cn = jnp.where(nl <= qn, _multiplicity(qn - jnp.minimum(nl, qn)), 0.0)

    for hh in range(ck_ref.shape[1]):
        kc = ck_ref[0, hh]
        knt = kn_ref[0, hh]
        qh = (q_ref[0, hh] * ATTN_SCALE).astype(BF16)
        sc = jnp.dot(qh, kc.astype(BF16), preferred_element_type=F32)
        sn = jnp.dot(qh, knt.astype(BF16), preferred_element_type=F32)
        sc = jnp.where(cc > 0, sc, NEG)
        sn = jnp.where(cn > 0, sn, NEG)
        m = jnp.maximum(jnp.max(sc, axis=1, keepdims=True), jnp.max(sn, axis=1, keepdims=True))
        pc = cc * jnp.exp(sc - m)
        pn = cn * jnp.exp(sn - m)
        den = jnp.sum(pc, axis=1, keepdims=True) + jnp.sum(pn, axis=1, keepdims=True)
        pc_ref[0, hh] = pc / den
        pn_ref[0, hh] = pn / den
        _shift_cache(kc, knt, ok_ref, hh, n_new)


def _sample_values(pc_ref, pn_ref, vn_ref, cv_ref, o_ref, ov_ref, n_new):
    nt = (((1,), (1,)), ((), ()))
    for hh in range(cv_ref.shape[1]):
        vc = cv_ref[0, hh]
        vnt = vn_ref[0, hh]
        o_ref[0, hh] = (
            lax.dot_general(pc_ref[0, hh].astype(BF16), vc.astype(BF16), nt, preferred_element_type=F32)
            + lax.dot_general(pn_ref[0, hh].astype(BF16), vnt.astype(BF16), nt, preferred_element_type=F32))
        _shift_cache(vc, vnt, ov_ref, hh, n_new)


def _hosted_heads(grid, head_groups, heads_per_step):
    def spec(*tail):
        def index(*ids):
            step = 0
            for size, idx in zip(grid, ids, strict=True):
                step = step * size + idx
            return (step // head_groups, step % head_groups, 0, 0)
        return pl.BlockSpec((1, heads_per_step) + tail, index)
    return spec


def _proj_sample_kernel(x_ref, g_ref, w_ref, z_ref, wbf_ref):
    xn = _rms(x_ref[...], g_ref[...]).astype(BF16)
    w = w_ref[...].astype(BF16)
    wbf_ref[...] = w
    z_ref[...] = jnp.dot(xn, w, preferred_element_type=F32)


def _project_sample(x_tb, g, w):
    rows = x_tb.shape[0]
    cols = pl.BlockSpec((D_MODEL, CAST_COLS), lambda c: (0, c))
    return pl.pallas_call(
        _proj_sample_kernel,
        grid=(IN_WIDTH // CAST_COLS,),
        in_specs=[pl.BlockSpec((rows, D_MODEL), lambda c: (0, 0)),
                  pl.BlockSpec((1, D_MODEL), lambda c: (0, 0)),
                  cols],
        out_specs=[pl.BlockSpec((rows, CAST_COLS), lambda c: (0, c)), cols],
        out_shape=[jax.ShapeDtypeStruct((rows, IN_WIDTH), F32),
                   jax.ShapeDtypeStruct((D_MODEL, IN_WIDTH), BF16)],
        compiler_params=pltpu.CompilerParams(dimension_semantics=("arbitrary",),
                                             vmem_limit_bytes=VMEM_LIMIT),
        name="proj_sample",
    )(x_tb, g, w)


def _proj_kernel(x_ref, g_ref, w_ref, qs_ref, kn_ref, ck_ref,
                 q_ref, k_ref, v_ref, sg_ref, u_ref, sgp_ref, kt_ref, vt_ref, pc_ref, pn_ref, ok_ref, *, n_new):
    _sample_scores(qs_ref, kn_ref, ck_ref, pc_ref, pn_ref, ok_ref, n_new)

    xn = _rms(x_ref[0], g_ref[...]).astype(BF16)

    def part(c):
        return jnp.dot(xn, w_ref[:, c * ATTN_WIDTH:(c + 1) * ATTN_WIDTH], preferred_element_type=F32)

    k = part(1)
    k_ref[0] = k
    kt_ref[0] = k.T
    sg_ref[0] = _silu(part(3)).astype(BF16)
    v = part(2)
    v_ref[0] = v
    vt_ref[0] = v.T
    sgp_ref[0] = _silu(part(5)).astype(BF16)
    u_ref[0] = part(4)
    q_ref[0] = part(0) * (ATTN_SCALE * LOG2E)


def _project_prompt_score_sample(x, g, w_bf, q_s, knt, ck, n_new):
    b, s, _ = x.shape
    n_batch, h, dh, buf = ck.shape
    q_rows = q_s.shape[2]
    tiles = s // PROJ_ROWS
    hosted = n_batch * h // (b * tiles)
    assert hosted * b * tiles == n_batch * h and h % hosted == 0, "sample heads split evenly over the grid"
    first_kept = (s - MAX_WINDOW) // PROJ_ROWS
    tok = lambda width: pl.BlockSpec((1, PROJ_ROWS, width), lambda bi, i: (bi, i, 0))
    feat = pl.BlockSpec((1, ATTN_WIDTH, PROJ_ROWS), lambda bi, i: (bi, 0, jnp.maximum(i - first_kept, 0)))
    heads = _hosted_heads((b, tiles), h // hosted, hosted)
    f32 = lambda *shape: jax.ShapeDtypeStruct(shape, F32)
    bf16 = lambda *shape: jax.ShapeDtypeStruct(shape, BF16)
    return pl.pallas_call(
        functools.partial(_proj_kernel, n_new=n_new),
        grid=(b, tiles),
        in_specs=[tok(D_MODEL),
                  pl.BlockSpec((1, D_MODEL), lambda bi, i: (0, 0)),
                  pl.BlockSpec((D_MODEL, IN_WIDTH), lambda bi, i: (0, 0), pipeline_mode=pl.Buffered(1)),
                  heads(q_rows, dh), heads(dh, LANES), heads(dh, buf)],
        out_specs=[tok(ATTN_WIDTH)] * 6 + [feat, feat,
                   heads(q_rows, buf), heads(q_rows, LANES), heads(dh, buf)],
        out_shape=[f32(b, s, ATTN_WIDTH)] * 3
        + [bf16(b, s, ATTN_WIDTH), f32(b, s, POOL_WIDTH), bf16(b, s, POOL_WIDTH)]
        + [f32(b, ATTN_WIDTH, MAX_WINDOW)] * 2
        + [f32(n_batch, h, q_rows, buf), f32(n_batch, h, q_rows, LANES), f32(n_batch, h, dh, buf)],
        compiler_params=pltpu.CompilerParams(dimension_semantics=("arbitrary", "arbitrary"),
                                             vmem_limit_bytes=VMEM_LIMIT),
        name="proj_prompt_score_sample",
    )(x, g, w_bf, q_s, knt, ck)


def _attn_kernel(q_ref, k_ref, v_ref, sg_ref, o_ref, acc, mx, sm, qsel, kmask):
    seq = q_ref.shape[0]

    kk = lax.broadcasted_iota(jnp.int32, (2 * BLOCK, BLOCK), 0)
    qq = lax.broadcasted_iota(jnp.int32, (2 * BLOCK, BLOCK), 1)
    band = (kk - qq >= 0) & (kk - qq <= BLOCK)
    kmask[0] = jnp.where(band, 0.0, NEG).astype(BF16)
    kmask[1] = jnp.where(band & (kk >= BLOCK), 0.0, NEG).astype(BF16)
    row = lax.broadcasted_iota(jnp.int32, (HEAD_PAIR * BLOCK, BLOCK), 0) % BLOCK
    col = lax.broadcasted_iota(jnp.int32, (HEAD_PAIR * BLOCK, BLOCK), 1)
    qsel[...] = jnp.where(row == col, 1.0, 0.0).astype(BF16)

    lane = lax.broadcasted_iota(jnp.int32, (BLOCK, LANES), 1)
    low = lane < HEAD_DIM
    ones = jnp.ones((2 * BLOCK, LANES), BF16)
    nt = (((1,), (1,)), ((), ()))

    def keys(ref, qs, d):
        span = d * BLOCK
        if qs >= span:
            return ref[pl.ds(qs - span, 2 * BLOCK, stride=d), :]
        newer = ref[pl.ds(qs, BLOCK, stride=d), :]
        return jnp.concatenate([newer, newer], axis=0)

    blocks = [[(pat, d, r, j) for r in range(d) for j in range(seq // (d * BLOCK))]
              for pat, (_, d) in enumerate(DILATED_PATTERNS)]
    for pat, d, r, j in [blk for group in zip(*reversed(blocks), strict=True) for blk in group]:
        qs = r + d * BLOCK * j
        rows = pl.ds(qs, BLOCK, stride=d)
        qb = q_ref[rows, :]
        q2 = jnp.concatenate([jnp.where(low, qb, 0.0), jnp.where(low, 0.0, qb)], axis=0).astype(BF16)
        kb = keys(k_ref, qs, d).astype(BF16)
        vb = keys(v_ref, qs, d).astype(BF16)
        s = lax.dot_general(jnp.concatenate([q2, qsel[...]], axis=1),
                            jnp.concatenate([kb, kmask[1 if j == 0 else 0]], axis=1), nt,
                            preferred_element_type=F32)
        m = jnp.max(s, axis=1, keepdims=True)
        p = jnp.exp2(s - m)
        ol = jnp.dot(p.astype(BF16), jnp.concatenate([vb, ones], axis=1),
                     preferred_element_type=F32)
        acc[pat, rows, :] = jnp.where(low, ol[:BLOCK, :LANES], ol[BLOCK:, :LANES])
        mx[pat, rows, :] = jnp.where(low, m[:BLOCK], m[BLOCK:])
        sm[pat, rows, :] = jnp.where(low, ol[:BLOCK, LANES:], ol[BLOCK:, LANES:])

    m_all = jnp.maximum(jnp.maximum(mx[0], mx[1]), mx[2])
    num = jnp.zeros((seq, LANES), F32)
    den = jnp.zeros((seq, LANES), F32)
    for pat in range(len(DILATED_PATTERNS)):
        w = jnp.exp2(mx[pat] - m_all)
        num = num + w * acc[pat]
        den = den + w * sm[pat]
    o_ref[...] = (num / den).astype(BF16) * sg_ref[...]


def _attend_prompt(q, k, v, sg):
    b, s, _ = q.shape
    whole = pl.BlockSpec((None, s, LANES), lambda bi, hp: (bi, 0, hp))
    n_pat = len(DILATED_PATTERNS)
    return pl.pallas_call(
        _attn_kernel,
        grid=(b, N_HEADS // HEAD_PAIR),
        in_specs=[whole] * 4,
        out_specs=whole,
        out_shape=jax.ShapeDtypeStruct((b, s, ATTN_WIDTH), BF16),
        scratch_shapes=[pltpu.VMEM((n_pat, s, LANES), F32),
                        pltpu.VMEM((n_pat, s, LANES), F32),
                        pltpu.VMEM((n_pat, s, LANES), F32),
                        pltpu.VMEM((HEAD_PAIR * BLOCK, BLOCK), BF16),
                        pltpu.VMEM((2, 2 * BLOCK, BLOCK), BF16)],
        compiler_params=pltpu.CompilerParams(dimension_semantics=("arbitrary",) * 2,
                                             vmem_limit_bytes=VMEM_LIMIT),
        name="attn_prompt",
    )(q, k, v, sg)


def _merge_kernel(x_ref, a_ref, u_ref, uh_ref, sgp_ref, wp_ref, ps_ref, wo_ref, fg_ref,
                  pc_ref, pn_ref, vn_ref, cv_ref, y_ref, os_ref, ov_ref, ue, *, n_new):
    i = pl.program_id(1)
    rows = u_ref.shape[1]
    ue[0:HIST_ROWS] = jnp.where(i == 0, 0.0, uh_ref[0])
    ue[HIST_ROWS:HIST_ROWS + rows] = u_ref[0]
    pos = i * rows + lax.broadcasted_iota(jnp.int32, (rows, 1), 0)
    counts = [jnp.minimum(pos + 1, w).astype(F32) for w in POOL_WINDOWS]
    pooled = _pooled(lambda g, j: ue[pl.ds(HIST_ROWS - j, rows), _group_cols(g)], counts)
    pool = _pool_project(lambda g: pooled[g].astype(BF16), wp_ref, ps_ref[...])
    mix = jnp.concatenate([a_ref[0], (pool * sgp_ref[0].astype(F32)).astype(BF16)], axis=-1)
    h = x_ref[0] + jnp.dot(mix, wo_ref[...], preferred_element_type=F32)
    y_ref[0] = _rms(h, fg_ref[...])
    _sample_values(pc_ref, pn_ref, vn_ref, cv_ref, os_ref, ov_ref, n_new)


def _merge_prompt_value_sample(x, mix_a, u, sgp, wp_bf, pool_scale, wo_bf, final_g, pc, pn, vnt, cv, n_new):
    b, s, _ = x.shape
    n_batch, h, dh, buf = cv.shape
    q_rows = pc.shape[2]
    tiles = s // MERGE_ROWS
    hosted = n_batch * h // (b * tiles)
    assert hosted * b * tiles == n_batch * h and h % hosted == 0, "sample heads split evenly over the grid"
    per_tile = MERGE_ROWS // HIST_ROWS
    tok = lambda width: pl.BlockSpec((1, MERGE_ROWS, width), lambda bi, i: (bi, i, 0))
    const = lambda shape: pl.BlockSpec(shape, lambda bi, i: (0,) * len(shape))
    heads = _hosted_heads((b, tiles), h // hosted, hosted)
    return pl.pallas_call(
        functools.partial(_merge_kernel, n_new=n_new),
        grid=(b, tiles),
        in_specs=[tok(D_MODEL), tok(ATTN_WIDTH), tok(POOL_WIDTH),
                  pl.BlockSpec((1, HIST_ROWS, POOL_WIDTH),
                               lambda bi, i: (bi, jnp.maximum(i * per_tile - 1, 0), 0)),
                  tok(POOL_WIDTH),
                  const((len(POOL_WINDOWS), POOL_GROUP_WIDTH, POOL_GROUP_WIDTH)),
                  const((1, POOL_WIDTH)),
                  pl.BlockSpec((D_MODEL, D_MODEL), lambda bi, i: (0, 0), pipeline_mode=pl.Buffered(1)),
                  const((1, D_MODEL)),
                  heads(q_rows, buf), heads(q_rows, LANES), heads(dh, LANES), heads(dh, buf)],
        out_specs=[tok(D_MODEL), heads(q_rows, dh), heads(dh, buf)],
        out_shape=[jax.ShapeDtypeStruct((b, s, D_MODEL), F32),
                   jax.ShapeDtypeStruct((n_batch, h, q_rows, dh), F32),
                   jax.ShapeDtypeStruct((n_batch, h, dh, buf), F32)],
        scratch_shapes=[pltpu.VMEM((HIST_ROWS + MERGE_ROWS, POOL_WIDTH), F32)],
        compiler_params=pltpu.CompilerParams(dimension_semantics=("arbitrary", "arbitrary"),
                                             vmem_limit_bytes=VMEM_LIMIT),
        name="merge_prompt_value_sample",
    )(x, mix_a, u, u, sgp, wp_bf, pool_scale, wo_bf, final_g, pc, pn, vnt, cv)


def _merge_sample_kernel(x_ref, o_ref, z_ref, hist_ref, wp_ref, ps_ref, wo_ref, fg_ref, y_ref, pool_ref, ue,
                         *, n_new, n_batch):
    u_col = 4 * ATTN_WIDTH
    ue[0:POOL_HIST] = hist_ref[...]
    ue[POOL_HIST:POOL_HIST + n_new] = z_ref[:, u_col:u_col + POOL_WIDTH].reshape(n_new, n_batch, POOL_WIDTH)
    pool_ref[...] = ue[n_new:n_new + POOL_HIST]

    rows = n_new * n_batch

    def window_sum(g, j):
        return ue[pl.ds(POOL_HIST - j, n_new), :, _group_cols(g)].reshape(rows, POOL_GROUP_WIDTH)

    pos = PAST_LEN + lax.broadcasted_iota(jnp.int32, (rows, 1), 0) // n_batch
    counts = [jnp.minimum(pos + 1, w).astype(F32) for w in POOL_WINDOWS]
    pooled = _pooled(window_sum, counts)
    pool = _pool_project(lambda g: pooled[g].astype(BF16), wp_ref, ps_ref[...])
    ga = z_ref[:, 3 * ATTN_WIDTH:4 * ATTN_WIDTH]
    gp = z_ref[:, u_col + POOL_WIDTH:u_col + 2 * POOL_WIDTH]
    mix = jnp.concatenate([o_ref[...] * _silu(ga), pool * _silu(gp)], axis=-1).astype(BF16)
    h = x_ref[...] + jnp.dot(mix, wo_ref[...], preferred_element_type=F32)
    y_ref[...] = _rms(h, fg_ref[...])


def _merge_sample(x_tb, o_tb, z_tb, hist, wp_bf, pool_scale, wo_bf, final_g, n_new, n_batch):
    return pl.pallas_call(
        functools.partial(_merge_sample_kernel, n_new=n_new, n_batch=n_batch),
        out_shape=[jax.ShapeDtypeStruct(x_tb.shape, F32),
                   jax.ShapeDtypeStruct((POOL_HIST, n_batch, POOL_WIDTH), F32)],
        scratch_shapes=[pltpu.VMEM((POOL_HIST + n_new, n_batch, POOL_WIDTH), F32)],
        compiler_params=pltpu.CompilerParams(vmem_limit_bytes=VMEM_LIMIT),
        name="merge_sample",
    )(x_tb, o_tb, z_tb, hist, wp_bf, pool_scale, wo_bf, final_g)


def _to_feature_major(z_cols, n_new, n_batch):
    t = z_cols.reshape(n_new, n_batch, N_HEADS, HEAD_DIM).transpose(1, 2, 3, 0)
    return jnp.pad(t, ((0, 0), (0, 0), (0, 0), (0, LANES - n_new)))


def kernel(x_prompt, x_sample, cache_k, cache_v, state_pool, norm_g, w_in, w_pool, pool_scale, w_out, final_norm_g):
    depth = norm_g.shape[0]
    assert depth == 1, "single-layer step"
    b, s, _ = x_prompt.shape
    n_batch, n_new, _ = x_sample.shape
    buf = cache_k.shape[2]
    assert s % SUPER == 0 and s >= MAX_WINDOW and buf == MAX_WINDOW and n_new <= POOL_HIST

    g = norm_g[0][None, :]
    fg = final_norm_g[None, :]
    wo_bf = w_out[0].astype(BF16)
    wp_bf = w_pool[0].astype(BF16)
    ps = pool_scale[0][None, :]

    x_tb = x_sample.transpose(1, 0, 2).reshape(n_new * n_batch, D_MODEL)
    z_tb, w_bf = _project_sample(x_tb, g, w_in[0])
    q_s = z_tb[:, 0:ATTN_WIDTH].reshape(n_new, n_batch, N_HEADS, HEAD_DIM).transpose(1, 2, 0, 3)
    q_s = jnp.pad(q_s, ((0, 0), (0, 0), (0, SUBLANES - n_new), (0, 0)))
    knt = _to_feature_major(z_tb[:, ATTN_WIDTH:2 * ATTN_WIDTH], n_new, n_batch)
    vnt = _to_feature_major(z_tb[:, 2 * ATTN_WIDTH:3 * ATTN_WIDTH], n_new, n_batch)
    ck = cache_k[0].transpose(0, 2, 3, 1)
    cv = cache_v[0].transpose(0, 2, 3, 1)

    q, k, v, sg, u, sgp, kt, vt, pc, pn, ok = _project_prompt_score_sample(x_prompt, g, w_bf, q_s, knt, ck, n_new)
    mix_a = _attend_prompt(q, k, v, sg)
    y_prompt, o_s, ov = _merge_prompt_value_sample(x_prompt, mix_a, u, sgp, wp_bf, ps, wo_bf, fg,
                                                   pc, pn, vnt, cv, n_new)
    to_window = lambda t: t.reshape(b, N_HEADS, HEAD_DIM, MAX_WINDOW).transpose(0, 3, 1, 2)[None]
    k_prompt, v_prompt = to_window(kt), to_window(vt)
    pool_prompt = u[:, s - POOL_HIST:][None]

    o_tb = o_s[:, :, :n_new].transpose(2, 0, 1, 3).reshape(n_new * n_batch, ATTN_WIDTH)
    hist = state_pool[0].transpose(1, 0, 2)
    y_tb, pool_tb = _merge_sample(x_tb, o_tb, z_tb, hist, wp_bf, ps, wo_bf, fg, n_new, n_batch)
    y_sample = y_tb.reshape(n_new, n_batch, D_MODEL).transpose(1, 0, 2)
    k_sample = ok.transpose(0, 3, 1, 2)[None]
    v_sample = ov.transpose(0, 3, 1, 2)[None]
    pool_sample = pool_tb.transpose(1, 0, 2)[None]

    return (y_prompt, y_sample, k_prompt, v_prompt, pool_prompt, k_sample, v_sample, pool_sample)
```

```python
import functools

import jax
import jax.numpy as jnp
from jax import lax
from jax.experimental import pallas as pl
from jax.experimental.pallas import tpu as pltpu

F32 = jnp.float32
BF16 = jnp.bfloat16

D_MODEL = 2048
ATTN_WIDTH = 1024
HEAD_DIM = 64
N_HEADS = 16
POOL_WIDTH = 1024
POOL_WINDOWS = (2, 4, 8, 16)
POOL_GROUP_WIDTH = 256
POOL_HIST = 15
IN_WIDTH = 6144
DILATED_PATTERNS = ((128, 1), (512, 4), (2048, 16))
MAX_WINDOW = 2048
BLOCK = 128
PAST_LEN = 16384
EPS = 1e-6
ATTN_SCALE = HEAD_DIM ** -0.5
NEG = -1e30
LOG2E = 1.4426950408889634

LANES = 128
SUBLANES = 8
HEAD_PAIR = LANES // HEAD_DIM
HIST_ROWS = 16

PROJ_ROWS = 256
MERGE_ROWS = 512
SUPER = 2048
CAST_COLS = 2048
VMEM_LIMIT = 60 * 1024 * 1024


def _silu(x):
    return x / (1.0 + jnp.exp(-x))


def _rms(x, g):
    ms = jnp.mean(x * x, axis=-1, keepdims=True)
    return x * lax.rsqrt(ms + EPS) * g


def _group_cols(g):
    return slice(g * POOL_GROUP_WIDTH, (g + 1) * POOL_GROUP_WIDTH)


def _pooled(window_sum, counts):
    outs = []
    for g, w in enumerate(POOL_WINDOWS):
        cur = window_sum(g, 0)
        tot = cur
        for j in range(1, w):
            tot = tot + window_sum(g, j)
        outs.append(tot / counts[g] - cur)
    return outs


def _pool_project(pooled_group, wp_ref, scale):
    outs = [jnp.dot(pooled_group(g), wp_ref[g], preferred_element_type=F32) for g in range(len(POOL_WINDOWS))]
    return jnp.concatenate(outs, axis=-1) * scale


def _multiplicity(delta):
    c = jnp.zeros(delta.shape, F32)
    for window, d in DILATED_PATTERNS:
        assert d & (d - 1) == 0, "power-of-two dilation: residue test is a bit mask"
        c = c + jnp.where(((delta & (d - 1)) == 0) & (delta <= window), 1.0, 0.0)
    return c


def _shift_cache(src, new, dst_ref, hh, n_new):
    buf = src.shape[-1]
    tail_lane = lax.broadcasted_iota(jnp.int32, (HEAD_DIM, LANES), 1)
    rolled = pltpu.roll(src, buf - n_new, 1)
    tail = rolled[:, buf - LANES:]
    placed = pltpu.roll(new, LANES - n_new, 1)
    dst_ref[0, hh, :, 0:buf - LANES] = rolled[:, 0:buf - LANES]
    dst_ref[0, hh, :, buf - LANES:buf] = jnp.where(tail_lane >= LANES - n_new, placed, tail)


def _sample_scores(q_ref, kn_ref, ck_ref, pc_ref, pn_ref, ok_ref, n_new):
    buf = ck_ref.shape[-1]
    q_rows = q_ref.shape[2]
    qi = lax.broadcasted_iota(jnp.int32, (q_rows, buf), 0)
    cc = _multiplicity(buf + qi - lax.broadcasted_iota(jnp.int32, (q_rows, buf), 1))
    qn = lax.broadcasted_iota(jnp.int32, (q_rows, LANES), 0)
    nl = lax.broadcasted_iota(jnp.int32, (q_rows, LANES), 1)
    cn = jnp.where(nl <= qn, _multiplicity(qn - jnp.minimum(nl, qn)), 0.0)

    for hh in range(ck_ref.shape[1]):
        kc = ck_ref[0, hh]
        knt = kn_ref[0, hh]
        qh = (q_ref[0, hh] * ATTN_SCALE).astype(BF16)
        sc = jnp.dot(qh, kc.astype(BF16), preferred_element_type=F32)
        sn = jnp.dot(qh, knt.astype(BF16), preferred_element_type=F32)
        sc = jnp.where(cc > 0, sc, NEG)
        sn = jnp.where(cn > 0, sn, NEG)
        m = jnp.maximum(jnp.max(sc, axis=1, keepdims=True), jnp.max(sn, axis=1, keepdims=True))
        pc = cc * jnp.exp(sc - m)
        pn = cn * jnp.exp(sn - m)
        den = jnp.sum(pc, axis=1, keepdims=True) + jnp.sum(pn, axis=1, keepdims=True)
        pc_ref[0, hh] = pc / den
        pn_ref[0, hh] = pn / den
        _shift_cache(kc, knt, ok_ref, hh, n_new)


def _sample_values(pc_ref, pn_ref, vn_ref, cv_ref, o_ref, ov_ref, n_new):
    nt = (((1,), (1,)), ((), ()))
    for hh in range(cv_ref.shape[1]):
        vc = cv_ref[0, hh]
        vnt = vn_ref[0, hh]
        o_ref[0, hh] = (
            lax.dot_general(pc_ref[0, hh].astype(BF16), vc.astype(BF16), nt, preferred_element_type=F32)
            + lax.dot_general(pn_ref[0, hh].astype(BF16), vnt.astype(BF16), nt, preferred_element_type=F32))
        _shift_cache(vc, vnt, ov_ref, hh, n_new)


def _hosted_heads(grid, head_groups, heads_per_step):
    def spec(*tail):
        def index(*ids):
            step = 0
            for size, idx in zip(grid, ids, strict=True):
                step = step * size + idx
            return (step // head_groups, step % head_groups, 0, 0)
        return pl.BlockSpec((1, heads_per_step) + tail, index)
    return spec


def _proj_sample_kernel(x_ref, g_ref, w_ref, z_ref, wbf_ref):
    xn = _rms(x_ref[...], g_ref[...]).astype(BF16)
    w = w_ref[...].astype(BF16)
    wbf_ref[...] = w
    z_ref[...] = jnp.dot(xn, w, preferred_element_type=F32)


def _project_sample(x_tb, g, w):
    rows = x_tb.shape[0]
    cols = pl.BlockSpec((D_MODEL, CAST_COLS), lambda c: (0, c))
    return pl.pallas_call(
        _proj_sample_kernel,
        grid=(IN_WIDTH // CAST_COLS,),
        in_specs=[pl.BlockSpec((rows, D_MODEL), lambda c: (0, 0)),
                  pl.BlockSpec((1, D_MODEL), lambda c: (0, 0)),
                  cols],
        out_specs=[pl.BlockSpec((rows, CAST_COLS), lambda c: (0, c)), cols],
        out_shape=[jax.ShapeDtypeStruct((rows, IN_WIDTH), F32),
                   jax.ShapeDtypeStruct((D_MODEL, IN_WIDTH), BF16)],
        compiler_params=pltpu.CompilerParams(dimension_semantics=("arbitrary",),
                                             vmem_limit_bytes=VMEM_LIMIT),
        name="proj_sample",
    )(x_tb, g, w)


def _proj_kernel(x_ref, g_ref, w_ref, qs_ref, kn_ref, ck_ref,
                 q_ref, k_ref, v_ref, sg_ref, u_ref, sgp_ref, kt_ref, vt_ref, pc_ref, pn_ref, ok_ref, *, n_new):
    _sample_scores(qs_ref, kn_ref, ck_ref, pc_ref, pn_ref, ok_ref, n_new)

    xn = _rms(x_ref[0], g_ref[...]).astype(BF16)

    def part(c):
        return jnp.dot(xn, w_ref[:, c * ATTN_WIDTH:(c + 1) * ATTN_WIDTH], preferred_element_type=F32)

    k = part(1)
    k_ref[0] = k
    kt_ref[0] = k.T
    sg_ref[0] = _silu(part(3)).astype(BF16)
    v = part(2)
    v_ref[0] = v
    vt_ref[0] = v.T
    sgp_ref[0] = _silu(part(5)).astype(BF16)
    u_ref[0] = part(4)
    q_ref[0] = part(0) * (ATTN_SCALE * LOG2E)


def _project_prompt_score_sample(x, g, w_bf, q_s, knt, ck, n_new):
    b, s, _ = x.shape
    n_batch, h, dh, buf = ck.shape
    q_rows = q_s.shape[2]
    tiles = s // PROJ_ROWS
    hosted = n_batch * h // (b * tiles)
    assert hosted * b * tiles == n_batch * h and h % hosted == 0, "sample heads split evenly over the grid"
    first_kept = (s - MAX_WINDOW) // PROJ_ROWS
    tok = lambda width: pl.BlockSpec((1, PROJ_ROWS, width), lambda bi, i: (bi, i, 0))
    feat = pl.BlockSpec((1, ATTN_WIDTH, PROJ_ROWS), lambda bi, i: (bi, 0, jnp.maximum(i - first_kept, 0)))
    heads = _hosted_heads((b, tiles), h // hosted, hosted)
    f32 = lambda *shape: jax.ShapeDtypeStruct(shape, F32)
    bf16 = lambda *shape: jax.ShapeDtypeStruct(shape, BF16)
    return pl.pallas_call(
        functools.partial(_proj_kernel, n_new=n_new),
        grid=(b, tiles),
        in_specs=[tok(D_MODEL),
                  pl.BlockSpec((1, D_MODEL), lambda bi, i: (0, 0)),
                  pl.BlockSpec((D_MODEL, IN_WIDTH), lambda bi, i: (0, 0), pipeline_mode=pl.Buffered(1)),
                  heads(q_rows, dh), heads(dh, LANES), heads(dh, buf)],
        out_specs=[tok(ATTN_WIDTH)] * 6 + [feat, feat,
                   heads(q_rows, buf), heads(q_rows, LANES), heads(dh, buf)],
        out_shape=[f32(b, s, ATTN_WIDTH)] * 3
        + [bf16(b, s, ATTN_WIDTH), f32(b, s, POOL_WIDTH), bf16(b, s, POOL_WIDTH)]
        + [f32(b, ATTN_WIDTH, MAX_WINDOW)] * 2
        + [f32(n_batch, h, q_rows, buf), f32(n_batch, h, q_rows, LANES), f32(n_batch, h, dh, buf)],
        compiler_params=pltpu.CompilerParams(dimension_semantics=("arbitrary", "arbitrary"),
                                             vmem_limit_bytes=VMEM_LIMIT),
        name="proj_prompt_score_sample",
    )(x, g, w_bf, q_s, knt, ck)


def _attn_kernel(q_ref, k_ref, v_ref, sg_ref, o_ref, acc, mx, sm, qsel, kmask):
    seq = q_ref.shape[0]

    kk = lax.broadcasted_iota(jnp.int32, (2 * BLOCK, BLOCK), 0)
    qq = lax.broadcasted_iota(jnp.int32, (2 * BLOCK, BLOCK), 1)
    band = (kk - qq >= 0) & (kk - qq <= BLOCK)
    kmask[0] = jnp.where(band, 0.0, NEG).astype(BF16)
    kmask[1] = jnp.where(band & (kk >= BLOCK), 0.0, NEG).astype(BF16)
    row = lax.broadcasted_iota(jnp.int32, (HEAD_PAIR * BLOCK, BLOCK), 0) % BLOCK
    col = lax.broadcasted_iota(jnp.int32, (HEAD_PAIR * BLOCK, BLOCK), 1)
    qsel[...] = jnp.where(row == col, 1.0, 0.0).astype(BF16)

    lane = lax.broadcasted_iota(jnp.int32, (BLOCK, LANES), 1)
    low = lane < HEAD_DIM
    ones = jnp.ones((2 * BLOCK, LANES), BF16)
    nt = (((1,), (1,)), ((), ()))

    def keys(ref, qs, d):
        span = d * BLOCK
        if qs >= span:
            return ref[pl.ds(qs - span, 2 * BLOCK, stride=d), :]
        newer = ref[pl.ds(qs, BLOCK, stride=d), :]
        return jnp.concatenate([newer, newer], axis=0)

    blocks = [[(pat, d, r, j) for r in range(d) for j in range(seq // (d * BLOCK))]
              for pat, (_, d) in enumerate(DILATED_PATTERNS)]
    for pat, d, r, j in [blk for group in zip(*reversed(blocks), strict=True) for blk in group]:
        qs = r + d * BLOCK * j
        rows = pl.ds(qs, BLOCK, stride=d)
        qb = q_ref[rows, :]
        q2 = jnp.concatenate([jnp.where(low, qb, 0.0), jnp.where(low, 0.0, qb)], axis=0).astype(BF16)
        kb = keys(k_ref, qs, d).astype(BF16)
        vb = keys(v_ref, qs, d).astype(BF16)
        s = lax.dot_general(jnp.concatenate([q2, qsel[...]], axis=1),
                            jnp.concatenate([kb, kmask[1 if j == 0 else 0]], axis=1), nt,
                            preferred_element_type=F32)
        m = jnp.max(s, axis=1, keepdims=True)
        p = jnp.exp2(s - m)
        ol = jnp.dot(p.astype(BF16), jnp.concatenate([vb, ones], axis=1),
                     preferred_element_type=F32)
        acc[pat, rows, :] = jnp.where(low, ol[:BLOCK, :LANES], ol[BLOCK:, :LANES])
        mx[pat, rows, :] = jnp.where(low, m[:BLOCK], m[BLOCK:])
        sm[pat, rows, :] = jnp.where(low, ol[:BLOCK, LANES:], ol[BLOCK:, LANES:])

    m_all = jnp.maximum(jnp.maximum(mx[0], mx[1]), mx[2])
    num = jnp.zeros((seq, LANES), F32)
    den = jnp.zeros((seq, LANES), F32)
    for pat in range(len(DILATED_PATTERNS)):
        w = jnp.exp2(mx[pat] - m_all)
        num = num + w * acc[pat]
        den = den + w * sm[pat]
    o_ref[...] = (num / den).astype(BF16) * sg_ref[...]


def _attend_prompt(q, k, v, sg):
    b, s, _ = q.shape
    whole = pl.BlockSpec((None, s, LANES), lambda bi, hp: (bi, 0, hp))
    n_pat = len(DILATED_PATTERNS)
    return pl.pallas_call(
        _attn_kernel,
        grid=(b, N_HEADS // HEAD_PAIR),
        in_specs=[whole] * 4,
        out_specs=whole,
        out_shape=jax.ShapeDtypeStruct((b, s, ATTN_WIDTH), BF16),
        scratch_shapes=[pltpu.VMEM((n_pat, s, LANES), F32),
                        pltpu.VMEM((n_pat, s, LANES), F32),
                        pltpu.VMEM((n_pat, s, LANES), F32),
                        pltpu.VMEM((HEAD_PAIR * BLOCK, BLOCK), BF16),
                        pltpu.VMEM((2, 2 * BLOCK, BLOCK), BF16)],
        compiler_params=pltpu.CompilerParams(dimension_semantics=("arbitrary",) * 2,
                                             vmem_limit_bytes=VMEM_LIMIT),
        name="attn_prompt",
    )(q, k, v, sg)


def _merge_kernel(x_ref, a_ref, u_ref, uh_ref, sgp_ref, wp_ref, ps_ref, wo_ref, fg_ref,
                  pc_ref, pn_ref, vn_ref, cv_ref, y_ref, os_ref, ov_ref, ue, *, n_new):
    i = pl.program_id(1)
    rows = u_ref.shape[1]
    ue[0:HIST_ROWS] = jnp.where(i == 0, 0.0, uh_ref[0])
    ue[HIST_ROWS:HIST_ROWS + rows] = u_ref[0]
    pos = i * rows + lax.broadcasted_iota(jnp.int32, (rows, 1), 0)
    counts = [jnp.minimum(pos + 1, w).astype(F32) for w in POOL_WINDOWS]
    pooled = _pooled(lambda g, j: ue[pl.ds(HIST_ROWS - j, rows), _group_cols(g)], counts)
    pool = _pool_project(lambda g: pooled[g].astype(BF16), wp_ref, ps_ref[...])
    mix = jnp.concatenate([a_ref[0], (pool * sgp_ref[0].astype(F32)).astype(BF16)], axis=-1)
    h = x_ref[0] + jnp.dot(mix, wo_ref[...], preferred_element_type=F32)
    y_ref[0] = _rms(h, fg_ref[...])
    _sample_values(pc_ref, pn_ref, vn_ref, cv_ref, os_ref, ov_ref, n_new)


def _merge_prompt_value_sample(x, mix_a, u, sgp, wp_bf, pool_scale, wo_bf, final_g, pc, pn, vnt, cv, n_new):
    b, s, _ = x.shape
    n_batch, h, dh, buf = cv.shape
    q_rows = pc.shape[2]
    tiles = s // MERGE_ROWS
    hosted = n_batch * h // (b * tiles)
    assert hosted * b * tiles == n_batch * h and h % hosted == 0, "sample heads split evenly over the grid"
    per_tile = MERGE_ROWS // HIST_ROWS
    tok = lambda width: pl.BlockSpec((1, MERGE_ROWS, width), lambda bi, i: (bi, i, 0))
    const = lambda shape: pl.BlockSpec(shape, lambda bi, i: (0,) * len(shape))
    heads = _hosted_heads((b, tiles), h // hosted, hosted)
    return pl.pallas_call(
        functools.partial(_merge_kernel, n_new=n_new),
        grid=(b, tiles),
        in_specs=[tok(D_MODEL), tok(ATTN_WIDTH), tok(POOL_WIDTH),
                  pl.BlockSpec((1, HIST_ROWS, POOL_WIDTH),
                               lambda bi, i: (bi, jnp.maximum(i * per_tile - 1, 0), 0)),
                  tok(POOL_WIDTH),
                  const((len(POOL_WINDOWS), POOL_GROUP_WIDTH, POOL_GROUP_WIDTH)),
                  const((1, POOL_WIDTH)),
                  pl.BlockSpec((D_MODEL, D_MODEL), lambda bi, i: (0, 0), pipeline_mode=pl.Buffered(1)),
                  const((1, D_MODEL)),
                  heads(q_rows, buf), heads(q_rows, LANES), heads(dh, LANES), heads(dh, buf)],
        out_specs=[tok(D_MODEL), heads(q_rows, dh), heads(dh, buf)],
        out_shape=[jax.ShapeDtypeStruct((b, s, D_MODEL), F32),
                   jax.ShapeDtypeStruct((n_batch, h, q_rows, dh), F32),
                   jax.ShapeDtypeStruct((n_batch, h, dh, buf), F32)],
        scratch_shapes=[pltpu.VMEM((HIST_ROWS + MERGE_ROWS, POOL_WIDTH), F32)],
        compiler_params=pltpu.CompilerParams(dimension_semantics=("arbitrary", "arbitrary"),
                                             vmem_limit_bytes=VMEM_LIMIT),
        name="merge_prompt_value_sample",
    )(x, mix_a, u, u, sgp, wp_bf, pool_scale, wo_bf, final_g, pc, pn, vnt, cv)


def _merge_sample_kernel(x_ref, o_ref, z_ref, hist_ref, wp_ref, ps_ref, wo_ref, fg_ref, y_ref, pool_ref, ue,
                         *, n_new, n_batch):
    u_col = 4 * ATTN_WIDTH
    ue[0:POOL_HIST] = hist_ref[...]
    ue[POOL_HIST:POOL_HIST + n_new] = z_ref[:, u_col:u_col + POOL_WIDTH].reshape(n_new, n_batch, POOL_WIDTH)
    pool_ref[...] = ue[n_new:n_new + POOL_HIST]

    rows = n_new * n_batch

    def window_sum(g, j):
        return ue[pl.ds(POOL_HIST - j, n_new), :, _group_cols(g)].reshape(rows, POOL_GROUP_WIDTH)

    pos = PAST_LEN + lax.broadcasted_iota(jnp.int32, (rows, 1), 0) // n_batch
    counts = [jnp.minimum(pos + 1, w).astype(F32) for w in POOL_WINDOWS]
    pooled = _pooled(window_sum, counts)
    pool = _pool_project(lambda g: pooled[g].astype(BF16), wp_ref, ps_ref[...])
    ga = z_ref[:, 3 * ATTN_WIDTH:4 * ATTN_WIDTH]
    gp = z_ref[:, u_col + POOL_WIDTH:u_col + 2 * POOL_WIDTH]
    mix = jnp.concatenate([o_ref[...] * _silu(ga), pool * _silu(gp)], axis=-1).astype(BF16)
    h = x_ref[...] + jnp.dot(mix, wo_ref[...], preferred_element_type=F32)
    y_ref[...] = _rms(h, fg_ref[...])


def _merge_sample(x_tb, o_tb, z_tb, hist, wp_bf, pool_scale, wo_bf, final_g, n_new, n_batch):
    return pl.pallas_call(
        functools.partial(_merge_sample_kernel, n_new=n_new, n_batch=n_batch),
        out_shape=[jax.ShapeDtypeStruct(x_tb.shape, F32),
                   jax.ShapeDtypeStruct((POOL_HIST, n_batch, POOL_WIDTH), F32)],
        scratch_shapes=[pltpu.VMEM((POOL_HIST + n_new, n_batch, POOL_WIDTH), F32)],
        compiler_params=pltpu.CompilerParams(vmem_limit_bytes=VMEM_LIMIT),
        name="merge_sample",
    )(x_tb, o_tb, z_tb, hist, wp_bf, pool_scale, wo_bf, final_g)


def _to_feature_major(z_cols, n_new, n_batch):
    t = z_cols.reshape(n_new, n_batch, N_HEADS, HEAD_DIM).transpose(1, 2, 3, 0)
    return jnp.pad(t, ((0, 0), (0, 0), (0, 0), (0, LANES - n_new)))


def kernel(x_prompt, x_sample, cache_k, cache_v, state_pool, norm_g, w_in, w_pool, pool_scale, w_out, final_norm_g):
    depth = norm_g.shape[0]
    assert depth == 1, "single-layer step"
    b, s, _ = x_prompt.shape
    n_batch, n_new, _ = x_sample.shape
    buf = cache_k.shape[2]
    assert s % SUPER == 0 and s >= MAX_WINDOW and buf == MAX_WINDOW and n_new <= POOL_HIST

    g = norm_g[0][None, :]
    fg = final_norm_g[None, :]
    wo_bf = w_out[0].astype(BF16)
    wp_bf = w_pool[0].astype(BF16)
    ps = pool_scale[0][None, :]

    x_tb = x_sample.transpose(1, 0, 2).reshape(n_new * n_batch, D_MODEL)
    z_tb, w_bf = _project_sample(x_tb, g, w_in[0])
    q_s = z_tb[:, 0:ATTN_WIDTH].reshape(n_new, n_batch, N_HEADS, HEAD_DIM).transpose(1, 2, 0, 3)
    q_s = jnp.pad(q_s, ((0, 0), (0, 0), (0, SUBLANES - n_new), (0, 0)))
    knt = _to_feature_major(z_tb[:, ATTN_WIDTH:2 * ATTN_WIDTH], n_new, n_batch)
    vnt = _to_feature_major(z_tb[:, 2 * ATTN_WIDTH:3 * ATTN_WIDTH], n_new, n_batch)
    ck = cache_k[0].transpose(0, 2, 3, 1)
    cv = cache_v[0].transpose(0, 2, 3, 1)

    q, k, v, sg, u, sgp, kt, vt, pc, pn, ok = _project_prompt_score_sample(x_prompt, g, w_bf, q_s, knt, ck, n_new)
    mix_a = _attend_prompt(q, k, v, sg)
    y_prompt, o_s, ov = _merge_prompt_value_sample(x_prompt, mix_a, u, sgp, wp_bf, ps, wo_bf, fg,
                                                   pc, pn, vnt, cv, n_new)
    to_window = lambda t: t.reshape(b, N_HEADS, HEAD_DIM, MAX_WINDOW).transpose(0, 3, 1, 2)[None]
    k_prompt, v_prompt = to_window(kt), to_window(vt)
    pool_prompt = u[:, s - POOL_HIST:][None]

    o_tb = o_s[:, :, :n_new].transpose(2, 0, 1, 3).reshape(n_new * n_batch, ATTN_WIDTH)
    hist = state_pool[0].transpose(1, 0, 2)
    y_tb, pool_tb = _merge_sample(x_tb, o_tb, z_tb, hist, wp_bf, ps, wo_bf, fg, n_new, n_batch)
    y_sample = y_tb.reshape(n_new, n_batch, D_MODEL).transpose(1, 0, 2)
    k_sample = ok.transpose(0, 3, 1, 2)[None]
    v_sample = ov.transpose(0, 3, 1, 2)[None]
    pool_sample = pool_tb.transpose(1, 0, 2)[None]

    return (y_prompt, y_sample, k_prompt, v_prompt, pool_prompt, k_sample, v_sample, pool_sample)
```

```python
import functools

import jax
import jax.numpy as jnp
from jax import lax
from jax.experimental import pallas as pl
from jax.experimental.pallas import tpu as pltpu

F32 = jnp.float32
BF16 = jnp.bfloat16

D_MODEL = 2048
ATTN_WIDTH = 1024
HEAD_DIM = 64
N_HEADS = 16
POOL_WIDTH = 1024
POOL_WINDOWS = (2, 4, 8, 16)
POOL_GROUP_WIDTH = 256
POOL_HIST = 15
IN_WIDTH = 6144
DILATED_PATTERNS = ((128, 1), (512, 4), (2048, 16))
MAX_WINDOW = 2048
BLOCK = 128
PAST_LEN = 16384
EPS = 1e-6
ATTN_SCALE = HEAD_DIM ** -0.5
NEG = -1e30
LOG2E = 1.4426950408889634

LANES = 128
SUBLANES = 8
HEAD_PAIR = LANES // HEAD_DIM
HIST_ROWS = 16

PROJ_ROWS = 256
MERGE_ROWS = 512
SUPER = 2048
CAST_COLS = 512
VMEM_LIMIT = 60 * 1024 * 1024


def _silu(x):
    return x / (1.0 + jnp.exp(-x))


def _rms(x, g):
    ms = jnp.mean(x * x, axis=-1, keepdims=True)
    return x * lax.rsqrt(ms + EPS) * g


def _group_cols(g):
    return slice(g * POOL_GROUP_WIDTH, (g + 1) * POOL_GROUP_WIDTH)


def _pooled(window_sum, counts):
    outs = []
    for g, w in enumerate(POOL_WINDOWS):
        cur = window_sum(g, 0)
        tot = cur
        for j in range(1, w):
            tot = tot + window_sum(g, j)
        outs.append(tot / counts[g] - cur)
    return outs


def _pool_project(pooled_group, wp_ref, scale):
    outs = [jnp.dot(pooled_group(g), wp_ref[g], preferred_element_type=F32) for g in range(len(POOL_WINDOWS))]
    return jnp.concatenate(outs, axis=-1) * scale


def _multiplicity(delta):
    c = jnp.zeros(delta.shape, F32)
    for window, d in DILATED_PATTERNS:
        assert d & (d - 1) == 0, "power-of-two dilation: residue test is a bit mask"
        c = c + jnp.where(((delta & (d - 1)) == 0) & (delta <= window), 1.0, 0.0)
    return c


def _shift_cache(src, new, dst_ref, hh, n_new):
    buf = src.shape[-1]
    tail_lane = lax.broadcasted_iota(jnp.int32, (HEAD_DIM, LANES), 1)
    rolled = pltpu.roll(src, buf - n_new, 1)
    tail = rolled[:, buf - LANES:]
    placed = pltpu.roll(new, LANES - n_new, 1)
    dst_ref[0, hh, :, 0:buf - LANES] = rolled[:, 0:buf - LANES]
    dst_ref[0, hh, :, buf - LANES:buf] = jnp.where(tail_lane >= LANES - n_new, placed, tail)


def _sample_scores(q_ref, kn_ref, ck_ref, pc_ref, pn_ref, ok_ref, n_new):
    buf = ck_ref.shape[-1]
    q_rows = q_ref.shape[2]
    qi = lax.broadcasted_iota(jnp.int32, (q_rows, buf), 0)
    cc = _multiplicity(buf + qi - lax.broadcasted_iota(jnp.int32, (q_rows, buf), 1))
    qn = lax.broadcasted_iota(jnp.int32, (q_rows, LANES), 0)
    nl = lax.broadcasted_iota(jnp.int32, (q_rows, LANES), 1)
    cn = jnp.where(nl <= qn, _multiplicity(qn - jnp.minimum(nl, qn)), 0.0)

    for hh in range(ck_ref.shape[1]):
        kc = ck_ref[0, hh]
        knt = kn_ref[0, hh]
        qh = (q_ref[0, hh] * ATTN_SCALE).astype(BF16)
        sc = jnp.dot(qh, kc.astype(BF16), preferred_element_type=F32)
        sn = jnp.dot(qh, knt.astype(BF16), preferred_element_type=F32)
        sc = jnp.where(cc > 0, sc, NEG)
        sn = jnp.where(cn > 0, sn, NEG)
        m = jnp.maximum(jnp.max(sc, axis=1, keepdims=True), jnp.max(sn, axis=1, keepdims=True))
        pc = cc * jnp.exp(sc - m)
        pn = cn * jnp.exp(sn - m)
        den = jnp.sum(pc, axis=1, keepdims=True) + jnp.sum(pn, axis=1, keepdims=True)
        pc_ref[0, hh] = pc / den
        pn_ref[0, hh] = pn / den
        _shift_cache(kc, knt, ok_ref, hh, n_new)


def _sample_values(pc_ref, pn_ref, vn_ref, cv_ref, o_ref, ov_ref, n_new):
    nt = (((1,), (1,)), ((), ()))
    for hh in range(cv_ref.shape[1]):
        vc = cv_ref[0, hh]
        vnt = vn_ref[0, hh]
        o_ref[0, hh] = (
            lax.dot_general(pc_ref[0, hh].astype(BF16), vc.astype(BF16), nt, preferred_element_type=F32)
            + lax.dot_general(pn_ref[0, hh].astype(BF16), vnt.astype(BF16), nt, preferred_element_type=F32))
        _shift_cache(vc, vnt, ov_ref, hh, n_new)


def _hosted_heads(grid, head_groups, heads_per_step):
    def spec(*tail):
        def index(*ids):
            step = 0
            for size, idx in zip(grid, ids, strict=True):
                step = step * size + idx
            return (step // head_groups, step % head_groups, 0, 0)
        return pl.BlockSpec((1, heads_per_step) + tail, index)
    return spec


def _proj_sample_kernel(x_ref, g_ref, w_ref, z_ref, wbf_ref):
    xn = _rms(x_ref[...], g_ref[...]).astype(BF16)
    w = w_ref[...].astype(BF16)
    wbf_ref[...] = w
    z_ref[...] = jnp.dot(xn, w, preferred_element_type=F32)


def _project_sample(x_tb, g, w):
    rows = x_tb.shape[0]
    cols = pl.BlockSpec((D_MODEL, CAST_COLS), lambda c: (0, c))
    return pl.pallas_call(
        _proj_sample_kernel,
        grid=(IN_WIDTH // CAST_COLS,),
        in_specs=[pl.BlockSpec((rows, D_MODEL), lambda c: (0, 0)),
                  pl.BlockSpec((1, D_MODEL), lambda c: (0, 0)),
                  cols],
        out_specs=[pl.BlockSpec((rows, CAST_COLS), lambda c: (0, c)), cols],
        out_shape=[jax.ShapeDtypeStruct((rows, IN_WIDTH), F32),
                   jax.ShapeDtypeStruct((D_MODEL, IN_WIDTH), BF16)],
        compiler_params=pltpu.CompilerParams(dimension_semantics=("arbitrary",),
                                             vmem_limit_bytes=VMEM_LIMIT),
        name="proj_sample",
    )(x_tb, g, w)


def _proj_kernel(x_ref, g_ref, w_ref, qs_ref, kn_ref, ck_ref,
                 q_ref, k_ref, v_ref, sg_ref, u_ref, sgp_ref, kt_ref, vt_ref, pc_ref, pn_ref, ok_ref, *, n_new):
    _sample_scores(qs_ref, kn_ref, ck_ref, pc_ref, pn_ref, ok_ref, n_new)

    xn = _rms(x_ref[0], g_ref[...]).astype(BF16)

    def part(c):
        return jnp.dot(xn, w_ref[:, c * ATTN_WIDTH:(c + 1) * ATTN_WIDTH], preferred_element_type=F32)

    k = part(1)
    k_ref[0] = k
    kt_ref[0] = k.T
    sg_ref[0] = _silu(part(3)).astype(BF16)
    v = part(2)
    v_ref[0] = v
    vt_ref[0] = v.T
    sgp_ref[0] = _silu(part(5)).astype(BF16)
    u_ref[0] = part(4)
    q_ref[0] = part(0) * (ATTN_SCALE * LOG2E)


def _project_prompt_score_sample(x, g, w_bf, q_s, knt, ck, n_new):
    b, s, _ = x.shape
    n_batch, h, dh, buf = ck.shape
    q_rows = q_s.shape[2]
    tiles = s // PROJ_ROWS
    hosted = n_batch * h // (b * tiles)
    assert hosted * b * tiles == n_batch * h and h % hosted == 0, "sample heads split evenly over the grid"
    first_kept = (s - MAX_WINDOW) // PROJ_ROWS
    tok = lambda width: pl.BlockSpec((1, PROJ_ROWS, width), lambda bi, i: (bi, i, 0))
    feat = pl.BlockSpec((1, ATTN_WIDTH, PROJ_ROWS), lambda bi, i: (bi, 0, jnp.maximum(i - first_kept, 0)))
    heads = _hosted_heads((b, tiles), h // hosted, hosted)
    f32 = lambda *shape: jax.ShapeDtypeStruct(shape, F32)
    bf16 = lambda *shape: jax.ShapeDtypeStruct(shape, BF16)
    return pl.pallas_call(
        functools.partial(_proj_kernel, n_new=n_new),
        grid=(b, tiles),
        in_specs=[tok(D_MODEL),
                  pl.BlockSpec((1, D_MODEL), lambda bi, i: (0, 0)),
                  pl.BlockSpec((D_MODEL, IN_WIDTH), lambda bi, i: (0, 0), pipeline_mode=pl.Buffered(1)),
                  heads(q_rows, dh), heads(dh, LANES), heads(dh, buf)],
        out_specs=[tok(ATTN_WIDTH)] * 6 + [feat, feat,
                   heads(q_rows, buf), heads(q_rows, LANES), heads(dh, buf)],
        out_shape=[f32(b, s, ATTN_WIDTH)] * 3
        + [bf16(b, s, ATTN_WIDTH), f32(b, s, POOL_WIDTH), bf16(b, s, POOL_WIDTH)]
        + [f32(b, ATTN_WIDTH, MAX_WINDOW)] * 2
        + [f32(n_batch, h, q_rows, buf), f32(n_batch, h, q_rows, LANES), f32(n_batch, h, dh, buf)],
        compiler_params=pltpu.CompilerParams(dimension_semantics=("arbitrary", "arbitrary"),
                                             vmem_limit_bytes=VMEM_LIMIT),
        name="proj_prompt_score_sample",
    )(x, g, w_bf, q_s, knt, ck)


def _attn_kernel(q_ref, k_ref, v_ref, sg_ref, o_ref, acc, mx, sm, qsel, kmask):
    seq = q_ref.shape[0]

    kk = lax.broadcasted_iota(jnp.int32, (2 * BLOCK, BLOCK), 0)
    qq = lax.broadcasted_iota(jnp.int32, (2 * BLOCK, BLOCK), 1)
    band = (kk - qq >= 0) & (kk - qq <= BLOCK)
    kmask[0] = jnp.where(band, 0.0, NEG).astype(BF16)
    kmask[1] = jnp.where(band & (kk >= BLOCK), 0.0, NEG).astype(BF16)
    row = lax.broadcasted_iota(jnp.int32, (HEAD_PAIR * BLOCK, BLOCK), 0) % BLOCK
    col = lax.broadcasted_iota(jnp.int32, (HEAD_PAIR * BLOCK, BLOCK), 1)
    qsel[...] = jnp.where(row == col, 1.0, 0.0).astype(BF16)

    lane = lax.broadcasted_iota(jnp.int32, (BLOCK, LANES), 1)
    low = lane < HEAD_DIM
    ones = jnp.ones((2 * BLOCK, LANES), BF16)
    nt = (((1,), (1,)), ((), ()))

    def keys(ref, qs, d):
        span = d * BLOCK
        if qs >= span:
            return ref[pl.ds(qs - span, 2 * BLOCK, stride=d), :]
        newer = ref[pl.ds(qs, BLOCK, stride=d), :]
        return jnp.concatenate([newer, newer], axis=0)

    blocks = [[(pat, d, r, j) for r in range(d) for j in range(seq // (d * BLOCK))]
              for pat, (_, d) in enumerate(DILATED_PATTERNS)]
    for pat, d, r, j in [blk for group in zip(*reversed(blocks), strict=True) for blk in group]:
        qs = r + d * BLOCK * j
        rows = pl.ds(qs, BLOCK, stride=d)
        qb = q_ref[rows, :]
        q2 = jnp.concatenate([jnp.where(low, qb, 0.0), jnp.where(low, 0.0, qb)], axis=0).astype(BF16)
        kb = keys(k_ref, qs, d).astype(BF16)
        vb = keys(v_ref, qs, d).astype(BF16)
        s = lax.dot_general(jnp.concatenate([q2, qsel[...]], axis=1),
                            jnp.concatenate([kb, kmask[1 if j == 0 else 0]], axis=1), nt,
                            preferred_element_type=F32)
        m = jnp.max(s, axis=1, keepdims=True)
        p = jnp.exp2(s - m)
        ol = jnp.dot(p.astype(BF16), jnp.concatenate([vb, ones], axis=1),
                     preferred_element_type=F32)
        acc[pat, rows, :] = jnp.where(low, ol[:BLOCK, :LANES], ol[BLOCK:, :LANES])
        mx[pat, rows, :] = jnp.where(low, m[:BLOCK], m[BLOCK:])
        sm[pat, rows, :] = jnp.where(low, ol[:BLOCK, LANES:], ol[BLOCK:, LANES:])

    m_all = jnp.maximum(jnp.maximum(mx[0], mx[1]), mx[2])
    num = jnp.zeros((seq, LANES), F32)
    den = jnp.zeros((seq, LANES), F32)
    for pat in range(len(DILATED_PATTERNS)):
        w = jnp.exp2(mx[pat] - m_all)
        num = num + w * acc[pat]
        den = den + w * sm[pat]
    o_ref[...] = (num / den).astype(BF16) * sg_ref[...]


def _attend_prompt(q, k, v, sg):
    b, s, _ = q.shape
    whole = pl.BlockSpec((None, s, LANES), lambda bi, hp: (bi, 0, hp))
    n_pat = len(DILATED_PATTERNS)
    return pl.pallas_call(
        _attn_kernel,
        grid=(b, N_HEADS // HEAD_PAIR),
        in_specs=[whole] * 4,
        out_specs=whole,
        out_shape=jax.ShapeDtypeStruct((b, s, ATTN_WIDTH), BF16),
        scratch_shapes=[pltpu.VMEM((n_pat, s, LANES), F32),
                        pltpu.VMEM((n_pat, s, LANES), F32),
                        pltpu.VMEM((n_pat, s, LANES), F32),
                        pltpu.VMEM((HEAD_PAIR * BLOCK, BLOCK), BF16),
                        pltpu.VMEM((2, 2 * BLOCK, BLOCK), BF16)],
        compiler_params=pltpu.CompilerParams(dimension_semantics=("arbitrary",) * 2,
                                             vmem_limit_bytes=VMEM_LIMIT),
        name="attn_prompt",
    )(q, k, v, sg)


def _merge_kernel(x_ref, a_ref, u_ref, uh_ref, sgp_ref, wp_ref, ps_ref, wo_ref, fg_ref,
                  pc_ref, pn_ref, vn_ref, cv_ref, y_ref, os_ref, ov_ref, ue, *, n_new):
    i = pl.program_id(1)
    rows = u_ref.shape[1]
    ue[0:HIST_ROWS] = jnp.where(i == 0, 0.0, uh_ref[0])
    ue[HIST_ROWS:HIST_ROWS + rows] = u_ref[0]
    pos = i * rows + lax.broadcasted_iota(jnp.int32, (rows, 1), 0)
    counts = [jnp.minimum(pos + 1, w).astype(F32) for w in POOL_WINDOWS]
    pooled = _pooled(lambda g, j: ue[pl.ds(HIST_ROWS - j, rows), _group_cols(g)], counts)
    pool = _pool_project(lambda g: pooled[g].astype(BF16), wp_ref, ps_ref[...])
    mix = jnp.concatenate([a_ref[0], (pool * sgp_ref[0].astype(F32)).astype(BF16)], axis=-1)
    h = x_ref[0] + jnp.dot(mix, wo_ref[...], preferred_element_type=F32)
    y_ref[0] = _rms(h, fg_ref[...])
    _sample_values(pc_ref, pn_ref, vn_ref, cv_ref, os_ref, ov_ref, n_new)


def _merge_prompt_value_sample(x, mix_a, u, sgp, wp_bf, pool_scale, wo_bf, final_g, pc, pn, vnt, cv, n_new):
    b, s, _ = x.shape
    n_batch, h, dh, buf = cv.shape
    q_rows = pc.shape[2]
    tiles = s // MERGE_ROWS
    hosted = n_batch * h // (b * tiles)
    assert hosted * b * tiles == n_batch * h and h % hosted == 0, "sample heads split evenly over the grid"
    per_tile = MERGE_ROWS // HIST_ROWS
    tok = lambda width: pl.BlockSpec((1, MERGE_ROWS, width), lambda bi, i: (bi, i, 0))
    const = lambda shape: pl.BlockSpec(shape, lambda bi, i: (0,) * len(shape))
    heads = _hosted_heads((b, tiles), h // hosted, hosted)
    return pl.pallas_call(
        functools.partial(_merge_kernel, n_new=n_new),
        grid=(b, tiles),
        in_specs=[tok(D_MODEL), tok(ATTN_WIDTH), tok(POOL_WIDTH),
                  pl.BlockSpec((1, HIST_ROWS, POOL_WIDTH),
                               lambda bi, i: (bi, jnp.maximum(i * per_tile - 1, 0), 0)),
                  tok(POOL_WIDTH),
                  const((len(POOL_WINDOWS), POOL_GROUP_WIDTH, POOL_GROUP_WIDTH)),
                  const((1, POOL_WIDTH)),
                  pl.BlockSpec((D_MODEL, D_MODEL), lambda bi, i: (0, 0), pipeline_mode=pl.Buffered(1)),
                  const((1, D_MODEL)),
                  heads(q_rows, buf), heads(q_rows, LANES), heads(dh, LANES), heads(dh, buf)],
        out_specs=[tok(D_MODEL), heads(q_rows, dh), heads(dh, buf)],
        out_shape=[jax.ShapeDtypeStruct((b, s, D_MODEL), F32),
                   jax.ShapeDtypeStruct((n_batch, h, q_rows, dh), F32),
                   jax.ShapeDtypeStruct((n_batch, h, dh, buf), F32)],
        scratch_shapes=[pltpu.VMEM((HIST_ROWS + MERGE_ROWS, POOL_WIDTH), F32)],
        compiler_params=pltpu.CompilerParams(dimension_semantics=("arbitrary", "arbitrary"),
                                             vmem_limit_bytes=VMEM_LIMIT),
        name="merge_prompt_value_sample",
    )(x, mix_a, u, u, sgp, wp_bf, pool_scale, wo_bf, final_g, pc, pn, vnt, cv)


def _merge_sample_kernel(x_ref, o_ref, z_ref, hist_ref, wp_ref, ps_ref, wo_ref, fg_ref, y_ref, pool_ref, ue,
                         *, n_new, n_batch):
    u_col = 4 * ATTN_WIDTH
    ue[0:POOL_HIST] = hist_ref[...]
    ue[POOL_HIST:POOL_HIST + n_new] = z_ref[:, u_col:u_col + POOL_WIDTH].reshape(n_new, n_batch, POOL_WIDTH)
    pool_ref[...] = ue[n_new:n_new + POOL_HIST]

    rows = n_new * n_batch

    def window_sum(g, j):
        return ue[pl.ds(POOL_HIST - j, n_new), :, _group_cols(g)].reshape(rows, POOL_GROUP_WIDTH)

    pos = PAST_LEN + lax.broadcasted_iota(jnp.int32, (rows, 1), 0) // n_batch
    counts = [jnp.minimum(pos + 1, w).astype(F32) for w in POOL_WINDOWS]
    pooled = _pooled(window_sum, counts)
    pool = _pool_project(lambda g: pooled[g].astype(BF16), wp_ref, ps_ref[...])
    ga = z_ref[:, 3 * ATTN_WIDTH:4 * ATTN_WIDTH]
    gp = z_ref[:, u_col + POOL_WIDTH:u_col + 2 * POOL_WIDTH]
    mix = jnp.concatenate([o_ref[...] * _silu(ga), pool * _silu(gp)], axis=-1).astype(BF16)
    h = x_ref[...] + jnp.dot(mix, wo_ref[...], preferred_element_type=F32)
    y_ref[...] = _rms(h, fg_ref[...])


def _merge_sample(x_tb, o_tb, z_tb, hist, wp_bf, pool_scale, wo_bf, final_g, n_new, n_batch):
    return pl.pallas_call(
        functools.partial(_merge_sample_kernel, n_new=n_new, n_batch=n_batch),
        out_shape=[jax.ShapeDtypeStruct(x_tb.shape, F32),
                   jax.ShapeDtypeStruct((POOL_HIST, n_batch, POOL_WIDTH), F32)],
        scratch_shapes=[pltpu.VMEM((POOL_HIST + n_new, n_batch, POOL_WIDTH), F32)],
        compiler_params=pltpu.CompilerParams(vmem_limit_bytes=VMEM_LIMIT),
        name="merge_sample",
    )(x_tb, o_tb, z_tb, hist, wp_bf, pool_scale, wo_bf, final_g)


def _to_feature_major(z_cols, n_new, n_batch):
    t = z_cols.reshape(n_new, n_batch, N_HEADS, HEAD_DIM).transpose(1, 2, 3, 0)
    return jnp.pad(t, ((0, 0), (0, 0), (0, 0), (0, LANES - n_new)))


def kernel(x_prompt, x_sample, cache_k, cache_v, state_pool, norm_g, w_in, w_pool, pool_scale, w_out, final_norm_g):
    depth = norm_g.shape[0]
    assert depth == 1, "single-layer step"
    b, s, _ = x_prompt.shape
    n_batch, n_new, _ = x_sample.shape
    buf = cache_k.shape[2]
    assert s % SUPER == 0 and s >= MAX_WINDOW and buf == MAX_WINDOW and n_new <= POOL_HIST

    g = norm_g[0][None, :]
    fg = final_norm_g[None, :]
    wo_bf = w_out[0].astype(BF16)
    wp_bf = w_pool[0].astype(BF16)
    ps = pool_scale[0][None, :]

    x_tb = x_sample.transpose(1, 0, 2).reshape(n_new * n_batch, D_MODEL)
    z_tb, w_bf = _project_sample(x_tb, g, w_in[0])
    q_s = z_tb[:, 0:ATTN_WIDTH].reshape(n_new, n_batch, N_HEADS, HEAD_DIM).transpose(1, 2, 0, 3)
    q_s = jnp.pad(q_s, ((0, 0), (0, 0), (0, SUBLANES - n_new), (0, 0)))
    knt = _to_feature_major(z_tb[:, ATTN_WIDTH:2 * ATTN_WIDTH], n_new, n_batch)
    vnt = _to_feature_major(z_tb[:, 2 * ATTN_WIDTH:3 * ATTN_WIDTH], n_new, n_batch)
    ck = cache_k[0].transpose(0, 2, 3, 1)
    cv = cache_v[0].transpose(0, 2, 3, 1)

    q, k, v, sg, u, sgp, kt, vt, pc, pn, ok = _project_prompt_score_sample(x_prompt, g, w_bf, q_s, knt, ck, n_new)
    mix_a = _attend_prompt(q, k, v, sg)
    y_prompt, o_s, ov = _merge_prompt_value_sample(x_prompt, mix_a, u, sgp, wp_bf, ps, wo_bf, fg,
                                                   pc, pn, vnt, cv, n_new)
    to_window = lambda t: t.reshape(b, N_HEADS, HEAD_DIM, MAX_WINDOW).transpose(0, 3, 1, 2)[None]
    k_prompt, v_prompt = to_window(kt), to_window(vt)
    pool_prompt = u[:, s - POOL_HIST:][None]

    o_tb = o_s[:, :, :n_new].transpose(2, 0, 1, 3).reshape(n_new * n_batch, ATTN_WIDTH)
    hist = state_pool[0].transpose(1, 0, 2)
    y_tb, pool_tb = _merge_sample(x_tb, o_tb, z_tb, hist, wp_bf, ps, wo_bf, fg, n_new, n_batch)
    y_sample = y_tb.reshape(n_new, n_batch, D_MODEL).transpose(1, 0, 2)
    k_sample = ok.transpose(0, 3, 1, 2)[None]
    v_sample = ov.transpose(0, 3, 1, 2)[None]
    pool_sample = pool_tb.transpose(1, 0, 2)[None]

    return (y_prompt, y_sample, k_prompt, v_prompt, pool_prompt, k_sample, v_sample, pool_sample)
```

```python
import functools

import jax
import jax.numpy as jnp
from jax import lax
from jax.experimental import pallas as pl
from jax.experimental.pallas import tpu as pltpu

F32 = jnp.float32
BF16 = jnp.bfloat16

D_MODEL = 2048
ATTN_WIDTH = 1024
HEAD_DIM = 64
N_HEADS = 16
POOL_WIDTH = 1024
POOL_WINDOWS = (2, 4, 8, 16)
POOL_GROUP_WIDTH = 256
POOL_HIST = 15
IN_WIDTH = 6144
DILATED_PATTERNS = ((128, 1), (512, 4), (2048, 16))
MAX_WINDOW = 2048
BLOCK = 128
PAST_LEN = 16384
EPS = 1e-6
ATTN_SCALE = HEAD_DIM ** -0.5
NEG = -1e30
LOG2E = 1.4426950408889634

LANES = 128
SUBLANES = 8
HEAD_PAIR = LANES // HEAD_DIM
HIST_ROWS = 16

PROJ_ROWS = 256
MERGE_ROWS = 512
SUPER = 2048
CAST_COLS = 512
VMEM_LIMIT = 60 * 1024 * 1024


def _silu(x):
    return x / (1.0 + jnp.exp(-x))


def _rms(x, g):
    ms = jnp.mean(x * x, axis=-1, keepdims=True)
    return x * lax.rsqrt(ms + EPS) * g


def _group_cols(g):
    return slice(g * POOL_GROUP_WIDTH, (g + 1) * POOL_GROUP_WIDTH)


def _pooled(window_sum, counts):
    outs = []
    for g, w in enumerate(POOL_WINDOWS):
        cur = window_sum(g, 0)
        tot = cur
        for j in range(1, w):
            tot = tot + window_sum(g, j)
        outs.append(tot / counts[g] - cur)
    return outs


def _pool_project(pooled_group, wp_ref, scale):
    outs = [jnp.dot(pooled_group(g), wp_ref[g], preferred_element_type=F32) for g in range(len(POOL_WINDOWS))]
    return jnp.concatenate(outs, axis=-1) * scale


def _multiplicity(delta):
    c = jnp.zeros(delta.shape, F32)
    for window, d in DILATED_PATTERNS:
        assert d & (d - 1) == 0, "power-of-two dilation: residue test is a bit mask"
        c = c + jnp.where(((delta & (d - 1)) == 0) & (delta <= window), 1.0, 0.0)
    return c


def _shift_cache(src, new, dst_ref, hh, n_new):
    buf = src.shape[-1]
    tail_lane = lax.broadcasted_iota(jnp.int32, (HEAD_DIM, LANES), 1)
    rolled = pltpu.roll(src, buf - n_new, 1)
    tail = rolled[:, buf - LANES:]
    placed = pltpu.roll(new, LANES - n_new, 1)
    dst_ref[0, hh, :, 0:buf - LANES] = rolled[:, 0:buf - LANES]
    dst_ref[0, hh, :, buf - LANES:buf] = jnp.where(tail_lane >= LANES - n_new, placed, tail)


def _sample_scores(q_ref, kn_ref, ck_ref, pc_ref, pn_ref, ok_ref, n_new):
    buf = ck_ref.shape[-1]
    q_rows = q_ref.shape[2]
    qi = lax.broadcasted_iota(jnp.int32, (q_rows, buf), 0)
    cc = _multiplicity(buf + qi - lax.broadcasted_iota(jnp.int32, (q_rows, buf), 1))
    qn = lax.broadcasted_iota(jnp.int32, (q_rows, LANES), 0)
    nl = lax.broadcasted_iota(jnp.int32, (q_rows, LANES), 1)
    cn = jnp.where(nl <= qn, _multiplicity(qn - jnp.minimum(nl, qn)), 0.0)

    for hh in range(ck_ref.shape[1]):
        kc = ck_ref[0, hh]
        knt = kn_ref[0, hh]
        qh = (q_ref[0, hh] * ATTN_SCALE).astype(BF16)
        sc = jnp.dot(qh, kc.astype(BF16), preferred_element_type=F32)
        sn = jnp.dot(qh, knt.astype(BF16), preferred_element_type=F32)
        sc = jnp.where(cc > 0, sc, NEG)
        sn = jnp.where(cn > 0, sn, NEG)
        m = jnp.maximum(jnp.max(sc, axis=1, keepdims=True), jnp.max(sn, axis=1, keepdims=True))
        pc = cc * jnp.exp(sc - m)
        pn = cn * jnp.exp(sn - m)
        den = jnp.sum(pc, axis=1, keepdims=True) + jnp.sum(pn, axis=1, keepdims=True)
        pc_ref[0, hh] = pc / den
        pn_ref[0, hh] = pn / den
        _shift_cache(kc, knt, ok_ref, hh, n_new)


def _sample_values(pc_ref, pn_ref, vn_ref, cv_ref, o_ref, ov_ref, n_new):
    nt = (((1,), (1,)), ((), ()))
    for hh in range(cv_ref.shape[1]):
        vc = cv_ref[0, hh]
        vnt = vn_ref[0, hh]
        o_ref[0, hh] = (
            lax.dot_general(pc_ref[0, hh].astype(BF16), vc.astype(BF16), nt, preferred_element_type=F32)
            + lax.dot_general(pn_ref[0, hh].astype(BF16), vnt.astype(BF16), nt, preferred_element_type=F32))
        _shift_cache(vc, vnt, ov_ref, hh, n_new)


def _hosted_heads(grid, head_groups, heads_per_step):
    def spec(*tail):
        def index(*ids):
            step = 0
            for size, idx in zip(grid, ids, strict=True):
                step = step * size + idx
            return (step // head_groups, step % head_groups, 0, 0)
        return pl.BlockSpec((1, heads_per_step) + tail, index)
    return spec


def _proj_sample_kernel(x_ref, g_ref, w_ref, z_ref, wbf_ref):
    xn = _rms(x_ref[...], g_ref[...]).astype(BF16)
    w = w_ref[...].astype(BF16)
    wbf_ref[...] = w
    z_ref[...] = jnp.dot(xn, w, preferred_element_type=F32)


def _project_sample(x_tb, g, w):
    rows = x_tb.shape[0]
    cols = pl.BlockSpec((D_MODEL, CAST_COLS), lambda c: (0, c))
    return pl.pallas_call(
        _proj_sample_kernel,
        grid=(IN_WIDTH // CAST_COLS,),
        in_specs=[pl.BlockSpec((rows, D_MODEL), lambda c: (0, 0)),
                  pl.BlockSpec((1, D_MODEL), lambda c: (0, 0)),
                  cols],
        out_specs=[pl.BlockSpec((rows, CAST_COLS), lambda c: (0, c)), cols],
        out_shape=[jax.ShapeDtypeStruct((rows, IN_WIDTH), F32),
                   jax.ShapeDtypeStruct((D_MODEL, IN_WIDTH), BF16)],
        compiler_params=pltpu.CompilerParams(dimension_semantics=("arbitrary",),
                                             vmem_limit_bytes=VMEM_LIMIT),
        name="proj_sample",
    )(x_tb, g, w)


def _proj_kernel(x_ref, g_ref, w_ref, qs_ref, kn_ref, ck_ref,
                 q_ref, k_ref, v_ref, sg_ref, u_ref, sgp_ref, kt_ref, vt_ref, pc_ref, pn_ref, ok_ref, *, n_new):
    _sample_scores(qs_ref, kn_ref, ck_ref, pc_ref, pn_ref, ok_ref, n_new)

    xn = _rms(x_ref[0], g_ref[...]).astype(BF16)

    def part(c):
        return jnp.dot(xn, w_ref[:, c * ATTN_WIDTH:(c + 1) * ATTN_WIDTH], preferred_element_type=F32)

    k = part(1)
    k_ref[0] = k
    kt_ref[0] = k.T
    sg_ref[0] = _silu(part(3)).astype(BF16)
    v = part(2)
    v_ref[0] = v
    vt_ref[0] = v.T
    sgp_ref[0] = _silu(part(5)).astype(BF16)
    u_ref[0] = part(4)
    q_ref[0] = part(0) * (ATTN_SCALE * LOG2E)


def _project_prompt_score_sample(x, g, w_bf, q_s, knt, ck, n_new):
    b, s, _ = x.shape
    n_batch, h, dh, buf = ck.shape
    q_rows = q_s.shape[2]
    tiles = s // PROJ_ROWS
    hosted = n_batch * h // (b * tiles)
    assert hosted * b * tiles == n_batch * h and h % hosted == 0, "sample heads split evenly over the grid"
    first_kept = (s - MAX_WINDOW) // PROJ_ROWS
    tok = lambda width: pl.BlockSpec((1, PROJ_ROWS, width), lambda bi, i: (bi, i, 0))
    feat = pl.BlockSpec((1, ATTN_WIDTH, PROJ_ROWS), lambda bi, i: (bi, 0, jnp.maximum(i - first_kept, 0)))
    heads = _hosted_heads((b, tiles), h // hosted, hosted)
    f32 = lambda *shape: jax.ShapeDtypeStruct(shape, F32)
    bf16 = lambda *shape: jax.ShapeDtypeStruct(shape, BF16)
    return pl.pallas_call(
        functools.partial(_proj_kernel, n_new=n_new),
        grid=(b, tiles),
        in_specs=[tok(D_MODEL),
                  pl.BlockSpec((1, D_MODEL), lambda bi, i: (0, 0)),
                  pl.BlockSpec((D_MODEL, IN_WIDTH), lambda bi, i: (0, 0), pipeline_mode=pl.Buffered(1)),
                  heads(q_rows, dh), heads(dh, LANES), heads(dh, buf)],
        out_specs=[tok(ATTN_WIDTH)] * 6 + [feat, feat,
                   heads(q_rows, buf), heads(q_rows, LANES), heads(dh, buf)],
        out_shape=[f32(b, s, ATTN_WIDTH)] * 3
        + [bf16(b, s, ATTN_WIDTH), f32(b, s, POOL_WIDTH), bf16(b, s, POOL_WIDTH)]
        + [f32(b, ATTN_WIDTH, MAX_WINDOW)] * 2
        + [f32(n_batch, h, q_rows, buf), f32(n_batch, h, q_rows, LANES), f32(n_batch, h, dh, buf)],
        compiler_params=pltpu.CompilerParams(dimension_semantics=("arbitrary", "arbitrary"),
                                             vmem_limit_bytes=VMEM_LIMIT),
        name="proj_prompt_score_sample",
    )(x, g, w_bf, q_s, knt, ck)


def _attn_kernel(q_ref, k_ref, v_ref, sg_ref, o_ref, acc, mx, sm, qsel, kmask):
    seq = q_ref.shape[0]

    kk = lax.broadcasted_iota(jnp.int32, (2 * BLOCK, BLOCK), 0)
    qq = lax.broadcasted_iota(jnp.int32, (2 * BLOCK, BLOCK), 1)
    band = (kk - qq >= 0) & (kk - qq <= BLOCK)
    kmask[0] = jnp.where(band, 0.0, NEG).astype(BF16)
    kmask[1] = jnp.where(band & (kk >= BLOCK), 0.0, NEG).astype(BF16)
    row = lax.broadcasted_iota(jnp.int32, (HEAD_PAIR * BLOCK, BLOCK), 0) % BLOCK
    col = lax.broadcasted_iota(jnp.int32, (HEAD_PAIR * BLOCK, BLOCK), 1)
    qsel[...] = jnp.where(row == col, 1.0, 0.0).astype(BF16)

    lane = lax.broadcasted_iota(jnp.int32, (BLOCK, LANES), 1)
    low = lane < HEAD_DIM
    ones = jnp.ones((2 * BLOCK, LANES), BF16)
    nt = (((1,), (1,)), ((), ()))

    def keys(ref, qs, d):
        span = d * BLOCK
        if qs >= span:
            return ref[pl.ds(qs - span, 2 * BLOCK, stride=d), :]
        newer = ref[pl.ds(qs, BLOCK, stride=d), :]
        return jnp.concatenate([newer, newer], axis=0)

    blocks = [[(pat, d, r, j) for r in range(d) for j in range(seq // (d * BLOCK))]
              for pat, (_, d) in enumerate(DILATED_PATTERNS)]
    for pat, d, r, j in [blk for group in zip(blocks[2], blocks[0], blocks[1], strict=True) for blk in group]:
        qs = r + d * BLOCK * j
        rows = pl.ds(qs, BLOCK, stride=d)
        qb = q_ref[rows, :]
        q2 = jnp.concatenate([jnp.where(low, qb, 0.0), jnp.where(low, 0.0, qb)], axis=0).astype(BF16)
        kb = keys(k_ref, qs, d).astype(BF16)
        vb = keys(v_ref, qs, d).astype(BF16)
        s = lax.dot_general(jnp.concatenate([q2, qsel[...]], axis=1),
                            jnp.concatenate([kb, kmask[1 if j == 0 else 0]], axis=1), nt,
                            preferred_element_type=F32)
        m = jnp.max(s, axis=1, keepdims=True)
        p = jnp.exp2(s - m)
        ol = jnp.dot(p.astype(BF16), jnp.concatenate([vb, ones], axis=1),
                     preferred_element_type=F32)
        acc[pat, rows, :] = jnp.where(low, ol[:BLOCK, :LANES], ol[BLOCK:, :LANES])
        mx[pat, rows, :] = jnp.where(low, m[:BLOCK], m[BLOCK:])
        sm[pat, rows, :] = jnp.where(low, ol[:BLOCK, LANES:], ol[BLOCK:, LANES:])

    m_all = jnp.maximum(jnp.maximum(mx[0], mx[1]), mx[2])
    num = jnp.zeros((seq, LANES), F32)
    den = jnp.zeros((seq, LANES), F32)
    for pat in range(len(DILATED_PATTERNS)):
        w = jnp.exp2(mx[pat] - m_all)
        num = num + w * acc[pat]
        den = den + w * sm[pat]
    o_ref[...] = (num / den).astype(BF16) * sg_ref[...]


def _attend_prompt(q, k, v, sg):
    b, s, _ = q.shape
    whole = pl.BlockSpec((None, s, LANES), lambda bi, hp: (bi, 0, hp))
    n_pat = len(DILATED_PATTERNS)
    return pl.pallas_call(
        _attn_kernel,
        grid=(b, N_HEADS // HEAD_PAIR),
        in_specs=[whole] * 4,
        out_specs=whole,
        out_shape=jax.ShapeDtypeStruct((b, s, ATTN_WIDTH), BF16),
        scratch_shapes=[pltpu.VMEM((n_pat, s, LANES), F32),
                        pltpu.VMEM((n_pat, s, LANES), F32),
                        pltpu.VMEM((n_pat, s, LANES), F32),
                        pltpu.VMEM((HEAD_PAIR * BLOCK, BLOCK), BF16),
                        pltpu.VMEM((2, 2 * BLOCK, BLOCK), BF16)],
        compiler_params=pltpu.CompilerParams(dimension_semantics=("arbitrary",) * 2,
                                             vmem_limit_bytes=VMEM_LIMIT),
        name="attn_prompt",
    )(q, k, v, sg)


def _merge_kernel(x_ref, a_ref, u_ref, uh_ref, sgp_ref, wp_ref, ps_ref, wo_ref, fg_ref,
                  pc_ref, pn_ref, vn_ref, cv_ref, y_ref, os_ref, ov_ref, ue, *, n_new):
    i = pl.program_id(1)
    rows = u_ref.shape[1]
    ue[0:HIST_ROWS] = jnp.where(i == 0, 0.0, uh_ref[0])
    ue[HIST_ROWS:HIST_ROWS + rows] = u_ref[0]
    pos = i * rows + lax.broadcasted_iota(jnp.int32, (rows, 1), 0)
    counts = [jnp.minimum(pos + 1, w).astype(F32) for w in POOL_WINDOWS]
    pooled = _pooled(lambda g, j: ue[pl.ds(HIST_ROWS - j, rows), _group_cols(g)], counts)
    pool = _pool_project(lambda g: pooled[g].astype(BF16), wp_ref, ps_ref[...])
    mix = jnp.concatenate([a_ref[0], (pool * sgp_ref[0].astype(F32)).astype(BF16)], axis=-1)
    h = x_ref[0] + jnp.dot(mix, wo_ref[...], preferred_element_type=F32)
    y_ref[0] = _rms(h, fg_ref[...])
    _sample_values(pc_ref, pn_ref, vn_ref, cv_ref, os_ref, ov_ref, n_new)


def _merge_prompt_value_sample(x, mix_a, u, sgp, wp_bf, pool_scale, wo_bf, final_g, pc, pn, vnt, cv, n_new):
    b, s, _ = x.shape
    n_batch, h, dh, buf = cv.shape
    q_rows = pc.shape[2]
    tiles = s // MERGE_ROWS
    hosted = n_batch * h // (b * tiles)
    assert hosted * b * tiles == n_batch * h and h % hosted == 0, "sample heads split evenly over the grid"
    per_tile = MERGE_ROWS // HIST_ROWS
    tok = lambda width: pl.BlockSpec((1, MERGE_ROWS, width), lambda bi, i: (bi, i, 0))
    const = lambda shape: pl.BlockSpec(shape, lambda bi, i: (0,) * len(shape))
    heads = _hosted_heads((b, tiles), h // hosted, hosted)
    return pl.pallas_call(
        functools.partial(_merge_kernel, n_new=n_new),
        grid=(b, tiles),
        in_specs=[tok(D_MODEL), tok(ATTN_WIDTH), tok(POOL_WIDTH),
                  pl.BlockSpec((1, HIST_ROWS, POOL_WIDTH),
                               lambda bi, i: (bi, jnp.maximum(i * per_tile - 1, 0), 0)),
                  tok(POOL_WIDTH),
                  const((len(POOL_WINDOWS), POOL_GROUP_WIDTH, POOL_GROUP_WIDTH)),
                  const((1, POOL_WIDTH)),
                  pl.BlockSpec((D_MODEL, D_MODEL), lambda bi, i: (0, 0), pipeline_mode=pl.Buffered(1)),
                  const((1, D_MODEL)),
                  heads(q_rows, buf), heads(q_rows, LANES), heads(dh, LANES), heads(dh, buf)],
        out_specs=[tok(D_MODEL), heads(q_rows, dh), heads(dh, buf)],
        out_shape=[jax.ShapeDtypeStruct((b, s, D_MODEL), F32),
                   jax.ShapeDtypeStruct((n_batch, h, q_rows, dh), F32),
                   jax.ShapeDtypeStruct((n_batch, h, dh, buf), F32)],
        scratch_shapes=[pltpu.VMEM((HIST_ROWS + MERGE_ROWS, POOL_WIDTH), F32)],
        compiler_params=pltpu.CompilerParams(dimension_semantics=("arbitrary", "arbitrary"),
                                             vmem_limit_bytes=VMEM_LIMIT),
        name="merge_prompt_value_sample",
    )(x, mix_a, u, u, sgp, wp_bf, pool_scale, wo_bf, final_g, pc, pn, vnt, cv)


def _merge_sample_kernel(x_ref, o_ref, z_ref, hist_ref, wp_ref, ps_ref, wo_ref, fg_ref, y_ref, pool_ref, ue,
                         *, n_new, n_batch):
    u_col = 4 * ATTN_WIDTH
    ue[0:POOL_HIST] = hist_ref[...]
    ue[POOL_HIST:POOL_HIST + n_new] = z_ref[:, u_col:u_col + POOL_WIDTH].reshape(n_new, n_batch, POOL_WIDTH)
    pool_ref[...] = ue[n_new:n_new + POOL_HIST]

    rows = n_new * n_batch

    def window_sum(g, j):
        return ue[pl.ds(POOL_HIST - j, n_new), :, _group_cols(g)].reshape(rows, POOL_GROUP_WIDTH)

    pos = PAST_LEN + lax.broadcasted_iota(jnp.int32, (rows, 1), 0) // n_batch
    counts = [jnp.minimum(pos + 1, w).astype(F32) for w in POOL_WINDOWS]
    pooled = _pooled(window_sum, counts)
    pool = _pool_project(lambda g: pooled[g].astype(BF16), wp_ref, ps_ref[...])
    ga = z_ref[:, 3 * ATTN_WIDTH:4 * ATTN_WIDTH]
    gp = z_ref[:, u_col + POOL_WIDTH:u_col + 2 * POOL_WIDTH]
    mix = jnp.concatenate([o_ref[...] * _silu(ga), pool * _silu(gp)], axis=-1).astype(BF16)
    h = x_ref[...] + jnp.dot(mix, wo_ref[...], preferred_element_type=F32)
    y_ref[...] = _rms(h, fg_ref[...])


def _merge_sample(x_tb, o_tb, z_tb, hist, wp_bf, pool_scale, wo_bf, final_g, n_new, n_batch):
    return pl.pallas_call(
        functools.partial(_merge_sample_kernel, n_new=n_new, n_batch=n_batch),
        out_shape=[jax.ShapeDtypeStruct(x_tb.shape, F32),
                   jax.ShapeDtypeStruct((POOL_HIST, n_batch, POOL_WIDTH), F32)],
        scratch_shapes=[pltpu.VMEM((POOL_HIST + n_new, n_batch, POOL_WIDTH), F32)],
        compiler_params=pltpu.CompilerParams(vmem_limit_bytes=VMEM_LIMIT),
        name="merge_sample",
    )(x_tb, o_tb, z_tb, hist, wp_bf, pool_scale, wo_bf, final_g)


def _to_feature_major(z_cols, n_new, n_batch):
    t = z_cols.reshape(n_new, n_batch, N_HEADS, HEAD_DIM).transpose(1, 2, 3, 0)
    return jnp.pad(t, ((0, 0), (0, 0), (0, 0), (0, LANES - n_new)))


def kernel(x_prompt, x_sample, cache_k, cache_v, state_pool, norm_g, w_in, w_pool, pool_scale, w_out, final_norm_g):
    depth = norm_g.shape[0]
    assert depth == 1, "single-layer step"
    b, s, _ = x_prompt.shape
    n_batch, n_new, _ = x_sample.shape
    buf = cache_k.shape[2]
    assert s % SUPER == 0 and s >= MAX_WINDOW and buf == MAX_WINDOW and n_new <= POOL_HIST

    g = norm_g[0][None, :]
    fg = final_norm_g[None, :]
    wo_bf = w_out[0].astype(BF16)
    wp_bf = w_pool[0].astype(BF16)
    ps = pool_scale[0][None, :]

    x_tb = x_sample.transpose(1, 0, 2).reshape(n_new * n_batch, D_MODEL)
    z_tb, w_bf = _project_sample(x_tb, g, w_in[0])
    q_s = z_tb[:, 0:ATTN_WIDTH].reshape(n_new, n_batch, N_HEADS, HEAD_DIM).transpose(1, 2, 0, 3)
    q_s = jnp.pad(q_s, ((0, 0), (0, 0), (0, SUBLANES - n_new), (0, 0)))
    knt = _to_feature_major(z_tb[:, ATTN_WIDTH:2 * ATTN_WIDTH], n_new, n_batch)
    vnt = _to_feature_major(z_tb[:, 2 * ATTN_WIDTH:3 * ATTN_WIDTH], n_new, n_batch)
    ck = cache_k[0].transpose(0, 2, 3, 1)
    cv = cache_v[0].transpose(0, 2, 3, 1)

    q, k, v, sg, u, sgp, kt, vt, pc, pn, ok = _project_prompt_score_sample(x_prompt, g, w_bf, q_s, knt, ck, n_new)
    mix_a = _attend_prompt(q, k, v, sg)
    y_prompt, o_s, ov = _merge_prompt_value_sample(x_prompt, mix_a, u, sgp, wp_bf, ps, wo_bf, fg,
                                                   pc, pn, vnt, cv, n_new)
    to_window = lambda t: t.reshape(b, N_HEADS, HEAD_DIM, MAX_WINDOW).transpose(0, 3, 1, 2)[None]
    k_prompt, v_prompt = to_window(kt), to_window(vt)
    pool_prompt = u[:, s - POOL_HIST:][None]

    o_tb = o_s[:, :, :n_new].transpose(2, 0, 1, 3).reshape(n_new * n_batch, ATTN_WIDTH)
    hist = state_pool[0].transpose(1, 0, 2)
    y_tb, pool_tb = _merge_sample(x_tb, o_tb, z_tb, hist, wp_bf, ps, wo_bf, fg, n_new, n_batch)
    y_sample = y_tb.reshape(n_new, n_batch, D_MODEL).transpose(1, 0, 2)
    k_sample = ok.transpose(0, 3, 1, 2)[None]
    v_sample = ov.transpose(0, 3, 1, 2)[None]
    pool_sample = pool_tb.transpose(1, 0, 2)[None]

    return (y_prompt, y_sample, k_prompt, v_prompt, pool_prompt, k_sample, v_sample, pool_sample)
```
